```python
import math, functools
import jax, jax.numpy as jnp
from jax import lax
import numpy as np

D_MODEL = 2048
BATCH = 4
SEQ = 2048
DEPTH = 4
DEC_BATCH = 8
DEC_SEQ = 1
PAST_LEN = 16384
PAGE_SIZE = 128

EPS = 1e-6
NEG = -1.0e30
A_WIDTH = D_MODEL // 4
A_GROUPS = 4
A_GW = A_WIDTH // A_GROUPS
A_CHUNK = 128
B_HEADS = 4
B_WIDTH = D_MODEL // 4
B_DV = B_WIDTH // B_HEADS
B_DK = B_DV // 2
B_GATE_RANK = 16
B_GATE_TAU = 16.0
B_CHUNK = 64
C_HEADS = 8
C_KV_HEADS = 2
C_HD = 128
C_WIDTH = C_HEADS * C_HD
C_REP = C_HEADS // C_KV_HEADS
C_ROT = C_HD // 4
ROPE_THETA = 500000.0
CMP_LEN = 32
CMP_STRIDE = 16
SLC_BLK = 64
SLC_TOPK = 16
FORCE_BONUS = 1.0e4
WINDOW = 512
SLC_QBLK = 64
WIN_QBLK = 128
P_HEADS = 8
P_NKEYS = 128
P_NEXP = P_NKEYS * P_NKEYS
P_QDIM = 256
P_HALF = P_QDIM // 2
P_TOPK = 16
P_TBLK = 128

IN_WIDTHS = (A_WIDTH, A_WIDTH, B_HEADS * B_DK, B_HEADS * B_DK, B_WIDTH, B_GATE_RANK, B_WIDTH, C_WIDTH, 6 * C_KV_HEADS * C_HD, 3 * C_HEADS, 3 * D_MODEL)
N_IN = sum(IN_WIDTHS)

kernel_name = 'hybrid_gmlp_gla_nsa_peer_step'


def split_points():
    pts, acc = [], 0
    for w in IN_WIDTHS[:-1]:
        acc += w
        pts.append(acc)
    return pts


def rmsnorm(x, g):
    xf = x.astype(jnp.float32)
    y = xf * lax.rsqrt(jnp.mean(xf * xf, axis=-1, keepdims=True) + EPS)
    return (y * g.astype(jnp.float32)).astype(x.dtype)


def layernorm(x, g, b):
    xf = x.astype(jnp.float32)
    mu = jnp.mean(xf, axis=-1, keepdims=True)
    var = jnp.mean(jnp.square(xf - mu), axis=-1, keepdims=True)
    y = (xf - mu) * lax.rsqrt(var + EPS) * g.astype(jnp.float32) + b.astype(jnp.float32)
    return y.astype(x.dtype)


def masked_softmax(s, valid):
    s = jnp.where(valid, s, NEG)
    m = jnp.max(s, axis=-1, keepdims=True)
    e = jnp.where(valid, jnp.exp(s - m), 0.0)
    return e / jnp.maximum(jnp.sum(e, axis=-1, keepdims=True), 1e-30)


def rope(x, pos):
    half = C_ROT // 2
    inv = jnp.float32(ROPE_THETA) ** (-jnp.arange(half, dtype=jnp.float32) / half)
    ang = pos.astype(jnp.float32)[:, None] * inv[None, :]
    shape = (pos.shape[0],) + (1,) * (x.ndim - 3) + (half,)
    cos = jnp.cos(ang).reshape(shape)
    sin = jnp.sin(ang).reshape(shape)
    xr = x[..., :C_ROT].astype(jnp.float32)
    x1, x2 = xr[..., :half], xr[..., half:]
    rot = jnp.concatenate([x1 * cos - x2 * sin, x2 * cos + x1 * sin], axis=-1).astype(x.dtype)
    return jnp.concatenate([rot, x[..., C_ROT:]], axis=-1)


def adaln(c, w, b):
    m = jax.nn.silu(c) @ w + b
    return jnp.split(m[:, None, :], 6, axis=-1)


def chunk_mlp(u, v, ws, bs):
    Bn, L, _ = v.shape
    c = A_CHUNK if L % A_CHUNK == 0 else L
    n = L // c
    w = jnp.where(jnp.tril(jnp.ones((c, c), dtype=bool)), ws[:, :c, :c], 0.0)
    vg = v.reshape(Bn, n, c, A_GROUPS, A_GW)
    mixed = jnp.einsum('gts,bnsgd->bntgd', w.astype(v.dtype), vg) + bs[:, :c].T[None, None, :, :, None]
    return u * mixed.reshape(Bn, L, A_WIDTH)


def gla(q, k, v, log_a, s0):
    Bn, L, H, DK = q.shape
    c = B_CHUNK if L % B_CHUNK == 0 else L
    n = L // c

    def blocks(t):
        return t.astype(jnp.float32).reshape(Bn, n, c, H, -1).transpose(1, 0, 3, 2, 4)

    qc = blocks(q) * (DK ** -0.5)
    kc = blocks(k)
    vc = blocks(v)
    bc = jnp.cumsum(blocks(log_a), axis=3)
    mask = jnp.tril(jnp.ones((c, c), dtype=bool))

    def step(S, inp):
        qi, ki, vi, bi = inp
        qd = qi * jnp.exp(bi)
        kd = ki * jnp.exp(-bi)
        att = jnp.where(mask, jnp.einsum('bhtd,bhsd->bhts', qd, kd), 0.0)
        o = jnp.einsum('bhts,bhsv->bhtv', att, vi) + jnp.einsum('bhtd,bhdv->bhtv', qd, S)
        blast = bi[:, :, -1:, :]
        S = S * jnp.exp(blast[:, :, 0, :])[..., None] + jnp.einsum('bhsd,bhsv->bhdv', ki * jnp.exp(blast - bi), vi)
        return S, o

    S, o = lax.scan(step, s0.astype(jnp.float32), (qc, kc, vc, bc))
    o = o.transpose(1, 0, 3, 2, 4).reshape(Bn, L, H, -1)
    return o.astype(q.dtype), S


def compress(rows, pool, w1, b1, w2, b2):
    Bn, T = rows.shape[:2]
    n_seg = T // CMP_STRIDE
    seg = rows[:, :n_seg * CMP_STRIDE].reshape(Bn, n_seg, CMP_STRIDE, C_KV_HEADS, C_HD)
    first = jnp.einsum('bnigd,i->bngd', seg, pool[:CMP_STRIDE])
    second = jnp.einsum('bnigd,i->bngd', seg, pool[CMP_STRIDE:])
    pooled = first[:, :-1] + second[:, 1:]
    return jax.nn.gelu(pooled @ w1 + b1) @ w2 + b2


def compress_kv(rows, P):
    ck = compress(rows[:, :, 0], P['cmp_pool'][0], P['cmp_w1'][0], P['cmp_b1'][0], P['cmp_w2'][0], P['cmp_b2'][0])
    cv = compress(rows[:, :, 1], P['cmp_pool'][1], P['cmp_w1'][1], P['cmp_b1'][1], P['cmp_w2'][1], P['cmp_b2'][1])
    return ck, cv


def cmp_attend(q, pos, ck, cv):
    Bn, L = q.shape[:2]
    n = ck.shape[1]
    qg = q.reshape(Bn, L, C_KV_HEADS, C_REP, C_HD)
    s = jnp.einsum('blgrd,bngd->bgrln', qg, ck).astype(jnp.float32) * (C_HD ** -0.5)
    end = jnp.arange(n, dtype=jnp.int32) * CMP_STRIDE + (CMP_LEN - 1)
    p = masked_softmax(s, end[None, :] <= pos[:, None])
    o = jnp.einsum('bgrln,bngd->blgrd', p.astype(cv.dtype), cv).reshape(Bn, L, C_HEADS, C_HD)
    return o, jnp.sum(p, axis=2)


def select_blocks(imp, pos, n_slc):
    r4 = SLC_BLK // CMP_STRIDE
    n = imp.shape[-1]
    pp = jnp.pad(imp, ((0, 0), (0, 0), (0, 0), (1, r4 * n_slc + r4 - 1 - n)))
    blk = pp[..., :r4 * n_slc].reshape(pp.shape[:-1] + (n_slc, r4)).sum(-1) + pp[..., r4::r4]
    j = jnp.arange(n_slc, dtype=jnp.int32)[None, :]
    cur = (pos // SLC_BLK)[:, None]
    valid = j * SLC_BLK <= pos[:, None]
    forced = (j == 0) | (j == cur) | (j == cur - 1)
    score = jnp.where(valid, blk + jnp.where(forced, FORCE_BONUS, 0.0), NEG)
    top, idx = lax.top_k(score, min(SLC_TOPK, n_slc))
    return idx, top > 0.5 * NEG


def block_rows(idx):
    return idx[..., None] * SLC_BLK + jnp.arange(SLC_BLK, dtype=jnp.int32)


def slc_attend(q, pos, idx, ok, kb, vb):
    Bn, Q = q.shape[:2]
    K = idx.shape[-1]
    qg = q.reshape(Bn, Q, C_KV_HEADS, C_REP, C_HD)
    s = jnp.einsum('bqgrd,bgqkld->bgrqkl', qg, kb).astype(jnp.float32) * (C_HD ** -0.5)
    valid = ok[..., None] & (block_rows(idx) <= pos[None, None, :, None, None])
    p = masked_softmax(s.reshape(Bn, C_KV_HEADS, C_REP, Q, K * SLC_BLK), valid.reshape(Bn, C_KV_HEADS, 1, Q, K * SLC_BLK))
    p = p.reshape(s.shape).astype(vb.dtype)
    return jnp.einsum('bgrqkl,bgqkld->bqgrd', p, vb).reshape(Bn, Q, C_HEADS, C_HD)


def slc_prompt(q, pos, idx, ok, ks, vs):
    Bn, L = q.shape[:2]
    nqb = L // SLC_QBLK
    bi = jnp.arange(Bn)[:, None, None, None, None]
    gi = jnp.arange(C_KV_HEADS)[None, :, None, None, None]

    def one(args):
        qb, pb, ib, okb = args
        rows = block_rows(ib)
        return slc_attend(qb, pb, ib, okb, ks[bi, rows, gi], vs[bi, rows, gi])

    qs = q.reshape(Bn, nqb, SLC_QBLK, C_HEADS, C_HD).transpose(1, 0, 2, 3, 4)
    ps = pos.reshape(nqb, SLC_QBLK)
    isb = idx.reshape(Bn, C_KV_HEADS, nqb, SLC_QBLK, -1).transpose(2, 0, 1, 3, 4)
    oks = ok.reshape(Bn, C_KV_HEADS, nqb, SLC_QBLK, -1).transpose(2, 0, 1, 3, 4)
    o = lax.map(one, (qs, ps, isb, oks))
    return o.transpose(1, 0, 2, 3, 4).reshape(Bn, L, C_HEADS, C_HD)


def win_prompt(q, pos, kw, vw):
    Bn, L = q.shape[:2]
    qb = WIN_QBLK if L % WIN_QBLK == 0 else L
    nqb = L // qb
    span = WINDOW + qb
    gidx = jnp.arange(nqb, dtype=jnp.int32)[:, None] * qb + jnp.arange(span, dtype=jnp.int32)[None, :]
    padw = ((0, 0), (WINDOW, 0), (0, 0), (0, 0))
    ks = jnp.pad(kw, padw)[:, gidx]
    vs = jnp.pad(vw, padw)[:, gidx]
    qg = q.reshape(Bn, nqb, qb, C_KV_HEADS, C_REP, C_HD)
    s = jnp.einsum('bnqgrd,bnkgd->bgrnqk', qg, ks).astype(jnp.float32) * (C_HD ** -0.5)
    qp = pos.reshape(nqb, qb)[:, :, None]
    kp = (gidx - WINDOW)[:, None, :]
    valid = (kp <= qp) & (kp > qp - WINDOW) & (kp >= 0)
    p = masked_softmax(s, valid)
    return jnp.einsum('bgrnqk,bnkgd->bnqgrd', p.astype(vs.dtype), vs).reshape(Bn, L, C_HEADS, C_HD)


def win_sample(q, pos, kw, vw, buf):
    DB, L = q.shape[:2]
    wb = buf.shape[1]
    kk = jnp.concatenate([buf[:, :, 0], kw], axis=1)
    vv = jnp.concatenate([buf[:, :, 1], vw], axis=1)
    kp = jnp.concatenate([PAST_LEN - wb + jnp.arange(wb, dtype=jnp.int32), pos])
    qg = q.reshape(DB, L, C_KV_HEADS, C_REP, C_HD)
    s = jnp.einsum('blgrd,bkgd->bgrlk', qg, kk).astype(jnp.float32) * (C_HD ** -0.5)
    valid = (kp[None, :] <= pos[:, None]) & (kp[None, :] > pos[:, None] - WINDOW)
    p = masked_softmax(s, valid)
    return jnp.einsum('bgrlk,bkgd->blgrd', p.astype(vv.dtype), vv).reshape(DB, L, C_HEADS, C_HD)


def nsa_prompt(q, kvn, pos, P):
    ck, cv = compress_kv(kvn[:, :, 0:2], P)
    o_cmp, imp = cmp_attend(q, pos, ck, cv)
    idx, ok = select_blocks(imp, pos, q.shape[1] // SLC_BLK)
    o_slc = slc_prompt(q, pos, idx, ok, kvn[:, :, 2], kvn[:, :, 3])
    o_win = win_prompt(q, pos, kvn[:, :, 4], kvn[:, :, 5])
    return o_cmp, o_slc, o_win


def nsa_sample(q, kvn, pos, P, cache, layer, win_buf, page_table):
    DB, L = q.shape[:2]
    n_past_blk = PAST_LEN // SLC_BLK
    bpp = PAGE_SIZE // SLC_BLK
    past = cache[layer, page_table, :, 0:2].reshape(DB, -1, 2, C_KV_HEADS, C_HD)
    ck, cv = compress_kv(jnp.concatenate([past, kvn[:, :, 0:2]], axis=1), P)
    o_cmp, imp = cmp_attend(q, pos, ck, cv)
    idx, ok = select_blocks(imp, pos, -(-(PAST_LEN + L) // SLC_BLK))
    bi = jnp.arange(DB)[:, None, None, None, None]
    gi = jnp.arange(C_KV_HEADS)[None, :, None, None, None]
    jc = jnp.minimum(idx, n_past_blk - 1)
    phys = page_table[jnp.arange(DB)[:, None, None, None], jc // bpp][..., None]
    prow = block_rows(jc % bpp)
    k_old = cache[layer, phys, prow, 2, gi]
    v_old = cache[layer, phys, prow, 3, gi]
    nb = -(-L // SLC_BLK)
    padn = ((0, 0), (0, nb * SLC_BLK - L), (0, 0), (0, 0))
    newk = jnp.pad(kvn[:, :, 2], padn)
    newv = jnp.pad(kvn[:, :, 3], padn)
    nrow = block_rows(jnp.clip(idx - n_past_blk, 0, nb - 1))
    is_new = (idx >= n_past_blk)[..., None, None]
    kb = jnp.where(is_new, newk[bi, nrow, gi], k_old)
    vb = jnp.where(is_new, newv[bi, nrow, gi], v_old)
    o_slc = slc_attend(q, pos, idx, ok, kb, vb)
    o_win = win_sample(q, pos, kvn[:, :, 4], kvn[:, :, 5], win_buf)
    return o_cmp, o_slc, o_win


def peer(x, wq, qn_g, subkeys, u_tab, v_tab):
    T, D = x.shape
    q = rmsnorm((x @ wq).reshape(T, P_HEADS, 2, P_HALF), qn_g)
    s = jnp.einsum('thcd,hcnd->thcn', q, subkeys).astype(jnp.float32)
    s1, i1 = lax.top_k(s[:, :, 0], P_TOPK)
    s2, i2 = lax.top_k(s[:, :, 1], P_TOPK)
    cand = (s1[..., :, None] + s2[..., None, :]).reshape(T, P_HEADS, P_TOPK * P_TOPK)
    cidx = (i1[..., :, None] * P_NKEYS + i2[..., None, :]).reshape(T, P_HEADS, P_TOPK * P_TOPK)
    top, j = lax.top_k(cand, P_TOPK)
    eidx = jnp.take_along_axis(cidx, j, axis=-1)
    g = jax.nn.softmax(top, axis=-1)
    tb = min(P_TBLK, T)
    nblk = -(-T // tb)
    pad = nblk * tb - T
    xb = jnp.pad(x, ((0, pad), (0, 0))).reshape(nblk, tb, D)
    eb = jnp.pad(eidx, ((0, pad), (0, 0), (0, 0))).reshape(nblk, tb, P_HEADS, P_TOPK)
    gb = jnp.pad(g, ((0, pad), (0, 0), (0, 0))).reshape(nblk, tb, P_HEADS, P_TOPK)

    def one(args):
        xt, et, gt = args
        hid = jax.nn.gelu(jnp.einsum('td,thkd->thk', xt, u_tab[et]).astype(jnp.float32))
        return jnp.einsum('thk,thkd->td', (gt * hid).astype(xt.dtype), v_tab[et])

    return lax.map(one, (xb, eb, gb)).reshape(nblk * tb, D)[:T]


def trunk_layer(x, c, pos, P, nsa_fn, gla_s0):
    Bn, L, _ = x.shape
    sh1, sc1, g1, sh2, sc2, g2 = adaln(c, P['ada_w'], P['ada_b'])
    h = rmsnorm(x, P['norm1_g']) * (1.0 + sc1) + sh1
    z = h @ P['w_in']
    a_u, a_v, b_q, b_k, b_v, b_g, b_r, c_q, c_kv, c_g, m_g = jnp.split(z, split_points(), axis=-1)
    a_u = jax.nn.gelu(a_u)
    a_v = layernorm(jax.nn.gelu(a_v), P['a_ln_g'], P['a_ln_b'])
    o_a = chunk_mlp(a_u, a_v, P['a_ws'], P['a_bs'])
    log_a = jax.nn.log_sigmoid((b_g @ P['b_gw2'] + P['b_gb']).astype(jnp.float32)) / B_GATE_TAU
    o_b, s_b = gla(b_q.reshape(Bn, L, B_HEADS, B_DK), b_k.reshape(Bn, L, B_HEADS, B_DK), b_v.reshape(Bn, L, B_HEADS, B_DV), log_a.reshape(Bn, L, B_HEADS, B_DK), gla_s0)
    o_b = rmsnorm(o_b, P['b_on_g']).reshape(Bn, L, B_WIDTH) * jax.nn.silu(b_r)
    q = rope(rmsnorm(c_q.reshape(Bn, L, C_HEADS, C_HD), P['c_qn_g']), pos)
    kv6 = c_kv.reshape(Bn, L, 6, C_KV_HEADS, C_HD)
    keys = rope(rmsnorm(kv6[:, :, 0::2], P['c_kn_g'][:, None, :]), pos)
    kvn = jnp.stack([keys, kv6[:, :, 1::2]], axis=3).reshape(Bn, L, 6, C_KV_HEADS, C_HD)
    o_cmp, o_slc, o_win = nsa_fn(q, kvn, pos, P)
    gc = jax.nn.sigmoid(c_g).reshape(Bn, L, 3, C_HEADS, 1)
    o_c = (gc[:, :, 0] * o_cmp + gc[:, :, 1] * o_slc + gc[:, :, 2] * o_win).reshape(Bn, L, C_WIDTH)
    gm = jax.nn.sigmoid(m_g).reshape(Bn, L, 3, D_MODEL)
    merged = gm[:, :, 0] * (o_a @ P['w_br_a']) + gm[:, :, 1] * (o_b @ P['w_br_b']) + gm[:, :, 2] * (o_c @ P['w_br_c'])
    x = x + g1 * (merged @ P['w_out'])
    h2 = rmsnorm(x, P['norm2_g']) * (1.0 + sc2) + sh2
    y = peer(h2.reshape(Bn * L, D_MODEL), P['p_wq'], P['p_qn_g'], P['p_subkeys'], P['p_u'], P['p_v'])
    x = x + g2 * y.reshape(Bn, L, D_MODEL)
    return x, a_v, s_b, kvn


def setup_inputs(seed: int = 0) -> dict:
    key = jax.random.key(seed)
    keys = list(jax.random.split(key, 40))

    def nrm(i, shape, scale):
        return jax.random.normal(keys[i], shape, jnp.float32) * scale

    def gain(i, shape):
        return 1.0 + nrm(i, shape, 0.05)

    n_pages = PAST_LEN // PAGE_SIZE
    n_used = DEC_BATCH * n_pages
    n_pool = n_used + max(1, n_used // 4)
    wb = min(WINDOW, PAST_LEN)
    page_table = jax.random.permutation(keys[39], n_pool)[:n_used].reshape(DEC_BATCH, n_pages).astype(jnp.int32)
    return {
        'x_prompt': nrm(0, (BATCH, SEQ, D_MODEL), 1.0),
        'x_sample': nrm(1, (DEC_BATCH, DEC_SEQ, D_MODEL), 1.0),
        'cache_nsa_kv': nrm(2, (DEPTH, n_pool, PAGE_SIZE, 4, C_KV_HEADS, C_HD), 1.0),
        'state_win_kv': nrm(3, (DEPTH, DEC_BATCH, wb, 2, C_KV_HEADS, C_HD), 1.0),
        'state_gla': nrm(4, (DEPTH, DEC_BATCH, B_HEADS, B_DK, B_DV), 0.3),
        'page_table': page_table,
        'c_prompt': nrm(5, (BATCH, D_MODEL), 1.0),
        'c_sample': nrm(6, (DEC_BATCH, D_MODEL), 1.0),
        'ada_w': nrm(7, (DEPTH, D_MODEL, 6 * D_MODEL), 0.5 * D_MODEL ** -0.5),
        'ada_b': nrm(8, (DEPTH, 6 * D_MODEL), 0.01),
        'norm1_g': gain(9, (DEPTH, D_MODEL)),
        'norm2_g': gain(10, (DEPTH, D_MODEL)),
        'w_in': nrm(11, (DEPTH, D_MODEL, N_IN), D_MODEL ** -0.5),
        'a_ln_g': gain(12, (DEPTH, A_WIDTH)),
        'a_ln_b': nrm(13, (DEPTH, A_WIDTH), 0.02),
        'a_ws': nrm(14, (DEPTH, A_GROUPS, A_CHUNK, A_CHUNK), A_CHUNK ** -0.5),
        'a_bs': gain(15, (DEPTH, A_GROUPS, A_CHUNK)),
        'b_gw2': nrm(16, (DEPTH, B_GATE_RANK, B_HEADS * B_DK), B_GATE_RANK ** -0.5),
        'b_gb': nrm(17, (DEPTH, B_HEADS * B_DK), 0.02),
        'b_on_g': gain(18, (DEPTH, B_DV)),
        'c_qn_g': gain(19, (DEPTH, C_HD)),
        'c_kn_g': gain(20, (DEPTH, 3, C_HD)),
        'cmp_pool': (1.0 + nrm(21, (DEPTH, 2, CMP_LEN), 0.1)) * CMP_LEN ** -0.5,
        'cmp_w1': nrm(22, (DEPTH, 2, C_HD, C_HD), C_HD ** -0.5),
        'cmp_b1': nrm(23, (DEPTH, 2, C_HD), 0.02),
        'cmp_w2': nrm(24, (DEPTH, 2, C_HD, C_HD), 2.0 * C_HD ** -0.5),
        'cmp_b2': nrm(25, (DEPTH, 2, C_HD), 0.02),
        'w_br_a': nrm(26, (DEPTH, A_WIDTH, D_MODEL), A_WIDTH ** -0.5),
        'w_br_b': nrm(27, (DEPTH, B_WIDTH, D_MODEL), B_WIDTH ** -0.5),
        'w_br_c': nrm(28, (DEPTH, C_WIDTH, D_MODEL), C_WIDTH ** -0.5),
        'w_out': nrm(29, (DEPTH, D_MODEL, D_MODEL), D_MODEL ** -0.5),
        'p_wq': nrm(30, (DEPTH, D_MODEL, P_HEADS * P_QDIM), D_MODEL ** -0.5),
        'p_qn_g': gain(31, (DEPTH, P_HALF)),
        'p_subkeys': nrm(32, (DEPTH, P_HEADS, 2, P_NKEYS, P_HALF), P_HALF ** -0.5),
        'p_u': nrm(33, (DEPTH, P_NEXP, D_MODEL), D_MODEL ** -0.5),
        'p_v': nrm(34, (DEPTH, P_NEXP, D_MODEL), 0.25),
    }


def reference(x_prompt, x_sample, cache_nsa_kv, state_win_kv, state_gla, page_table, c_prompt, c_sample, ada_w, ada_b, norm1_g, norm2_g, w_in, a_ln_g, a_ln_b, a_ws, a_bs, b_gw2, b_gb, b_on_g, c_qn_g, c_kn_g, cmp_pool, cmp_w1, cmp_b1, cmp_w2, cmp_b2, w_br_a, w_br_b, w_br_c, w_out, p_wq, p_qn_g, p_subkeys, p_u, p_v):
    Bp, Lp = x_prompt.shape[:2]
    Ls = x_sample.shape[1]
    pos_p = jnp.arange(Lp, dtype=jnp.int32)
    pos_s = PAST_LEN + jnp.arange(Ls, dtype=jnp.int32)
    wb_p = min(WINDOW, Lp)
    gla_zero = jnp.zeros((Bp, B_HEADS, B_DK, B_DV), jnp.float32)
    xp, xs = x_prompt, x_sample
    kv_p, kv_s, win_p, win_s, gla_p, gla_s, cv_s = [], [], [], [], [], [], []
    for l in range(DEPTH):
        P = {
            'ada_w': ada_w[l], 'ada_b': ada_b[l], 'norm1_g': norm1_g[l], 'norm2_g': norm2_g[l], 'w_in': w_in[l],
            'a_ln_g': a_ln_g[l], 'a_ln_b': a_ln_b[l], 'a_ws': a_ws[l], 'a_bs': a_bs[l],
            'b_gw2': b_gw2[l], 'b_gb': b_gb[l], 'b_on_g': b_on_g[l],
            'c_qn_g': c_qn_g[l], 'c_kn_g': c_kn_g[l], 'cmp_pool': cmp_pool[l], 'cmp_w1': cmp_w1[l], 'cmp_b1': cmp_b1[l],
            'cmp_w2': cmp_w2[l], 'cmp_b2': cmp_b2[l],
            'w_br_a': w_br_a[l], 'w_br_b': w_br_b[l], 'w_br_c': w_br_c[l], 'w_out': w_out[l],
            'p_wq': p_wq[l], 'p_qn_g': p_qn_g[l], 'p_subkeys': p_subkeys[l], 'p_u': p_u[l], 'p_v': p_v[l],
        }
        xp, _, sp, kvp = trunk_layer(xp, c_prompt, pos_p, P, nsa_prompt, gla_zero)
        nsa_s = functools.partial(nsa_sample, cache=cache_nsa_kv, layer=l, win_buf=state_win_kv[l], page_table=page_table)
        xs, avs, ss, kvs = trunk_layer(xs, c_sample, pos_s, P, nsa_s, state_gla[l])
        kv_p.append(kvp[:, :, :4])
        win_p.append(kvp[:, Lp - wb_p:, 4:])
        gla_p.append(sp)
        kv_s.append(kvs[:, :, :4])
        win_s.append(kvs[:, :, 4:])
        gla_s.append(ss)
        cv_s.append(avs)
    return (xp, xs, jnp.stack(kv_p), jnp.stack(kv_s), jnp.stack(win_p), jnp.stack(win_s), jnp.stack(gla_p), jnp.stack(gla_s), jnp.stack(cv_s))
```

```python
import functools
import math

import numpy as np
import jax
import jax.numpy as jnp
from jax import lax
from jax.experimental import pallas as pl
from jax.experimental.pallas import tpu as pltpu

F32 = jnp.float32
BF16 = jnp.bfloat16

D_MODEL = 2048
BATCH = 4
SEQ = 2048
DEPTH = 4
DEC_BATCH = 8
PAST_LEN = 16384
PAGE_SIZE = 128
EPS = 1e-6
NEG = -1.0e30
A_WIDTH = 512
A_GROUPS = 4
A_GW = 128
A_CHUNK = 128
B_HEADS = 4
B_WIDTH = 512
B_DV = 128
B_DK = 64
B_GATE_RANK = 16
B_GATE_TAU = 16.0
B_CHUNK = 64
C_HEADS = 8
C_KV_HEADS = 2
C_HD = 128
C_WIDTH = 1024
C_REP = 4
C_ROT = 32
ROPE_THETA = 500000.0
CMP_LEN = 32
CMP_STRIDE = 16
SLC_BLK = 64
SLC_TOPK = 16
FORCE_BONUS = 1.0e4
WINDOW = 512
P_HEADS = 8
P_NKEYS = 128
P_NEXP = P_NKEYS * P_NKEYS
P_HALF = 128
P_TOPK = 16

T_P = BATCH * SEQ
T_S = 128
N_SEQ_ROWS = 8

Z_CQ = 0
Z_AU = 1024
Z_AV = 1536
Z_BV = 2048
Z_BR = 2560
Z_CKV = 3072
Z_BQ = 4608
Z_BK = 4864
Z_MG = 5120
Z1_W = 11264
Z2_W = 128
Z2_CG = 16

VMEM_LIMIT = 56 * 1024 * 1024


def _cp(sem, vmem=VMEM_LIMIT):
    return pltpu.CompilerParams(dimension_semantics=sem, vmem_limit_bytes=vmem)


def _gelu(x):
    return 0.5 * x * (1.0 + jnp.tanh(0.7978845608028654 * (x + 0.044715 * x * x * x)))


def _sigmoid(x):
    return 1.0 / (1.0 + jnp.exp(-x))


def _log_sigmoid(x):
    return jnp.minimum(x, 0.0) - jnp.log(1.0 + jnp.exp(-jnp.abs(x)))


def _dot(a, b):
    return jnp.dot(a, b, preferred_element_type=F32)


def _dot_nt(a, b):
    return lax.dot_general(a, b, (((1,), (1,)), ((), ())), preferred_element_type=F32)


def _dot_tn(a, b):
    return lax.dot_general(a, b, (((0,), (0,)), ((), ())), preferred_element_type=F32)


def _split3(v):
    hi = v.astype(BF16)
    r1 = v - hi.astype(F32)
    mid = r1.astype(BF16)
    r2 = r1 - mid.astype(F32)
    lo = r2.astype(BF16)
    return hi, mid, lo


def _dot_exact_lhs(sel_bf16, v):
    hi, mid, lo = _split3(v)
    return (_dot(sel_bf16, hi) + _dot(sel_bf16, mid)) + _dot(sel_bf16, lo)


def _dot_exact_rhs(v, sel_bf16):
    hi, mid, lo = _split3(v)
    return (_dot(hi, sel_bf16) + _dot(mid, sel_bf16)) + _dot(lo, sel_bf16)


def _rms(x, g):
    return x * lax.rsqrt(jnp.mean(x * x, axis=-1, keepdims=True) + EPS) * g


def _mod_rows(y, scale8, shift8=None):
    tm, n = y.shape
    y3 = y.reshape(tm // N_SEQ_ROWS, N_SEQ_ROWS, n) * scale8[None]
    if shift8 is not None:
        y3 = y3 + shift8[None]
    return y3.reshape(tm, n)


def _adaln_kernel(c_ref, w_ref, b_ref, o_ref):
    c = c_ref[...]
    s = (c * _sigmoid(c)).astype(BF16)
    o_ref[...] = _dot(s, w_ref[...].astype(BF16)) + b_ref[...]


def adaln_table(c40, ada_w, ada_b):
    tn = 1024
    n = 6 * D_MODEL
    return pl.pallas_call(
        _adaln_kernel,
        grid=(DEPTH, n // tn),
        in_specs=[
            pl.BlockSpec((5 * N_SEQ_ROWS, D_MODEL), lambda l, j: (0, 0)),
            pl.BlockSpec((None, D_MODEL, tn), lambda l, j: (l, 0, j)),
            pl.BlockSpec((None, 1, tn), lambda l, j: (l, 0, j)),
        ],
        out_specs=pl.BlockSpec((None, 5 * N_SEQ_ROWS, tn), lambda l, j: (l, 0, j)),
        out_shape=jax.ShapeDtypeStruct((DEPTH, 5 * N_SEQ_ROWS, n), F32),
        compiler_params=_cp(("arbitrary", "arbitrary")),
        name="adaln_table",
    )(c40, ada_w, ada_b.reshape(DEPTH, 1, n))


def _seq_of(tm, is_sample):
    if is_sample:
        return lambda i: 4
    per = SEQ // tm
    return lambda i: i // per


def _norm_mod_kernel(x_ref, g_ref, sh_ref, sc_ref, h_ref):
    y = _rms(x_ref[...], g_ref[...])
    h_ref[...] = _mod_rows(y, 1.0 + sc_ref[...], sh_ref[...]).astype(h_ref.dtype)


def norm_mod(x, g, mod, k_shift, k_scale, tm, is_sample):
    t = x.shape[0]
    sq = _seq_of(tm, is_sample)
    return pl.pallas_call(
        _norm_mod_kernel,
        grid=(t // tm,),
        in_specs=[
            pl.BlockSpec((tm, D_MODEL), lambda i: (i, 0)),
            pl.BlockSpec((1, D_MODEL), lambda i: (0, 0)),
            pl.BlockSpec((N_SEQ_ROWS, D_MODEL), lambda i: (sq(i), k_shift)),
            pl.BlockSpec((N_SEQ_ROWS, D_MODEL), lambda i: (sq(i), k_scale)),
        ],
        out_specs=pl.BlockSpec((tm, D_MODEL), lambda i: (i, 0)),
        out_shape=jax.ShapeDtypeStruct((t, D_MODEL), BF16),
        compiler_params=_cp(("arbitrary",)),
        name="norm_mod",
    )(x, g, mod, mod)


def _resid_norm_mod_kernel(x_ref, y_ref, gate_ref, g_ref, sh_ref, sc_ref, xo_ref, h_ref):
    xn = x_ref[...] + _mod_rows(y_ref[...], gate_ref[...])
    xo_ref[...] = xn
    h_ref[...] = _mod_rows(_rms(xn, g_ref[...]), 1.0 + sc_ref[...], sh_ref[...]).astype(h_ref.dtype)


def resid_norm_mod(x, y, g, mod_gate, mod, k_gate, k_shift, k_scale, tm, is_sample):
    t = x.shape[0]
    sq = _seq_of(tm, is_sample)
    row = pl.BlockSpec((tm, D_MODEL), lambda i: (i, 0))
    return pl.pallas_call(
        _resid_norm_mod_kernel,
        grid=(t // tm,),
        in_specs=[
            row,
            row,
            pl.BlockSpec((N_SEQ_ROWS, D_MODEL), lambda i: (sq(i), k_gate)),
            pl.BlockSpec((1, D_MODEL), lambda i: (0, 0)),
            pl.BlockSpec((N_SEQ_ROWS, D_MODEL), lambda i: (sq(i), k_shift)),
            pl.BlockSpec((N_SEQ_ROWS, D_MODEL), lambda i: (sq(i), k_scale)),
        ],
        out_specs=[row, row],
        out_shape=[jax.ShapeDtypeStruct((t, D_MODEL), F32), jax.ShapeDtypeStruct((t, D_MODEL), BF16)],
        compiler_params=_cp(("arbitrary",)),
        name="resid_norm_mod",
    )(x, y, mod_gate, g, mod, mod)


def _resid_kernel(x_ref, y_ref, gate_ref, xo_ref):
    xo_ref[...] = x_ref[...] + _mod_rows(y_ref[...], gate_ref[...])


def resid(x, y, mod, k_gate, tm, is_sample):
    t = x.shape[0]
    sq = _seq_of(tm, is_sample)
    row = pl.BlockSpec((tm, D_MODEL), lambda i: (i, 0))
    return pl.pallas_call(
        _resid_kernel,
        grid=(t // tm,),
        in_specs=[row, row, pl.BlockSpec((N_SEQ_ROWS, D_MODEL), lambda i: (sq(i), k_gate))],
        out_specs=row,
        out_shape=jax.ShapeDtypeStruct((t, D_MODEL), F32),
        compiler_params=_cp(("arbitrary",)),
        name="resid",
    )(x, y, mod)


def _mm_kernel(x_ref, w_ref, o_ref):
    o_ref[...] = _dot(x_ref[...], w_ref[...]).astype(o_ref.dtype)


def matmul(x, w, tm, tn, out_dtype, name):
    m, k = x.shape
    n = w.shape[1]
    return pl.pallas_call(
        _mm_kernel,
        grid=(n // tn, m // tm),
        in_specs=[
            pl.BlockSpec((tm, k), lambda j, i: (i, 0)),
            pl.BlockSpec((k, tn), lambda j, i: (0, j)),
        ],
        out_specs=pl.BlockSpec((tm, tn), lambda j, i: (i, j)),
        out_shape=jax.ShapeDtypeStruct((m, n), out_dtype),
        compiler_params=_cp(("arbitrary", "arbitrary")),
        name=name,
    )(x, w)


def _wq_kernel(x_ref, w_ref, g_ref, o_ref):
    acc = _dot(x_ref[...], w_ref[...])
    g = g_ref[...]
    for c in range(acc.shape[1] // P_HALF):
        a = acc[:, c * P_HALF:(c + 1) * P_HALF]
        o_ref[:, c * P_HALF:(c + 1) * P_HALF] = _rms(a, g).astype(o_ref.dtype)


def peer_query(h2, wq, qn_g, tm):
    m = h2.shape[0]
    tn = 512
    return pl.pallas_call(
        _wq_kernel,
        grid=(D_MODEL // tn, m // tm),
        in_specs=[
            pl.BlockSpec((tm, D_MODEL), lambda j, i: (i, 0)),
            pl.BlockSpec((D_MODEL, tn), lambda j, i: (0, j)),
            pl.BlockSpec((1, P_HALF), lambda j, i: (0, 0)),
        ],
        out_specs=pl.BlockSpec((tm, tn), lambda j, i: (i, j)),
        out_shape=jax.ShapeDtypeStruct((m, D_MODEL), BF16),
        compiler_params=_cp(("arbitrary", "arbitrary")),
        name="peer_query",
    )(h2, wq, qn_g)


def _qk_prep_kernel(zq_ref, zkv_ref, gq_ref, gk_ref, c_ref, s1_ref, s2_ref, qn_ref, kvn_ref):
    cos = c_ref[...]
    s1 = s1_ref[...]
    s2 = s2_ref[...]

    def norm_rope(x, g):
        xn = _rms(x, g)
        return xn * cos + pltpu.roll(xn, C_HD - C_ROT // 2, 1) * s1 + pltpu.roll(xn, C_ROT // 2, 1) * s2

    gq = gq_ref[...]
    for h in range(C_HEADS):
        x = zq_ref[:, h * C_HD:(h + 1) * C_HD].astype(F32)
        qn_ref[:, h * C_HD:(h + 1) * C_HD] = norm_rope(x, gq).astype(qn_ref.dtype)
    for s in range(6):
        for g in range(C_KV_HEADS):
            c0 = (s * C_KV_HEADS + g) * C_HD
            x = zkv_ref[:, c0:c0 + C_HD].astype(F32)
            if s % 2 == 0:
                x = norm_rope(x, gk_ref[s // 2:s // 2 + 1, :])
            kvn_ref[:, c0:c0 + C_HD] = x


def qk_prep(z1, gq, gk, rope_c, rope_s1, rope_s2, tm, is_sample):
    t = z1.shape[0]
    per = 1 if is_sample else SEQ // tm
    tab = pl.BlockSpec((tm, C_HD), lambda i: (i % per, 0))
    return pl.pallas_call(
        _qk_prep_kernel,
        grid=(t // tm,),
        in_specs=[
            pl.BlockSpec((tm, C_WIDTH), lambda i: (i, Z_CQ // C_WIDTH)),
            pl.BlockSpec((tm, 1536), lambda i: (i, Z_CKV // 1536)),
            pl.BlockSpec((1, C_HD), lambda i: (0, 0)),
            pl.BlockSpec((3, C_HD), lambda i: (0, 0)),
            tab, tab, tab,
        ],
        out_specs=[
            pl.BlockSpec((tm, C_WIDTH), lambda i: (i, 0)),
            pl.BlockSpec((tm, 1536), lambda i: (i, 0)),
        ],
        out_shape=[jax.ShapeDtypeStruct((t, C_WIDTH), BF16), jax.ShapeDtypeStruct((t, 1536), F32)],
        compiler_params=_cp(("arbitrary",)),
        name="qk_prep",
    )(z1, z1, gq, gk, rope_c, rope_s1, rope_s2)


def _chunk_mlp_kernel(au_ref, av_ref, lng_ref, lnb_ref, ws_ref, bst_ref, o_ref):
    u = _gelu(au_ref[...].astype(F32))
    v = _gelu(av_ref[...].astype(F32))
    mu = jnp.mean(v, axis=-1, keepdims=True)
    vc = v - mu
    var = jnp.mean(vc * vc, axis=-1, keepdims=True)
    v = vc * lax.rsqrt(var + EPS) * lng_ref[...] + lnb_ref[...]
    r = lax.broadcasted_iota(jnp.int32, (A_CHUNK, A_CHUNK), 0)
    c = lax.broadcasted_iota(jnp.int32, (A_CHUNK, A_CHUNK), 1)
    tril = r >= c
    tm = u.shape[0]
    for g in range(A_GROUPS):
        w = jnp.where(tril, ws_ref[g], 0.0).astype(BF16)
        bias = bst_ref[:, g:g + 1]
        for ch in range(tm // A_CHUNK):
            rows = slice(ch * A_CHUNK, (ch + 1) * A_CHUNK)
            cols = slice(g * A_GW, (g + 1) * A_GW)
            mixed = _dot(w, v[rows, cols].astype(BF16)) + bias
            o_ref[rows, cols] = (u[rows, cols] * mixed).astype(o_ref.dtype)


def chunk_mlp_prompt(z1, ln_g, ln_b, ws, bs_t):
    tm = 512
    return pl.pallas_call(
        _chunk_mlp_kernel,
        grid=(T_P // tm,),
        in_specs=[
            pl.BlockSpec((tm, A_WIDTH), lambda i: (i, Z_AU // A_WIDTH)),
            pl.BlockSpec((tm, A_WIDTH), lambda i: (i, Z_AV // A_WIDTH)),
            pl.BlockSpec((1, A_WIDTH), lambda i: (0, 0)),
            pl.BlockSpec((1, A_WIDTH), lambda i: (0, 0)),
            pl.BlockSpec((A_GROUPS, A_CHUNK, A_CHUNK), lambda i: (0, 0, 0)),
            pl.BlockSpec((A_CHUNK, A_GROUPS), lambda i: (0, 0)),
        ],
        out_specs=pl.BlockSpec((tm, A_WIDTH), lambda i: (i, 0)),
        out_shape=jax.ShapeDtypeStruct((T_P, A_WIDTH), BF16),
        compiler_params=_cp(("arbitrary",)),
        name="chunk_mlp",
    )(z1, z1, ln_g, ln_b, ws, bs_t)


def _col_from_row(row):
    n = row.shape[1]
    r = lax.broadcasted_iota(jnp.int32, (n, n), 0)
    c = lax.broadcasted_iota(jnp.int32, (n, n), 1)
    return jnp.sum(jnp.where(r == c, jnp.broadcast_to(row, (n, n)), 0.0), axis=1, keepdims=True)


def _gla_kernel(q_ref, k_ref, v_ref, r_ref, bg_ref, gw_ref, gb_ref, on_ref, o_ref, s_out_ref, s_sc):
    n = pl.program_id(1)

    @pl.when(n == 0)
    def _():
        s_sc[...] = jnp.zeros_like(s_sc)

    c = B_CHUNK
    ri = lax.broadcasted_iota(jnp.int32, (c, c), 0)
    ci = lax.broadcasted_iota(jnp.int32, (c, c), 1)
    tril = ri >= ci
    eye = jnp.where(ri == ci, 1.0, 0.0).astype(BF16)
    la = _log_sigmoid(_dot(bg_ref[...].astype(BF16), gw_ref[...]) + gb_ref[...]) * (1.0 / B_GATE_TAU)
    bc = _dot_exact_lhs(jnp.where(tril, 1.0, 0.0).astype(BF16), la)
    q = q_ref[...].astype(F32) * (B_DK ** -0.5)
    k = k_ref[...].astype(F32)
    v = v_ref[...].astype(BF16)
    gate = r_ref[...].astype(F32)
    gate = gate * _sigmoid(gate)
    on = on_ref[...]
    for h in range(B_HEADS):
        dk = slice(h * B_DK, (h + 1) * B_DK)
        dv = slice(h * B_DV, (h + 1) * B_DV)
        b = bc[:, dk]
        qd = (q[:, dk] * jnp.exp(b)).astype(BF16)
        kd = (k[:, dk] * jnp.exp(-b)).astype(BF16)
        att = jnp.where(tril, _dot_nt(qd, kd), 0.0)
        s_old = s_sc[h]
        o = _dot(att.astype(BF16), v[:, dv]) + _dot(qd, s_old.astype(BF16))
        blast = b[c - 1:c, :]
        kdec = (k[:, dk] * jnp.exp(blast - b)).astype(BF16)
        kdec_t = _dot_nt(eye, kdec).astype(BF16)
        s_new = s_old * _col_from_row(jnp.exp(blast)) + _dot(kdec_t, v[:, dv])
        s_sc[h] = s_new
        o_ref[:, dv] = (_rms(o, on) * gate[:, dv]).astype(o_ref.dtype)

    @pl.when(n == pl.num_programs(1) - 1)
    def _():
        s_out_ref[...] = s_sc[...]


def gla_prompt(z1, z2, gw2p, gb, on_g):
    c = B_CHUNK
    nch = SEQ // c
    row = lambda b, n: b * nch + n
    return pl.pallas_call(
        _gla_kernel,
        grid=(BATCH, nch),
        in_specs=[
            pl.BlockSpec((c, 256), lambda b, n: (row(b, n), Z_BQ // 256)),
            pl.BlockSpec((c, 256), lambda b, n: (row(b, n), Z_BK // 256)),
            pl.BlockSpec((c, B_WIDTH), lambda b, n: (row(b, n), Z_BV // B_WIDTH)),
            pl.BlockSpec((c, B_WIDTH), lambda b, n: (row(b, n), Z_BR // B_WIDTH)),
            pl.BlockSpec((c, Z2_W), lambda b, n: (row(b, n), 0)),
            pl.BlockSpec((Z2_W, 256), lambda b, n: (0, 0)),
            pl.BlockSpec((1, 256), lambda b, n: (0, 0)),
            pl.BlockSpec((1, B_DV), lambda b, n: (0, 0)),
        ],
        out_specs=[
            pl.BlockSpec((c, B_WIDTH), lambda b, n: (row(b, n), 0)),
            pl.BlockSpec((None, B_HEADS, B_DK, B_DV), lambda b, n: (b, 0, 0, 0)),
        ],
        out_shape=[
            jax.ShapeDtypeStruct((T_P, B_WIDTH), BF16),
            jax.ShapeDtypeStruct((BATCH, B_HEADS, B_DK, B_DV), F32),
        ],
        scratch_shapes=[pltpu.VMEM((B_HEADS, B_DK, B_DV), F32)],
        compiler_params=_cp(("arbitrary", "arbitrary")),
        name="gla",
    )(z1, z1, z1, z1, z2, gw2p, gb, on_g)


def _compress_kernel(pool_ref, rows_ref, w1_ref, b1_ref, w2_ref, b2_ref, o_ref):
    kv = pl.program_id(2)
    nblk = SEQ // CMP_STRIDE
    r = lax.broadcasted_iota(jnp.int32, (nblk, SEQ), 0)
    c = lax.broadcasted_iota(jnp.int32, (nblk, SEQ), 1)
    d = c - CMP_STRIDE * r
    p = jnp.zeros((nblk, SEQ), F32)
    for i in range(CMP_LEN):
        p = jnp.where(d == i, pool_ref[kv, i], p)
    rows = rows_ref[...]
    ph, pm, plo = _split3(p)
    rh, rm, rl = _split3(rows)
    pooled = (_dot(ph, rh) + (_dot(ph, rm) + _dot(pm, rh))) + ((_dot(pm, rm) + _dot(ph, rl)) + _dot(plo, rh))
    hid = _gelu(_dot(pooled.astype(BF16), w1_ref[...].astype(BF16)) + b1_ref[...])
    o_ref[...] = _dot(hid.astype(BF16), w2_ref[...].astype(BF16)) + b2_ref[...]


def compress_prompt(kvn, pool, w1, b1, w2, b2):
    nblk = SEQ // CMP_STRIDE
    return pl.pallas_call(
        _compress_kernel,
        grid=(BATCH, C_KV_HEADS, 2),
        in_specs=[
            pl.BlockSpec(memory_space=pltpu.SMEM),
            pl.BlockSpec((SEQ, C_HD), lambda b, g, kv: (b, kv * C_KV_HEADS + g)),
            pl.BlockSpec((None, C_HD, C_HD), lambda b, g, kv: (kv, 0, 0)),
            pl.BlockSpec((None, 1, C_HD), lambda b, g, kv: (kv, 0, 0)),
            pl.BlockSpec((None, C_HD, C_HD), lambda b, g, kv: (kv, 0, 0)),
            pl.BlockSpec((None, 1, C_HD), lambda b, g, kv: (kv, 0, 0)),
        ],
        out_specs=pl.BlockSpec((None, None, None, nblk, C_HD), lambda b, g, kv: (b, g, kv, 0, 0)),
        out_shape=jax.ShapeDtypeStruct((BATCH, C_KV_HEADS, 2, nblk, C_HD), F32),
        compiler_params=_cp(("arbitrary", "arbitrary", "arbitrary")),
        name="compress",
    )(pool, kvn, w1, b1.reshape(2, 1, C_HD), w2, b2.reshape(2, 1, C_HD))


def _softmax_rows(s, valid):
    s = jnp.where(valid, s, NEG)
    m = jnp.max(s, axis=-1, keepdims=True)
    e = jnp.where(valid, jnp.exp(s - m), 0.0)
    return e / jnp.maximum(jnp.sum(e, axis=-1, keepdims=True), 1e-30)


def _cmp_select_kernel(q_ref, ckv_ref, gate_ref, m5_ref, o_ref, sel_ref):
    g = pl.program_id(1)
    qi = pl.program_id(2)
    tq = q_ref.shape[0]
    nblk = SEQ // CMP_STRIDE
    ck = ckv_ref[0].astype(BF16)
    cv = ckv_ref[1].astype(BF16)
    pos = qi * tq + lax.broadcasted_iota(jnp.int32, (tq, nblk), 0)
    blk = lax.broadcasted_iota(jnp.int32, (tq, nblk), 1)
    valid = blk * CMP_STRIDE + (CMP_LEN - 1) <= pos
    gates = gate_ref[...]
    imp = jnp.zeros((tq, nblk), F32)
    for r in range(C_REP):
        q = q_ref[:, r * C_HD:(r + 1) * C_HD]
        p = _softmax_rows(_dot_nt(q, ck) * (C_HD ** -0.5), valid)
        imp = imp + p
        o = _dot(p.astype(BF16), cv)
        col = Z2_CG + g * C_REP + r
        lane = lax.broadcasted_iota(jnp.int32, gates.shape, 1)
        gcol = jnp.sum(jnp.where(lane == col, gates, 0.0), axis=1, keepdims=True)
        o_ref[:, r * C_HD:(r + 1) * C_HD] = (o * _sigmoid(gcol)).astype(o_ref.dtype)
    score = _dot_exact_rhs(imp, m5_ref[...])
    j = lax.broadcasted_iota(jnp.int32, (tq, nblk), 1)
    posj = qi * tq + lax.broadcasted_iota(jnp.int32, (tq, nblk), 0)
    cur = posj // SLC_BLK
    ok_blk = j * SLC_BLK <= posj
    forced = (j == 0) | (j == cur) | (j == cur - 1)
    score = jnp.where(ok_blk, score + jnp.where(forced, FORCE_BONUS, 0.0), NEG)
    score = jnp.where(j < SEQ // SLC_BLK, score, -3.0e38)
    rank = jnp.zeros((tq, nblk), F32)
    for jj in range(SEQ // SLC_BLK):
        col = score[:, jj:jj + 1]
        ahead = (col > score) | ((col == score) & (j > jj))
        rank = rank + jnp.where(ahead, 1.0, 0.0)
    sel = (rank < SLC_TOPK) & (score > 0.5 * NEG)
    sel_ref[...] = jnp.where(sel, 1.0, 0.0).astype(sel_ref.dtype)


def cmp_select_prompt(qn, ckv, z2, m5):
    tq = 512
    nq = SEQ // tq
    nblk = SEQ // CMP_STRIDE
    return pl.pallas_call(
        _cmp_select_kernel,
        grid=(BATCH, C_KV_HEADS, nq),
        in_specs=[
            pl.BlockSpec((tq, C_REP * C_HD), lambda b, g, i: (b * nq + i, g)),
            pl.BlockSpec((None, None, 2, nblk, C_HD), lambda b, g, i: (b, g, 0, 0, 0)),
            pl.BlockSpec((tq, Z2_W), lambda b, g, i: (b * nq + i, 0)),
            pl.BlockSpec((nblk, nblk), lambda b, g, i: (0, 0)),
        ],
        out_specs=[
            pl.BlockSpec((tq, C_REP * C_HD), lambda b, g, i: (b * nq + i, g)),
            pl.BlockSpec((tq, nblk), lambda b, g, i: (b * nq + i, g)),
        ],
        out_shape=[
            jax.ShapeDtypeStruct((T_P, C_WIDTH), BF16),
            jax.ShapeDtypeStruct((T_P, C_KV_HEADS * nblk), BF16),
        ],
        compiler_params=_cp(("arbitrary", "arbitrary", "arbitrary")),
        name="cmp_select",
    )(qn, ckv, z2, m5)


def _flash_kernel(q_ref, k_ref, v_ref, sel_ref, gate_ref, o_ref, m_sc, l_sc, acc_sc, *, branch, tk):
    g = pl.program_id(1)
    qi = pl.program_id(2)
    tq = q_ref.shape[0]
    q = jnp.concatenate([q_ref[:, r * C_HD:(r + 1) * C_HD] for r in range(C_REP)], axis=0)
    m_sc[...] = jnp.full(m_sc.shape, NEG, F32)
    l_sc[...] = jnp.zeros(l_sc.shape, F32)
    acc_sc[...] = jnp.zeros(acc_sc.shape, F32)
    pos = qi * tq + lax.broadcasted_iota(jnp.int32, (tq, tk), 0)
    if branch == 1:
        lo = 0
        hi = ((qi + 1) * tq + tk - 1) // tk
        sel = sel_ref[...]
    else:
        lo = jnp.maximum(qi * tq - WINDOW, 0) // tk
        hi = ((qi + 1) * tq + tk - 1) // tk

    def step(kj, carry):
        k0 = pl.multiple_of(kj * tk, tk)
        k = k_ref[pl.ds(k0, tk), :].astype(BF16)
        v = v_ref[pl.ds(k0, tk), :].astype(BF16)
        key = kj * tk + lax.broadcasted_iota(jnp.int32, (tq, tk), 1)
        if branch == 1:
            blk_of_key = kj * (tk // SLC_BLK) + lax.broadcasted_iota(jnp.int32, (sel.shape[1], tk), 1) // SLC_BLK
            expand = jnp.where(lax.broadcasted_iota(jnp.int32, (sel.shape[1], tk), 0) == blk_of_key, 1.0, 0.0)
            chosen = _dot(sel, expand.astype(BF16)) > 0.5
            valid = chosen & (key <= pos)
        else:
            valid = (key <= pos) & (key > pos - WINDOW)
        valid4 = jnp.concatenate([valid] * C_REP, axis=0)
        s = jnp.where(valid4, _dot_nt(q, k) * (C_HD ** -0.5), NEG)
        m_old = m_sc[...]
        m_new = jnp.maximum(m_old, jnp.max(s, axis=-1, keepdims=True))
        alpha = jnp.exp(m_old - m_new)
        p = jnp.where(valid4, jnp.exp(s - m_new), 0.0)
        l_sc[...] = alpha * l_sc[...] + jnp.sum(p, axis=-1, keepdims=True)
        acc_sc[...] = alpha * acc_sc[...] + _dot(p.astype(BF16), v)
        m_sc[...] = m_new
        return carry

    lax.fori_loop(lo, hi, step, 0)
    out = acc_sc[...] / jnp.maximum(l_sc[...], 1e-30)
    gates = gate_ref[...]
    lane = lax.broadcasted_iota(jnp.int32, gates.shape, 1)
    for r in range(C_REP):
        col = Z2_CG + branch * C_HEADS + g * C_REP + r
        gcol = jnp.sum(jnp.where(lane == col, gates, 0.0), axis=1, keepdims=True)
        o_ref[:, r * C_HD:(r + 1) * C_HD] = (out[r * tq:(r + 1) * tq] * _sigmoid(gcol)).astype(o_ref.dtype)


def flash_prompt(qn, kvn, sel, z2, branch):
    tq = 256
    tk = 256
    nq = SEQ // tq
    nblk = SEQ // CMP_STRIDE
    kslot = 2 * branch
    kern = functools.partial(_flash_kernel, branch=branch, tk=tk)
    return pl.pallas_call(
        kern,
        grid=(BATCH, C_KV_HEADS, nq),
        in_specs=[
            pl.BlockSpec((tq, C_REP * C_HD), lambda b, g, i: (b * nq + i, g)),
            pl.BlockSpec((SEQ, C_HD), lambda b, g, i: (b, kslot * C_KV_HEADS + g)),
            pl.BlockSpec((SEQ, C_HD), lambda b, g, i: (b, (kslot + 1) * C_KV_HEADS + g)),
            pl.BlockSpec((tq, nblk), lambda b, g, i: (b * nq + i, g)),
            pl.BlockSpec((tq, Z2_W), lambda b, g, i: (b * nq + i, 0)),
        ],
        out_specs=pl.BlockSpec((tq, C_REP * C_HD), lambda b, g, i: (b * nq + i, g)),
        out_shape=jax.ShapeDtypeStruct((T_P, C_WIDTH), BF16),
        scratch_shapes=[
            pltpu.VMEM((C_REP * tq, 1), F32),
            pltpu.VMEM((C_REP * tq, 1), F32),
            pltpu.VMEM((C_REP * tq, C_HD), F32),
        ],
        compiler_params=_cp(("arbitrary", "arbitrary", "arbitrary")),
        name="flash_slc" if branch == 1 else "flash_win",
    )(qn, kvn, kvn, sel, z2)


def _merge_kernel(oa_ref, ob_ref, oc0_ref, oc1_ref, oc2_ref, ga_ref, gb_ref, gc_ref, wa_ref, wb_ref, wc_ref, o_ref):
    oc = (oc0_ref[...].astype(F32) + oc1_ref[...].astype(F32) + oc2_ref[...].astype(F32)).astype(BF16)
    m = _sigmoid(ga_ref[...].astype(F32)) * _dot(oa_ref[...], wa_ref[...])
    m = m + _sigmoid(gb_ref[...].astype(F32)) * _dot(ob_ref[...], wb_ref[...])
    m = m + _sigmoid(gc_ref[...].astype(F32)) * _dot(oc, wc_ref[...])
    o_ref[...] = m.astype(o_ref.dtype)


def merge(o_a, o_b, oc0, oc1, oc2, z1, wa, wb, wc, tm):
    t = o_a.shape[0]
    tn = 512
    nj = D_MODEL // tn
    gate = lambda k: pl.BlockSpec((tm, tn), lambda i, j: (i, (Z_MG + k * D_MODEL) // tn + j))
    return pl.pallas_call(
        _merge_kernel,
        grid=(t // tm, nj),
        in_specs=[
            pl.BlockSpec((tm, A_WIDTH), lambda i, j: (i, 0)),
            pl.BlockSpec((tm, B_WIDTH), lambda i, j: (i, 0)),
            pl.BlockSpec((tm, C_WIDTH), lambda i, j: (i, 0)),
            pl.BlockSpec((tm, C_WIDTH), lambda i, j: (i, 0)),
            pl.BlockSpec((tm, C_WIDTH), lambda i, j: (i, 0)),
            gate(0), gate(1), gate(2),
            pl.BlockSpec((A_WIDTH, tn), lambda i, j: (0, j)),
            pl.BlockSpec((B_WIDTH, tn), lambda i, j: (0, j)),
            pl.BlockSpec((C_WIDTH, tn), lambda i, j: (0, j)),
        ],
        out_specs=pl.BlockSpec((tm, tn), lambda i, j: (i, j)),
        out_shape=jax.ShapeDtypeStruct((t, D_MODEL), BF16),
        compiler_params=_cp(("arbitrary", "arbitrary")),
        name="merge",
    )(o_a, o_b, oc0, oc1, oc2, z1, z1, z1, wa, wb, wc)


N_CELLS = 50
N_CELL_ROWS = 56


def _cell_tables():
    sa = np.zeros((N_CELL_ROWS, P_TOPK), np.float32)
    sb = np.zeros((N_CELL_ROWS, P_TOPK), np.float32)
    r = 0
    for a in range(P_TOPK):
        for b in range(P_TOPK // (a + 1)):
            sa[r, a] = 1.0
            sb[r, b] = 1.0
            r += 1
    assert r == N_CELLS
    return sa, sb


def _top16_cols(s, n_iota):
    rank = jnp.full(s.shape, float(P_TOPK), F32)
    work = s
    top = jnp.zeros((P_TOPK, s.shape[1]), F32)
    r_iota = lax.broadcasted_iota(jnp.int32, top.shape, 0)
    for r in range(P_TOPK):
        m = jnp.max(work, axis=0, keepdims=True)
        idx = jnp.min(jnp.where(work == m, n_iota, 1.0e9), axis=0, keepdims=True)
        hit = n_iota == idx
        rank = jnp.where(hit, float(r), rank)
        work = jnp.where(hit, -jnp.inf, work)
        top = jnp.where(r_iota == r, m, top)
    return rank, top


def _peer_topk_kernel(qn_ref, sk_ref, sa_ref, sb_ref, sat_ref, r2_ref, cnt_ref, e1_ref, e2_ref):
    tb = qn_ref.shape[0]
    lanes = 128
    sk1 = sk_ref[0].astype(BF16)
    sk2 = sk_ref[1].astype(BF16)
    s1_all = _dot_nt(sk1, qn_ref[:, 0:P_HALF])
    s2_all = _dot_nt(sk2, qn_ref[:, P_HALF:2 * P_HALF])
    n_iota = lax.broadcasted_iota(jnp.int32, (P_NKEYS, lanes), 0).astype(F32)
    c_iota = lax.broadcasted_iota(jnp.int32, (N_CELL_ROWS, lanes), 0).astype(F32)
    sa = sa_ref[...]
    sb = sb_ref[...]
    sat = sat_ref[...]
    for t in range(tb // lanes):
        cols = slice(t * lanes, (t + 1) * lanes)
        s1 = s1_all[:, cols]
        s2 = s2_all[:, cols]
        rank1, top1 = _top16_cols(s1, n_iota)
        rank2, top2 = _top16_cols(s2, n_iota)
        cand = _dot_exact_lhs(sa, top1) + _dot_exact_lhs(sb, top2)
        work = jnp.where(c_iota < N_CELLS, cand, -jnp.inf)
        picked = jnp.zeros(work.shape, F32)
        v0 = None
        zsum = None
        for r in range(P_TOPK):
            m = jnp.max(work, axis=0, keepdims=True)
            idx = jnp.min(jnp.where(work == m, c_iota, 1.0e9), axis=0, keepdims=True)
            hit = c_iota == idx
            picked = jnp.where(hit, 1.0, picked)
            work = jnp.where(hit, -jnp.inf, work)
            if r == 0:
                v0 = m
                zsum = jnp.ones_like(m)
            else:
                zsum = zsum + jnp.exp(m - v0)
        cnt = _dot(sat, picked.astype(BF16))
        cntd = jnp.zeros(rank1.shape, F32)
        for a in range(P_TOPK):
            cntd = jnp.where(rank1 == float(a), cnt[a:a + 1, :], cntd)
        e1 = jnp.where(rank1 < float(P_TOPK), jnp.exp(s1 - top1[0:1, :]), 0.0) / zsum
        e2 = jnp.exp(s2 - top2[0:1, :])
        r2_ref[:, cols] = rank2.astype(r2_ref.dtype)
        cnt_ref[:, cols] = cntd.astype(cnt_ref.dtype)
        e1_ref[:, cols] = e1
        e2_ref[:, cols] = e2


def peer_topk(qn, subkeys, tb):
    t = qn.shape[0]
    sa, sb = _cell_tables()
    out = lambda dt: jax.ShapeDtypeStruct((P_HEADS, P_NKEYS, t), dt)
    ospec = pl.BlockSpec((None, P_NKEYS, tb), lambda i, h: (h, 0, i))
    return pl.pallas_call(
        _peer_topk_kernel,
        grid=(t // tb, P_HEADS),
        in_specs=[
            pl.BlockSpec((tb, 2 * P_HALF), lambda i, h: (i, h)),
            pl.BlockSpec((None, 2, P_NKEYS, P_HALF), lambda i, h: (h, 0, 0, 0)),
            pl.BlockSpec((N_CELL_ROWS, P_TOPK), lambda i, h: (0, 0)),
            pl.BlockSpec((N_CELL_ROWS, P_TOPK), lambda i, h: (0, 0)),
            pl.BlockSpec((P_TOPK, N_CELL_ROWS), lambda i, h: (0, 0)),
        ],
        out_specs=[ospec, ospec, ospec, ospec],
        out_shape=[out(BF16), out(F32), out(F32), out(F32)],
        compiler_params=_cp(("arbitrary", "arbitrary")),
        name="peer_topk",
    )(qn, subkeys, jnp.asarray(sa, BF16), jnp.asarray(sb, BF16), jnp.asarray(sa.T, BF16))


def _peer_main_kernel(h_ref, u_ref, vt_ref, r2_ref, cnt_ref, e1_ref, e2_ref, y_ref, acc_sc, p_sc, *, ec):
    e = pl.program_id(1)

    @pl.when(e == 0)
    def _():
        acc_sc[...] = jnp.zeros_like(acc_sc)

    hid = _gelu(_dot_nt(u_ref[...], h_ref[...]))
    nchunk = ec // P_NKEYS
    for cc in range(nchunk):
        c = e * nchunk + cc
        w = None
        for h in range(P_HEADS):
            e1 = e1_ref[h, pl.ds(c, 1), :]
            cn = cnt_ref[h, pl.ds(c, 1), :].astype(F32)
            term = e1 * jnp.where(r2_ref[h].astype(F32) < cn, e2_ref[h], 0.0)
            w = term if w is None else w + term
        rows = slice(cc * P_NKEYS, (cc + 1) * P_NKEYS)
        p_sc[rows, :] = (w * hid[rows, :]).astype(BF16)
    acc_sc[...] += _dot(vt_ref[...], p_sc[...])

    @pl.when(e == pl.num_programs(1) - 1)
    def _():
        y_ref[...] = acc_sc[...].T


def peer_main(h2, u_bf, vt_bf, r2, cnt, e1, e2, tb):
    t = h2.shape[0]
    ec = 512
    gspec = pl.BlockSpec((P_HEADS, P_NKEYS, tb), lambda i, e: (0, 0, i))
    return pl.pallas_call(
        functools.partial(_peer_main_kernel, ec=ec),
        grid=(t // tb, P_NEXP // ec),
        in_specs=[
            pl.BlockSpec((tb, D_MODEL), lambda i, e: (i, 0)),
            pl.BlockSpec((ec, D_MODEL), lambda i, e: (e, 0)),
            pl.BlockSpec((D_MODEL, ec), lambda i, e: (0, e)),
            gspec, gspec, gspec, gspec,
        ],
        out_specs=pl.BlockSpec((tb, D_MODEL), lambda i, e: (i, 0)),
        out_shape=jax.ShapeDtypeStruct((t, D_MODEL), F32),
        scratch_shapes=[pltpu.VMEM((D_MODEL, tb), F32), pltpu.VMEM((ec, tb), BF16)],
        compiler_params=_cp(("arbitrary", "arbitrary")),
        name="peer_main",
    )(h2, u_bf, vt_bf, r2, cnt, e1, e2)


def _rmsnorm_x(x, g):
    return x * lax.rsqrt(jnp.mean(x * x, axis=-1, keepdims=True) + EPS) * g


def _masked_softmax_x(s, valid):
    s = jnp.where(valid, s, NEG)
    m = jnp.max(s, axis=-1, keepdims=True)
    e = jnp.where(valid, jnp.exp(s - m), 0.0)
    return e / jnp.maximum(jnp.sum(e, axis=-1, keepdims=True), 1e-30)


def sample_mixers(z1, z2, qn, kvn, P, cache_l, win_buf, gla_s0, page_table):
    db = DEC_BATCH
    z1 = z1[:db].astype(F32)
    z2 = z2[:db]
    a_u = jax.nn.gelu(z1[:, Z_AU:Z_AU + A_WIDTH])
    a_v = jax.nn.gelu(z1[:, Z_AV:Z_AV + A_WIDTH])
    mu = jnp.mean(a_v, axis=-1, keepdims=True)
    var = jnp.mean(jnp.square(a_v - mu), axis=-1, keepdims=True)
    a_v = (a_v - mu) * lax.rsqrt(var + EPS) * P['a_ln_g'] + P['a_ln_b']
    w00 = jnp.repeat(P['a_ws'][:, 0, 0], A_GW)
    b0 = jnp.repeat(P['a_bs'][:, 0], A_GW)
    o_a = a_u * (a_v * w00 + b0)
    b_g = z2[:, :B_GATE_RANK]
    log_a = jax.nn.log_sigmoid(b_g @ P['b_gw2'] + P['b_gb']) / B_GATE_TAU
    q = z1[:, Z_BQ:Z_BQ + 256].reshape(db, B_HEADS, B_DK) * (B_DK ** -0.5)
    k = z1[:, Z_BK:Z_BK + 256].reshape(db, B_HEADS, B_DK)
    v = z1[:, Z_BV:Z_BV + B_WIDTH].reshape(db, B_HEADS, B_DV)
    bl = log_a.reshape(db, B_HEADS, B_DK)
    qd = q * jnp.exp(bl)
    kd = k * jnp.exp(-bl)
    att = jnp.sum(qd * kd, axis=-1, keepdims=True)
    o = att * v + jnp.einsum('bhd,bhdv->bhv', qd, gla_s0)
    s_new = gla_s0 * jnp.exp(bl)[..., None] + k[..., None] * v[:, :, None, :]
    o_b = _rmsnorm_x(o, P['b_on_g']).reshape(db, B_WIDTH) * jax.nn.silu(z1[:, Z_BR:Z_BR + B_WIDTH])
    qs = qn[:db].astype(F32).reshape(db, C_KV_HEADS, C_REP, C_HD)
    kv = kvn[:db].reshape(db, 6, C_KV_HEADS, C_HD)
    pos = PAST_LEN
    past = cache_l[page_table][:, :, :, 0:2].reshape(db, PAST_LEN, 2, C_KV_HEADS, C_HD)
    nseg = PAST_LEN // CMP_STRIDE
    seg = past.reshape(db, nseg, CMP_STRIDE, 2, C_KV_HEADS, C_HD)
    cks = []
    for i in range(2):
        first = jnp.einsum('bnigd,i->bngd', seg[:, :, :, i], P['cmp_pool'][i, :CMP_STRIDE])
        second = jnp.einsum('bnigd,i->bngd', seg[:, :, :, i], P['cmp_pool'][i, CMP_STRIDE:])
        pooled = first[:, :-1] + second[:, 1:]
        cks.append(jax.nn.gelu(pooled @ P['cmp_w1'][i] + P['cmp_b1'][i]) @ P['cmp_w2'][i] + P['cmp_b2'][i])
    ck, cv = cks
    ncmp = ck.shape[1]
    s = jnp.einsum('bgrd,bngd->bgrn', qs, ck) * (C_HD ** -0.5)
    end = jnp.arange(ncmp, dtype=jnp.int32) * CMP_STRIDE + (CMP_LEN - 1)
    p = _masked_softmax_x(s, (end <= pos)[None, None, None, :])
    o_cmp = jnp.einsum('bgrn,bngd->bgrd', p, cv)
    imp = jnp.sum(p, axis=2)
    n_slc = -(-(PAST_LEN + 1) // SLC_BLK)
    r4 = SLC_BLK // CMP_STRIDE
    pp = jnp.pad(imp, ((0, 0), (0, 0), (1, r4 * n_slc + r4 - 1 - ncmp)))
    blk = pp[..., :r4 * n_slc].reshape(db, C_KV_HEADS, n_slc, r4).sum(-1) + pp[..., r4::r4]
    j = jnp.arange(n_slc, dtype=jnp.int32)
    cur = pos // SLC_BLK
    forced = (j == 0) | (j == cur) | (j == cur - 1)
    score = jnp.where(j * SLC_BLK <= pos, blk + jnp.where(forced, FORCE_BONUS, 0.0), NEG)
    top, idx = lax.top_k(score, SLC_TOPK)
    ok = top > 0.5 * NEG
    n_past_blk = PAST_LEN // SLC_BLK
    bpp = PAGE_SIZE // SLC_BLK
    jc = jnp.minimum(idx, n_past_blk - 1)
    phys = page_table[jnp.arange(db)[:, None, None], jc // bpp][..., None]
    prow = (jc % bpp)[..., None] * SLC_BLK + jnp.arange(SLC_BLK, dtype=jnp.int32)
    gi = jnp.arange(C_KV_HEADS)[None, :, None, None]
    k_old = cache_l[phys, prow, 2, gi]
    v_old = cache_l[phys, prow, 3, gi]
    is_new = (idx >= n_past_blk)[..., None, None]
    first_row = (jnp.arange(SLC_BLK) == 0)[None, None, None, :, None]
    newk = jnp.where(first_row, kv[:, 2][:, :, None, None, :], 0.0)
    newv = jnp.where(first_row, kv[:, 3][:, :, None, None, :], 0.0)
    kb = jnp.where(is_new, newk, k_old)
    vb = jnp.where(is_new, newv, v_old)
    rows = idx[..., None] * SLC_BLK + jnp.arange(SLC_BLK, dtype=jnp.int32)
    s = jnp.einsum('bgrd,bgkld->bgrkl', qs, kb) * (C_HD ** -0.5)
    valid = (ok[..., None] & (rows <= pos)).reshape(db, C_KV_HEADS, 1, SLC_TOPK * SLC_BLK)
    p = _masked_softmax_x(s.reshape(db, C_KV_HEADS, C_REP, SLC_TOPK * SLC_BLK), valid)
    o_slc = jnp.einsum('bgrm,bgmd->bgrd', p, vb.reshape(db, C_KV_HEADS, SLC_TOPK * SLC_BLK, C_HD))
    wb = win_buf.shape[1]
    kk = jnp.concatenate([win_buf[:, :, 0], kv[:, 4][:, None]], axis=1)
    vv = jnp.concatenate([win_buf[:, :, 1], kv[:, 5][:, None]], axis=1)
    kp = jnp.concatenate([PAST_LEN - wb + jnp.arange(wb, dtype=jnp.int32), jnp.array([pos], jnp.int32)])
    s = jnp.einsum('bgrd,bkgd->bgrk', qs, kk) * (C_HD ** -0.5)
    p = _masked_softmax_x(s, ((kp <= pos) & (kp > pos - WINDOW))[None, None, None, :])
    o_win = jnp.einsum('bgrk,bkgd->bgrd', p, vv)
    gc = jax.nn.sigmoid(z2[:, Z2_CG:Z2_CG + 3 * C_HEADS]).reshape(db, 3, C_KV_HEADS, C_REP, 1)
    o_c = (gc[:, 0] * o_cmp + gc[:, 1] * o_slc + gc[:, 2] * o_win).reshape(db, C_WIDTH)

    def pad(a):
        return jnp.pad(a, ((0, T_S - db), (0, 0))).astype(BF16)

    return pad(o_a), pad(o_b), pad(o_c), a_v, s_new


def _rope_tables(pos):
    half = C_ROT // 2
    inv = jnp.float32(ROPE_THETA) ** (-jnp.arange(half, dtype=F32) / half)
    ang = pos.astype(F32)[:, None] * inv[None, :]
    cos = jnp.cos(ang)
    sin = jnp.sin(ang)
    n = pos.shape[0]
    ones = jnp.ones((n, C_HD - C_ROT), F32)
    zeros = jnp.zeros((n, C_HD - half), F32)
    c = jnp.concatenate([cos, cos, ones], axis=1)
    s1 = jnp.concatenate([-sin, zeros], axis=1)
    s2 = jnp.concatenate([jnp.zeros((n, half), F32), sin, jnp.zeros((n, C_HD - C_ROT), F32)], axis=1)
    return c, s1, s2


def _m5_table():
    nblk = SEQ // CMP_STRIDE
    m5 = np.zeros((nblk, nblk), np.float32)
    for j in range(SEQ // SLC_BLK):
        for n in range(4 * j - 1, 4 * j + 4):
            if 0 <= n < nblk - 1:
                m5[n, j] = 1.0
    return m5


def _permute_w_in(w_in):
    o = np.cumsum([0, 512, 512, 256, 256, 512, 16, 512, 1024, 1536, 24, 6144])
    seg = lambda i: w_in[..., o[i]:o[i + 1]]
    w1 = jnp.concatenate([seg(7), seg(0), seg(1), seg(4), seg(6), seg(8), seg(2), seg(3), seg(10)], axis=-1)
    pad = jnp.zeros(w_in.shape[:-1] + (Z2_W - 40,), w_in.dtype)
    w2 = jnp.concatenate([seg(5), seg(9), pad], axis=-1)
    return w1.astype(BF16), w2.astype(BF16)


def kernel(x_prompt, x_sample, cache_nsa_kv, state_win_kv, state_gla, page_table, c_prompt, c_sample, ada_w, ada_b, norm1_g, norm2_g, w_in, a_ln_g, a_ln_b, a_ws, a_bs, b_gw2, b_gb, b_on_g, c_qn_g, c_kn_g, cmp_pool, cmp_w1, cmp_b1, cmp_w2, cmp_b2, w_br_a, w_br_b, w_br_c, w_out, p_wq, p_qn_g, p_subkeys, p_u, p_v):
    tm_p, tm_s = 512, T_S
    w1_all, w2_all = _permute_w_in(w_in)
    wa_all = w_br_a.astype(BF16)
    wb_all = w_br_b.astype(BF16)
    wc_all = w_br_c.astype(BF16)
    wo_all = w_out.astype(BF16)
    wq_all = p_wq.astype(BF16)
    u_all = p_u.astype(BF16)
    vt_all = jnp.swapaxes(p_v, 1, 2).astype(BF16)
    c40 = jnp.concatenate([jnp.repeat(c_prompt, N_SEQ_ROWS, axis=0), c_sample], axis=0)
    mod_all = adaln_table(c40, ada_w, ada_b)
    rope_p = _rope_tables(jnp.arange(SEQ, dtype=jnp.int32))
    rope_s = _rope_tables(jnp.full((T_S,), PAST_LEN, jnp.int32))
    m5 = jnp.asarray(_m5_table(), BF16)
    xp = x_prompt.reshape(T_P, D_MODEL)
    xs = jnp.pad(x_sample.reshape(DEC_BATCH, D_MODEL), ((0, T_S - DEC_BATCH), (0, 0)))
    gw2p_all = jnp.pad(b_gw2, ((0, 0), (0, Z2_W - B_GATE_RANK), (0, 0))).astype(BF16)

    outs = {k: [] for k in ('kv_p', 'kv_s', 'win_p', 'win_s', 'gla_p', 'gla_s', 'cv_s')}
    yp = ys = None
    for l in range(DEPTH):
        mod = mod_all[l]
        g1 = norm1_g[l][None]
        g2 = norm2_g[l][None]
        P = {
            'a_ln_g': a_ln_g[l], 'a_ln_b': a_ln_b[l], 'a_ws': a_ws[l], 'a_bs': a_bs[l], 'b_gw2': b_gw2[l],
            'b_gb': b_gb[l], 'b_on_g': b_on_g[l], 'cmp_pool': cmp_pool[l], 'cmp_w1': cmp_w1[l],
            'cmp_b1': cmp_b1[l], 'cmp_w2': cmp_w2[l], 'cmp_b2': cmp_b2[l],
        }
        streams = []
        for is_s, x, y, tm in ((False, xp, yp, tm_p), (True, xs, ys, tm_s)):
            if l == 0:
                h = norm_mod(x, g1, mod, 0, 1, tm, is_s)
            else:
                x, h = resid_norm_mod(x, y, g1, mod_all[l - 1], mod, 5, 0, 1, tm, is_s)
            z1 = matmul(h, w1_all[l], tm, 1024, BF16, "w_in")
            z2 = matmul(h, w2_all[l], tm, Z2_W, F32, "w_in_gates")
            rope = rope_s if is_s else rope_p
            qn, kvn = qk_prep(z1, c_qn_g[l][None], c_kn_g[l], rope[0], rope[1], rope[2], tm, is_s)
            streams.append((x, z1, z2, qn, kvn))
        x, z1, z2, qn, kvn = streams[0]
        o_a = chunk_mlp_prompt(z1, a_ln_g[l][None], a_ln_b[l][None], a_ws[l], a_bs[l].T)
        o_b, s_p = gla_prompt(z1, z2, gw2p_all[l], b_gb[l][None], b_on_g[l][None])
        ckv = compress_prompt(kvn, cmp_pool[l], cmp_w1[l], cmp_b1[l], cmp_w2[l], cmp_b2[l])
        o_cmp, sel = cmp_select_prompt(qn, ckv, z2, m5)
        o_slc = flash_prompt(qn, kvn, sel, z2, 1)
        o_win = flash_prompt(qn, kvn, sel, z2, 2)
        mixed_p = (o_a, o_b, o_cmp, o_slc, o_win)
        kv5 = kvn.reshape(BATCH, SEQ, 6, C_KV_HEADS, C_HD)
        outs['kv_p'].append(kv5[:, :, :4])
        outs['win_p'].append(kv5[:, SEQ - WINDOW:, 4:])
        outs['gla_p'].append(s_p)
        xs_, z1s, z2s, qns, kvns = streams[1]
        cache_l = cache_nsa_kv[l]
        o_as, o_bs, o_cs, a_v_s, s_s = sample_mixers(z1s, z2s, qns, kvns, P, cache_l, state_win_kv[l], state_gla[l], page_table)
        zc = jnp.zeros_like(o_cs)
        mixed_s = (o_as, o_bs, o_cs, zc, zc)
        kv5s = kvns[:DEC_BATCH].reshape(DEC_BATCH, 1, 6, C_KV_HEADS, C_HD)
        outs['kv_s'].append(kv5s[:, :, :4])
        outs['win_s'].append(kv5s[:, :, 4:])
        outs['gla_s'].append(s_s)
        outs['cv_s'].append(a_v_s[:, None, :])
        new = []
        for is_s, (x, z1, z2, qn, kvn), mixed, tm in ((False, streams[0], mixed_p, tm_p), (True, streams[1], mixed_s, tm_s)):
            mg = merge(*mixed, z1, wa_all[l], wb_all[l], wc_all[l], tm)
            att = matmul(mg, wo_all[l], tm, 1024, F32, "w_out")
            x1, h2 = resid_norm_mod(x, att, g2, mod, mod, 2, 3, 4, tm, is_s)
            pq = peer_query(h2, wq_all[l], p_qn_g[l][None], tm)
            r2, cnt, e1, e2 = peer_topk(pq, p_subkeys[l], tm)
            y = peer_main(h2, u_all[l], vt_all[l], r2, cnt, e1, e2, tm)
            new.append((x1, y))
        (xp, yp), (xs, ys) = new
    mod = mod_all[DEPTH - 1]
    xp = resid(xp, yp, mod, 5, tm_p, False)
    xs = resid(xs, ys, mod, 5, tm_s, True)
    return (
        xp.reshape(BATCH, SEQ, D_MODEL),
        xs[:DEC_BATCH].reshape(DEC_BATCH, 1, D_MODEL),
        jnp.stack(outs['kv_p']),
        jnp.stack(outs['kv_s']),
        jnp.stack(outs['win_p']),
        jnp.stack(outs['win_s']),
        jnp.stack(outs['gla_p']),
        jnp.stack(outs['gla_s']),
        jnp.stack(outs['cv_s']),
    )
```

```python
import functools

import numpy as np
import jax
import jax.numpy as jnp
from jax import lax
from jax.experimental import pallas as pl
from jax.experimental.pallas import tpu as pltpu

F32 = jnp.float32
BF16 = jnp.bfloat16

D_MODEL = 2048
BATCH = 4
SEQ = 2048
DEPTH = 4
DEC_BATCH = 8
PAST_LEN = 16384
PAGE_SIZE = 128
EPS = 1e-6
NEG = -1.0e30
A_WIDTH = 512
A_GROUPS = 4
A_GW = 128
A_CHUNK = 128
B_HEADS = 4
B_WIDTH = 512
B_DV = 128
B_DK = 64
B_GATE_RANK = 16
B_GATE_TAU = 16.0
B_CHUNK = 64
C_HEADS = 8
C_KV_HEADS = 2
C_HD = 128
C_WIDTH = 1024
C_REP = 4
C_ROT = 32
ROPE_THETA = 500000.0
CMP_LEN = 32
CMP_STRIDE = 16
SLC_BLK = 64
SLC_TOPK = 16
FORCE_BONUS = 1.0e4
WINDOW = 512
P_HEADS = 8
P_NKEYS = 128
P_NEXP = P_NKEYS * P_NKEYS
P_HALF = 128
P_TOPK = 16

T_P = BATCH * SEQ
T_S = 128
N_SEQ_ROWS = 8
N_CMP = SEQ // CMP_STRIDE

Z_CQ = 0
Z_AU = 1024
Z_AV = 1536
Z_BV = 2048
Z_BR = 2560
Z_CKV = 3072
Z_BQ = 4608
Z_BK = 4864
Z_MG = 5120
Z1_W = 11264
Z2_W = 128
Z2_CG = 16
KVA_W = 1024
KVB_W = 512

VMEM_LIMIT = 56 * 1024 * 1024


def _cp(sem, vmem=VMEM_LIMIT):
    return pltpu.CompilerParams(dimension_semantics=sem, vmem_limit_bytes=vmem)


def _lspec(l, *dims):
    nd = len(dims)
    return pl.BlockSpec((None,) + tuple(dims), lambda *_: (l,) + (0,) * nd)


def _gelu(x):
    return 0.5 * x * (1.0 + jnp.tanh(0.7978845608028654 * (x + 0.044715 * x * x * x)))


def _sigmoid(x):
    return 1.0 / (1.0 + jnp.exp(-x))


def _log_sigmoid(x):
    return jnp.minimum(x, 0.0) - jnp.log(1.0 + jnp.exp(-jnp.abs(x)))


def _dot(a, b):
    return jnp.dot(a, b, preferred_element_type=F32)


def _dot_nt(a, b):
    return lax.dot_general(a, b, (((1,), (1,)), ((), ())), preferred_element_type=F32)


def _split3(v):
    hi = v.astype(BF16)
    r1 = v - hi.astype(F32)
    mid = r1.astype(BF16)
    r2 = r1 - mid.astype(F32)
    lo = r2.astype(BF16)
    return hi, mid, lo


def _dot_exact_lhs(sel_bf16, v):
    hi, mid, lo = _split3(v)
    return (_dot(sel_bf16, hi) + _dot(sel_bf16, mid)) + _dot(sel_bf16, lo)


def _dot_exact_rhs(v, sel_bf16):
    hi, mid, lo = _split3(v)
    return (_dot(hi, sel_bf16) + _dot(mid, sel_bf16)) + _dot(lo, sel_bf16)


def _rms(x, g):
    return x * lax.rsqrt(jnp.mean(x * x, axis=-1, keepdims=True) + EPS) * g


def _mod_rows(y, scale8, shift8=None):
    tm, n = y.shape
    y3 = y.reshape(tm // N_SEQ_ROWS, N_SEQ_ROWS, n) * scale8[None]
    if shift8 is not None:
        y3 = y3 + shift8[None]
    return y3.reshape(tm, n)


def _gate_col(gates, col):
    lane = lax.broadcasted_iota(jnp.int32, gates.shape, 1)
    return jnp.sum(jnp.where(lane == col, gates, 0.0), axis=1, keepdims=True)


def _adaln_kernel(c_ref, w_ref, b_ref, o_ref):
    c = c_ref[...]
    s = (c * _sigmoid(c)).astype(BF16)
    o_ref[...] = _dot(s, w_ref[...].astype(BF16)) + b_ref[...]


def adaln_table(c40, ada_w, ada_b):
    tn = 1024
    n = 6 * D_MODEL
    return pl.pallas_call(
        _adaln_kernel,
        grid=(DEPTH, n // tn),
        in_specs=[
            pl.BlockSpec((5 * N_SEQ_ROWS, D_MODEL), lambda l, j: (0, 0)),
            pl.BlockSpec((None, D_MODEL, tn), lambda l, j: (l, 0, j)),
            pl.BlockSpec((None, 1, tn), lambda l, j: (l, 0, j)),
        ],
        out_specs=pl.BlockSpec((None, 5 * N_SEQ_ROWS, tn), lambda l, j: (l, 0, j)),
        out_shape=jax.ShapeDtypeStruct((DEPTH, 5 * N_SEQ_ROWS, n), F32),
        compiler_params=_cp(("arbitrary", "arbitrary")),
        name="adaln_table",
    )(c40, ada_w, ada_b.reshape(DEPTH, 1, n))


def _seq_of(tm, is_sample):
    if is_sample:
        return lambda i: 4
    per = SEQ // tm
    return lambda i: i // per


def _mod_spec(l, sq, k):
    return pl.BlockSpec((None, N_SEQ_ROWS, D_MODEL), lambda i: (l, sq(i), k))


def _norm_mod_kernel(x_ref, g_ref, sh_ref, sc_ref, h_ref):
    y = _rms(x_ref[...], g_ref[...])
    h_ref[...] = _mod_rows(y, 1.0 + sc_ref[...], sh_ref[...]).astype(h_ref.dtype)


def norm_mod(x, g, mod, l, k_shift, k_scale, tm, is_sample):
    t = x.shape[0]
    sq = _seq_of(tm, is_sample)
    return pl.pallas_call(
        _norm_mod_kernel,
        grid=(t // tm,),
        in_specs=[
            pl.BlockSpec((tm, D_MODEL), lambda i: (i, 0)),
            _lspec(l, 1, D_MODEL),
            _mod_spec(l, sq, k_shift),
            _mod_spec(l, sq, k_scale),
        ],
        out_specs=pl.BlockSpec((tm, D_MODEL), lambda i: (i, 0)),
        out_shape=jax.ShapeDtypeStruct((t, D_MODEL), BF16),
        compiler_params=_cp(("arbitrary",)),
        name="norm_mod",
    )(x, g, mod, mod)


def _resid_norm_mod_kernel(x_ref, y_ref, gate_ref, g_ref, sh_ref, sc_ref, xo_ref, h_ref):
    xn = x_ref[...] + _mod_rows(y_ref[...], gate_ref[...])
    xo_ref[...] = xn
    h_ref[...] = _mod_rows(_rms(xn, g_ref[...]), 1.0 + sc_ref[...], sh_ref[...]).astype(h_ref.dtype)


def resid_norm_mod(x, y, g, mod, l_gate, l, k_gate, k_shift, k_scale, tm, is_sample):
    t = x.shape[0]
    sq = _seq_of(tm, is_sample)
    row = pl.BlockSpec((tm, D_MODEL), lambda i: (i, 0))
    return pl.pallas_call(
        _resid_norm_mod_kernel,
        grid=(t // tm,),
        in_specs=[
            row,
            row,
            _mod_spec(l_gate, sq, k_gate),
            _lspec(l, 1, D_MODEL),
            _mod_spec(l, sq, k_shift),
            _mod_spec(l, sq, k_scale),
        ],
        out_specs=[row, row],
        out_shape=[jax.ShapeDtypeStruct((t, D_MODEL), F32), jax.ShapeDtypeStruct((t, D_MODEL), BF16)],
        compiler_params=_cp(("arbitrary",)),
        name="resid_norm_mod",
    )(x, y, mod, g, mod, mod)


def _resid_kernel(x_ref, y_ref, gate_ref, xo_ref):
    xo_ref[...] = x_ref[...] + _mod_rows(y_ref[...], gate_ref[...])


def resid(x, y, mod, l, k_gate, tm, is_sample):
    t = x.shape[0]
    sq = _seq_of(tm, is_sample)
    row = pl.BlockSpec((tm, D_MODEL), lambda i: (i, 0))
    return pl.pallas_call(
        _resid_kernel,
        grid=(t // tm,),
        in_specs=[row, row, _mod_spec(l, sq, k_gate)],
        out_specs=row,
        out_shape=jax.ShapeDtypeStruct((t, D_MODEL), F32),
        compiler_params=_cp(("arbitrary",)),
        name="resid",
    )(x, y, mod)


def _mm_kernel(x_ref, w_ref, o_ref):
    o_ref[...] = _dot(x_ref[...], w_ref[...]).astype(o_ref.dtype)


def matmul(x, w, l, tm, tn, out_dtype, name):
    m, k = x.shape
    n = w.shape[2]
    return pl.pallas_call(
        _mm_kernel,
        grid=(n // tn, m // tm),
        in_specs=[
            pl.BlockSpec((tm, k), lambda j, i: (i, 0)),
            pl.BlockSpec((None, k, tn), lambda j, i: (l, 0, j)),
        ],
        out_specs=pl.BlockSpec((tm, tn), lambda j, i: (i, j)),
        out_shape=jax.ShapeDtypeStruct((m, n), out_dtype),
        compiler_params=_cp(("arbitrary", "arbitrary")),
        name=name,
    )(x, w)


def _wq_kernel(x_ref, w_ref, g_ref, o_ref):
    acc = _dot(x_ref[...], w_ref[...])
    g = g_ref[...]
    for c in range(acc.shape[1] // P_HALF):
        a = acc[:, c * P_HALF:(c + 1) * P_HALF]
        o_ref[:, c * P_HALF:(c + 1) * P_HALF] = _rms(a, g).astype(o_ref.dtype)


def peer_query(h2, wq, qn_g, l, tm):
    m = h2.shape[0]
    tn = 512
    return pl.pallas_call(
        _wq_kernel,
        grid=(D_MODEL // tn, m // tm),
        in_specs=[
            pl.BlockSpec((tm, D_MODEL), lambda j, i: (i, 0)),
            pl.BlockSpec((None, D_MODEL, tn), lambda j, i: (l, 0, j)),
            _lspec(l, 1, P_HALF),
        ],
        out_specs=pl.BlockSpec((tm, tn), lambda j, i: (i, j)),
        out_shape=jax.ShapeDtypeStruct((m, D_MODEL), BF16),
        compiler_params=_cp(("arbitrary", "arbitrary")),
        name="peer_query",
    )(h2, wq, qn_g)


def _qk_prep_kernel(zq_ref, zkv_ref, gq_ref, gk_ref, c_ref, s1_ref, s2_ref, qn_ref, kva_ref, kvb_ref):
    cos = c_ref[...]
    s1 = s1_ref[...]
    s2 = s2_ref[...]

    def norm_rope(x, g):
        xn = _rms(x, g)
        return xn * cos + pltpu.roll(xn, C_HD - C_ROT // 2, 1) * s1 + pltpu.roll(xn, C_ROT // 2, 1) * s2

    gq = gq_ref[...]
    for h in range(C_HEADS):
        x = zq_ref[:, h * C_HD:(h + 1) * C_HD].astype(F32)
        qn_ref[:, h * C_HD:(h + 1) * C_HD] = norm_rope(x, gq).astype(qn_ref.dtype)
    for s in range(6):
        for g in range(C_KV_HEADS):
            c0 = (s * C_KV_HEADS + g) * C_HD
            x = zkv_ref[:, c0:c0 + C_HD].astype(F32)
            if s % 2 == 0:
                x = norm_rope(x, gk_ref[s // 2:s // 2 + 1, :])
            if c0 < KVA_W:
                kva_ref[:, c0:c0 + C_HD] = x
            else:
                kvb_ref[:, c0 - KVA_W:c0 - KVA_W + C_HD] = x


def qk_prep(z1, gq, gk, l, rope_c, rope_s1, rope_s2, tm, is_sample):
    t = z1.shape[0]
    per = 1 if is_sample else SEQ // tm
    tab = pl.BlockSpec((tm, C_HD), lambda i: (i % per, 0))
    return pl.pallas_call(
        _qk_prep_kernel,
        grid=(t // tm,),
        in_specs=[
            pl.BlockSpec((tm, C_WIDTH), lambda i: (i, Z_CQ // C_WIDTH)),
            pl.BlockSpec((tm, 1536), lambda i: (i, Z_CKV // 1536)),
            _lspec(l, 1, C_HD),
            _lspec(l, 3, C_HD),
            tab, tab, tab,
        ],
        out_specs=[
            pl.BlockSpec((tm, C_WIDTH), lambda i: (i, 0)),
            pl.BlockSpec((tm, KVA_W), lambda i: (i, 0)),
            pl.BlockSpec((tm, KVB_W), lambda i: (i, 0)),
        ],
        out_shape=[
            jax.ShapeDtypeStruct((t, C_WIDTH), BF16),
            jax.ShapeDtypeStruct((t, KVA_W), F32),
            jax.ShapeDtypeStruct((t, KVB_W), F32),
        ],
        compiler_params=_cp(("arbitrary",)),
        name="qk_prep",
    )(z1, z1, gq, gk, rope_c, rope_s1, rope_s2)


def _chunk_mlp_kernel(au_ref, av_ref, lng_ref, lnb_ref, ws_ref, bst_ref, o_ref):
    u = _gelu(au_ref[...].astype(F32))
    v = _gelu(av_ref[...].astype(F32))
    mu = jnp.mean(v, axis=-1, keepdims=True)
    vc = v - mu
    var = jnp.mean(vc * vc, axis=-1, keepdims=True)
    v = vc * lax.rsqrt(var + EPS) * lng_ref[...] + lnb_ref[...]
    r = lax.broadcasted_iota(jnp.int32, (A_CHUNK, A_CHUNK), 0)
    c = lax.broadcasted_iota(jnp.int32, (A_CHUNK, A_CHUNK), 1)
    tril = r >= c
    tm = u.shape[0]
    for g in range(A_GROUPS):
        w = jnp.where(tril, ws_ref[g], 0.0).astype(BF16)
        bias = bst_ref[:, g:g + 1]
        for ch in range(tm // A_CHUNK):
            rows = slice(ch * A_CHUNK, (ch + 1) * A_CHUNK)
            cols = slice(g * A_GW, (g + 1) * A_GW)
            mixed = _dot(w, v[rows, cols].astype(BF16)) + bias
            o_ref[rows, cols] = (u[rows, cols] * mixed).astype(o_ref.dtype)


def chunk_mlp_prompt(z1, ln_g, ln_b, ws, bs_t, l):
    tm = 512
    return pl.pallas_call(
        _chunk_mlp_kernel,
        grid=(T_P // tm,),
        in_specs=[
            pl.BlockSpec((tm, A_WIDTH), lambda i: (i, Z_AU // A_WIDTH)),
            pl.BlockSpec((tm, A_WIDTH), lambda i: (i, Z_AV // A_WIDTH)),
            _lspec(l, 1, A_WIDTH),
            _lspec(l, 1, A_WIDTH),
            _lspec(l, A_GROUPS, A_CHUNK, A_CHUNK),
            _lspec(l, A_CHUNK, A_GROUPS),
        ],
        out_specs=pl.BlockSpec((tm, A_WIDTH), lambda i: (i, 0)),
        out_shape=jax.ShapeDtypeStruct((T_P, A_WIDTH), BF16),
        compiler_params=_cp(("arbitrary",)),
        name="chunk_mlp",
    )(z1, z1, ln_g, ln_b, ws, bs_t)


def _col_from_row(row):
    n = row.shape[1]
    r = lax.broadcasted_iota(jnp.int32, (n, n), 0)
    c = lax.broadcasted_iota(jnp.int32, (n, n), 1)
    return jnp.sum(jnp.where(r == c, jnp.broadcast_to(row, (n, n)), 0.0), axis=1, keepdims=True)


def _gla_kernel(q_ref, k_ref, v_ref, r_ref, bg_ref, gw_ref, gb_ref, on_ref, o_ref, s_out_ref, s_sc):
    n = pl.program_id(1)

    @pl.when(n == 0)
    def _():
        s_sc[...] = jnp.zeros_like(s_sc)

    c = B_CHUNK
    ri = lax.broadcasted_iota(jnp.int32, (c, c), 0)
    ci = lax.broadcasted_iota(jnp.int32, (c, c), 1)
    tril = ri >= ci
    eye = jnp.where(ri == ci, 1.0, 0.0).astype(BF16)
    la = _log_sigmoid(_dot(bg_ref[...].astype(BF16), gw_ref[...]) + gb_ref[...]) * (1.0 / B_GATE_TAU)
    bc = _dot_exact_lhs(jnp.where(tril, 1.0, 0.0).astype(BF16), la)
    q = q_ref[...].astype(F32) * (B_DK ** -0.5)
    k = k_ref[...].astype(F32)
    v = v_ref[...].astype(BF16)
    gate = r_ref[...].astype(F32)
    gate = gate * _sigmoid(gate)
    on = on_ref[...]
    for h in range(B_HEADS):
        dk = slice(h * B_DK, (h + 1) * B_DK)
        dv = slice(h * B_DV, (h + 1) * B_DV)
        b = bc[:, dk]
        qd = (q[:, dk] * jnp.exp(b)).astype(BF16)
        kd = (k[:, dk] * jnp.exp(-b)).astype(BF16)
        att = jnp.where(tril, _dot_nt(qd, kd), 0.0)
        s_old = s_sc[h]
        o = _dot(att.astype(BF16), v[:, dv]) + _dot(qd, s_old.astype(BF16))
        blast = b[c - 1:c, :]
        kdec = (k[:, dk] * jnp.exp(blast - b)).astype(BF16)
        kdec_t = _dot_nt(eye, kdec).astype(BF16)
        s_new = s_old * _col_from_row(jnp.exp(blast)) + _dot(kdec_t, v[:, dv])
        s_sc[h] = s_new
        o_ref[:, dv] = (_rms(o, on) * gate[:, dv]).astype(o_ref.dtype)

    @pl.when(n == pl.num_programs(1) - 1)
    def _():
        s_out_ref[...] = s_sc[...]


def gla_prompt(z1, z2, gw2p, gb, on_g, l):
    c = B_CHUNK
    nch = SEQ // c
    row = lambda b, n: b * nch + n
    return pl.pallas_call(
        _gla_kernel,
        grid=(BATCH, nch),
        in_specs=[
            pl.BlockSpec((c, 256), lambda b, n: (row(b, n), Z_BQ // 256)),
            pl.BlockSpec((c, 256), lambda b, n: (row(b, n), Z_BK // 256)),
            pl.BlockSpec((c, B_WIDTH), lambda b, n: (row(b, n), Z_BV // B_WIDTH)),
            pl.BlockSpec((c, B_WIDTH), lambda b, n: (row(b, n), Z_BR // B_WIDTH)),
            pl.BlockSpec((c, Z2_W), lambda b, n: (row(b, n), 0)),
            _lspec(l, Z2_W, 256),
            _lspec(l, 1, 256),
            _lspec(l, 1, B_DV),
        ],
        out_specs=[
            pl.BlockSpec((c, B_WIDTH), lambda b, n: (row(b, n), 0)),
            pl.BlockSpec((None, B_HEADS, B_DK, B_DV), lambda b, n: (b, 0, 0, 0)),
        ],
        out_shape=[
            jax.ShapeDtypeStruct((T_P, B_WIDTH), BF16),
            jax.ShapeDtypeStruct((BATCH, B_HEADS, B_DK, B_DV), F32),
        ],
        scratch_shapes=[pltpu.VMEM((B_HEADS, B_DK, B_DV), F32)],
        compiler_params=_cp(("arbitrary", "arbitrary")),
        name="gla",
    )(z1, z1, z1, z1, z2, gw2p, gb, on_g)


def _compress_kernel(pool_ref, rows_ref, w1_ref, b1_ref, w2_ref, b2_ref, o_ref, *, l):
    kv = pl.program_id(2)
    r = lax.broadcasted_iota(jnp.int32, (N_CMP, SEQ), 0)
    c = lax.broadcasted_iota(jnp.int32, (N_CMP, SEQ), 1)
    d = c - CMP_STRIDE * r
    p = jnp.zeros((N_CMP, SEQ), F32)
    for i in range(CMP_LEN):
        p = jnp.where(d == i, pool_ref[l, kv, i], p)
    rows = rows_ref[...]
    ph, pm, plo = _split3(p)
    rh, rm, rl = _split3(rows)
    pooled = (_dot(ph, rh) + (_dot(ph, rm) + _dot(pm, rh))) + ((_dot(pm, rm) + _dot(ph, rl)) + _dot(plo, rh))
    hid = _gelu(_dot(pooled.astype(BF16), w1_ref[...].astype(BF16)) + b1_ref[...])
    o_ref[...] = _dot(hid.astype(BF16), w2_ref[...].astype(BF16)) + b2_ref[...]


def compress_prompt(kva, pool, w1, b1, w2, b2, l):
    wspec = pl.BlockSpec((None, None, C_HD, C_HD), lambda b, g, kv: (l, kv, 0, 0))
    bspec = pl.BlockSpec((None, None, 1, C_HD), lambda b, g, kv: (l, kv, 0, 0))
    return pl.pallas_call(
        functools.partial(_compress_kernel, l=l),
        grid=(BATCH, C_KV_HEADS, 2),
        in_specs=[
            pl.BlockSpec(memory_space=pltpu.SMEM),
            pl.BlockSpec((SEQ, C_HD), lambda b, g, kv: (b, kv * C_KV_HEADS + g)),
            wspec, bspec, wspec, bspec,
        ],
        out_specs=pl.BlockSpec((None, None, None, N_CMP, C_HD), lambda b, g, kv: (b, g, kv, 0, 0)),
        out_shape=jax.ShapeDtypeStruct((BATCH, C_KV_HEADS, 2, N_CMP, C_HD), F32),
        compiler_params=_cp(("arbitrary", "arbitrary", "arbitrary")),
        name="compress",
    )(pool, kva, w1, b1, w2, b2)


def _softmax_rows(s, valid):
    s = jnp.where(valid, s, NEG)
    m = jnp.max(s, axis=-1, keepdims=True)
    e = jnp.where(valid, jnp.exp(s - m), 0.0)
    return e / jnp.maximum(jnp.sum(e, axis=-1, keepdims=True), 1e-30)


def _cmp_select_kernel(q_ref, ckv_ref, gate_ref, m5_ref, o_ref, selt_ref):
    g = pl.program_id(1)
    qi = pl.program_id(2)
    tq = q_ref.shape[0]
    ck = ckv_ref[0].astype(BF16)
    cv = ckv_ref[1].astype(BF16)
    pos = qi * tq + lax.broadcasted_iota(jnp.int32, (tq, N_CMP), 0)
    j = lax.broadcasted_iota(jnp.int32, (tq, N_CMP), 1)
    valid = j * CMP_STRIDE + (CMP_LEN - 1) <= pos
    gates = gate_ref[...]
    imp = jnp.zeros((tq, N_CMP), F32)
    for r in range(C_REP):
        q = q_ref[:, r * C_HD:(r + 1) * C_HD]
        p = _softmax_rows(_dot_nt(q, ck) * (C_HD ** -0.5), valid)
        imp = imp + p
        o = _dot(p.astype(BF16), cv)
        gcol = _gate_col(gates, Z2_CG + g * C_REP + r)
        o_ref[:, r * C_HD:(r + 1) * C_HD] = (o * _sigmoid(gcol)).astype(o_ref.dtype)
    score = _dot_exact_rhs(imp, m5_ref[...])
    cur = pos // SLC_BLK
    ok_blk = j * SLC_BLK <= pos
    forced = (j == 0) | (j == cur) | (j == cur - 1)
    score = jnp.where(ok_blk, score + jnp.where(forced, FORCE_BONUS, 0.0), NEG)
    score = jnp.where(j < SEQ // SLC_BLK, score, -3.0e38)
    rank = jnp.zeros((tq, N_CMP), F32)
    for jj in range(SEQ // SLC_BLK):
        col = score[:, jj:jj + 1]
        ahead = (col > score) | ((col == score) & (j > jj))
        rank = rank + jnp.where(ahead, 1.0, 0.0)
    sel = (rank < SLC_TOPK) & (score > 0.5 * NEG)
    selt_ref[...] = jnp.where(sel, 1.0, 0.0).T.astype(selt_ref.dtype)


def cmp_select_prompt(qn, ckv, z2, m5):
    tq = 512
    nq = SEQ // tq
    return pl.pallas_call(
        _cmp_select_kernel,
        grid=(BATCH, C_KV_HEADS, nq),
        in_specs=[
            pl.BlockSpec((tq, C_REP * C_HD), lambda b, g, i: (b * nq + i, g)),
            pl.BlockSpec((None, None, 2, N_CMP, C_HD), lambda b, g, i: (b, g, 0, 0, 0)),
            pl.BlockSpec((tq, Z2_W), lambda b, g, i: (b * nq + i, 0)),
            pl.BlockSpec((N_CMP, N_CMP), lambda b, g, i: (0, 0)),
        ],
        out_specs=[
            pl.BlockSpec((tq, C_REP * C_HD), lambda b, g, i: (b * nq + i, g)),
            pl.BlockSpec((None, None, N_CMP, tq), lambda b, g, i: (b, g, 0, i)),
        ],
        out_shape=[
            jax.ShapeDtypeStruct((T_P, C_WIDTH), BF16),
            jax.ShapeDtypeStruct((BATCH, C_KV_HEADS, N_CMP, SEQ), BF16),
        ],
        compiler_params=_cp(("arbitrary", "arbitrary", "arbitrary")),
        name="cmp_select",
    )(qn, ckv, z2, m5)


def _flash_kernel(q_ref, k_ref, v_ref, selt_ref, gate_ref, o_ref, vt_sc, m_sc, l_sc, acc_sc, *, branch, tk):
    g = pl.program_id(1)
    qi = pl.program_id(2)
    tq = q_ref.shape[0]

    @pl.when(qi == 0)
    def _():
        for jt in range(SEQ // tk):
            vt_sc[jt] = v_ref[jt * tk:(jt + 1) * tk, :].T

    q = jnp.concatenate([q_ref[:, r * C_HD:(r + 1) * C_HD] for r in range(C_REP)], axis=0)
    m_sc[...] = jnp.full(m_sc.shape, NEG, F32)
    l_sc[...] = jnp.zeros(l_sc.shape, F32)
    acc_sc[...] = jnp.zeros(acc_sc.shape, F32)
    pos = qi * tq + lax.broadcasted_iota(jnp.int32, (tk, tq), 1)
    hi = ((qi + 1) * tq + tk - 1) // tk
    if branch == 1:
        lo = 0
        selt = selt_ref[...]
    else:
        lo = jnp.maximum(qi * tq - WINDOW, 0) // tk

    def step(kj, carry):
        k0 = pl.multiple_of(kj * tk, tk)
        k = k_ref[pl.ds(k0, tk), :].astype(BF16)
        vt = vt_sc[kj].astype(BF16)
        key = k0 + lax.broadcasted_iota(jnp.int32, (tk, tq), 0)
        if branch == 1:
            kb = (k0 + lax.broadcasted_iota(jnp.int32, (tk, N_CMP), 0)) // SLC_BLK
            expand = jnp.where(kb == lax.broadcasted_iota(jnp.int32, (tk, N_CMP), 1), 1.0, 0.0).astype(BF16)
            valid = (_dot(expand, selt) > 0.5) & (key <= pos)
        else:
            valid = (key <= pos) & (key > pos - WINDOW)
        valid4 = jnp.concatenate([valid] * C_REP, axis=1)
        s = jnp.where(valid4, _dot_nt(k, q) * (C_HD ** -0.5), NEG)
        m_old = m_sc[...]
        m_new = jnp.maximum(m_old, jnp.max(s, axis=0, keepdims=True))
        alpha = jnp.exp(m_old - m_new)
        p = jnp.where(valid4, jnp.exp(s - m_new), 0.0)
        l_sc[...] = alpha * l_sc[...] + jnp.sum(p, axis=0, keepdims=True)
        acc_sc[...] = alpha * acc_sc[...] + _dot(vt, p.astype(BF16))
        m_sc[...] = m_new
        return carry

    lax.fori_loop(lo, hi, step, 0)
    out = (acc_sc[...] / jnp.maximum(l_sc[...], 1e-30)).T
    gates = gate_ref[...]
    for r in range(C_REP):
        gcol = _gate_col(gates, Z2_CG + branch * C_HEADS + g * C_REP + r)
        o_ref[:, r * C_HD:(r + 1) * C_HD] = (out[r * tq:(r + 1) * tq] * _sigmoid(gcol)).astype(o_ref.dtype)


def flash_prompt(qn, kv, selt, z2, branch):
    tq = 256
    tk = 256
    nq = SEQ // tq
    kcol = 2 * C_KV_HEADS if branch == 1 else 0
    kern = functools.partial(_flash_kernel, branch=branch, tk=tk)
    return pl.pallas_call(
        kern,
        grid=(BATCH, C_KV_HEADS, nq),
        in_specs=[
            pl.BlockSpec((tq, C_REP * C_HD), lambda b, g, i: (b * nq + i, g)),
            pl.BlockSpec((SEQ, C_HD), lambda b, g, i: (b, kcol + g)),
            pl.BlockSpec((SEQ, C_HD), lambda b, g, i: (b, kcol + C_KV_HEADS + g)),
            pl.BlockSpec((None, None, N_CMP, tq), lambda b, g, i: (b, g, 0, i)),
            pl.BlockSpec((tq, Z2_W), lambda b, g, i: (b * nq + i, 0)),
        ],
        out_specs=pl.BlockSpec((tq, C_REP * C_HD), lambda b, g, i: (b * nq + i, g)),
        out_shape=jax.ShapeDtypeStruct((T_P, C_WIDTH), BF16),
        scratch_shapes=[
            pltpu.VMEM((SEQ // tk, C_HD, tk), F32),
            pltpu.VMEM((1, C_REP * tq), F32),
            pltpu.VMEM((1, C_REP * tq), F32),
            pltpu.VMEM((C_HD, C_REP * tq), F32),
        ],
        compiler_params=_cp(("arbitrary", "arbitrary", "arbitrary")),
        name="flash_slc" if branch == 1 else "flash_win",
    )(qn, kv, kv, selt, z2)


def _merge_kernel(oa_ref, ob_ref, oc0_ref, oc1_ref, oc2_ref, ga_ref, gb_ref, gc_ref, wa_ref, wb_ref, wc_ref, o_ref):
    oc = (oc0_ref[...].astype(F32) + oc1_ref[...].astype(F32) + oc2_ref[...].astype(F32)).astype(BF16)
    m = _sigmoid(ga_ref[...].astype(F32)) * _dot(oa_ref[...], wa_ref[...])
    m = m + _sigmoid(gb_ref[...].astype(F32)) * _dot(ob_ref[...], wb_ref[...])
    m = m + _sigmoid(gc_ref[...].astype(F32)) * _dot(oc, wc_ref[...])
    o_ref[...] = m.astype(o_ref.dtype)


def merge(o_a, o_b, oc0, oc1, oc2, z1, wa, wb, wc, l, tm):
    t = o_a.shape[0]
    tn = 512
    nj = D_MODEL // tn
    gate = lambda k: pl.BlockSpec((tm, tn), lambda i, j: (i, (Z_MG + k * D_MODEL) // tn + j))
    wspec = lambda kdim: pl.BlockSpec((None, kdim, tn), lambda i, j: (l, 0, j))
    return pl.pallas_call(
        _merge_kernel,
        grid=(t // tm, nj),
        in_specs=[
            pl.BlockSpec((tm, A_WIDTH), lambda i, j: (i, 0)),
            pl.BlockSpec((tm, B_WIDTH), lambda i, j: (i, 0)),
            pl.BlockSpec((tm, C_WIDTH), lambda i, j: (i, 0)),
            pl.BlockSpec((tm, C_WIDTH), lambda i, j: (i, 0)),
            pl.BlockSpec((tm, C_WIDTH), lambda i, j: (i, 0)),
            gate(0), gate(1), gate(2),
            wspec(A_WIDTH), wspec(B_WIDTH), wspec(C_WIDTH),
        ],
        out_specs=pl.BlockSpec((tm, tn), lambda i, j: (i, j)),
        out_shape=jax.ShapeDtypeStruct((t, D_MODEL), BF16),
        compiler_params=_cp(("arbitrary", "arbitrary")),
        name="merge",
    )(o_a, o_b, oc0, oc1, oc2, z1, z1, z1, wa, wb, wc)


N_CELLS = 50
N_CELL_ROWS = 56


def _cell_tables():
    sa = np.zeros((N_CELL_ROWS, P_TOPK), np.float32)
    sb = np.zeros((N_CELL_ROWS, P_TOPK), np.float32)
    r = 0
    for a in range(P_TOPK):
        for b in range(P_TOPK // (a + 1)):
            sa[r, a] = 1.0
            sb[r, b] = 1.0
            r += 1
    assert r == N_CELLS
    return sa, sb


def _top16_cols(s, n_iota):
    rank = jnp.full(s.shape, float(P_TOPK), F32)
    work = s
    top = jnp.zeros((P_TOPK, s.shape[1]), F32)
    r_iota = lax.broadcasted_iota(jnp.int32, top.shape, 0)
    for r in range(P_TOPK):
        m = jnp.max(work, axis=0, keepdims=True)
        idx = jnp.min(jnp.where(work == m, n_iota, 1.0e9), axis=0, keepdims=True)
        hit = n_iota == idx
        rank = jnp.where(hit, float(r), rank)
        work = jnp.where(hit, -jnp.inf, work)
        top = jnp.where(r_iota == r, m, top)
    return rank, top


def _peer_topk_kernel(qn_ref, sk_ref, sa_ref, sb_ref, sat_ref, r2_ref, cnt_ref, e1_ref, e2_ref):
    tb = qn_ref.shape[0]
    lanes = 128
    sk1 = sk_ref[0].astype(BF16)
    sk2 = sk_ref[1].astype(BF16)
    s1_all = _dot_nt(sk1, qn_ref[:, 0:P_HALF])
    s2_all = _dot_nt(sk2, qn_ref[:, P_HALF:2 * P_HALF])
    n_iota = lax.broadcasted_iota(jnp.int32, (P_NKEYS, lanes), 0).astype(F32)
    c_iota = lax.broadcasted_iota(jnp.int32, (N_CELL_ROWS, lanes), 0).astype(F32)
    sa = sa_ref[...]
    sb = sb_ref[...]
    sat = sat_ref[...]
    for t in range(tb // lanes):
        cols = slice(t * lanes, (t + 1) * lanes)
        s1 = s1_all[:, cols]
        s2 = s2_all[:, cols]
        rank1, top1 = _top16_cols(s1, n_iota)
        rank2, top2 = _top16_cols(s2, n_iota)
        cand = _dot_exact_lhs(sa, top1) + _dot_exact_lhs(sb, top2)
        work = jnp.where(c_iota < N_CELLS, cand, -jnp.inf)
        picked = jnp.zeros(work.shape, F32)
        v0 = None
        zsum = None
        for r in range(P_TOPK):
            m = jnp.max(work, axis=0, keepdims=True)
            idx = jnp.min(jnp.where(work == m, c_iota, 1.0e9), axis=0, keepdims=True)
            hit = c_iota == idx
            picked = jnp.where(hit, 1.0, picked)
            work = jnp.where(hit, -jnp.inf, work)
            if r == 0:
                v0 = m
                zsum = jnp.ones_like(m)
            else:
                zsum = zsum + jnp.exp(m - v0)
        cnt = _dot(sat, picked.astype(BF16))
        cntd = jnp.zeros(rank1.shape, F32)
        for a in range(P_TOPK):
            cntd = jnp.where(rank1 == float(a), cnt[a:a + 1, :], cntd)
        e1 = jnp.where(rank1 < float(P_TOPK), jnp.exp(s1 - top1[0:1, :]), 0.0) / zsum
        e2 = jnp.exp(s2 - top2[0:1, :])
        r2_ref[:, cols] = rank2.astype(r2_ref.dtype)
        cnt_ref[:, cols] = cntd
        e1_ref[:, cols] = e1
        e2_ref[:, cols] = e2.astype(e2_ref.dtype)


def peer_topk(qn, subkeys, l, tb):
    t = qn.shape[0]
    sa, sb = _cell_tables()
    out = lambda dt: jax.ShapeDtypeStruct((P_HEADS, P_NKEYS, t), dt)
    ospec = pl.BlockSpec((None, P_NKEYS, tb), lambda i, h: (h, 0, i))
    return pl.pallas_call(
        _peer_topk_kernel,
        grid=(t // tb, P_HEADS),
        in_specs=[
            pl.BlockSpec((tb, 2 * P_HALF), lambda i, h: (i, h)),
            pl.BlockSpec((None, None, 2, P_NKEYS, P_HALF), lambda i, h: (l, h, 0, 0, 0)),
            pl.BlockSpec((N_CELL_ROWS, P_TOPK), lambda i, h: (0, 0)),
            pl.BlockSpec((N_CELL_ROWS, P_TOPK), lambda i, h: (0, 0)),
            pl.BlockSpec((P_TOPK, N_CELL_ROWS), lambda i, h: (0, 0)),
        ],
        out_specs=[ospec, ospec, ospec, ospec],
        out_shape=[out(BF16), out(F32), out(F32), out(BF16)],
        compiler_params=_cp(("arbitrary", "arbitrary")),
        name="peer_topk",
    )(qn, subkeys, jnp.asarray(sa, BF16), jnp.asarray(sb, BF16), jnp.asarray(sa.T, BF16))


def _peer_main_kernel(h_ref, u_ref, v_ref, r2_ref, cnt_ref, e1_ref, e2_ref, y_ref, *, ec):
    e = pl.program_id(1)

    @pl.when(e == 0)
    def _():
        y_ref[...] = jnp.zeros_like(y_ref)

    hid = _gelu(_dot_nt(u_ref[...], h_ref[...]))
    nchunk = ec // P_NKEYS
    zero = jnp.zeros((), BF16)
    parts = []
    for cc in range(nchunk):
        c = e * nchunk + cc
        w = None
        for h in range(P_HEADS):
            e1 = e1_ref[h, pl.ds(c, 1), :].astype(BF16)
            cn = cnt_ref[h, pl.ds(c, 1), :].astype(BF16)
            term = jnp.where(r2_ref[h] < cn, e2_ref[h], zero) * e1
            w = term if w is None else w + term
        parts.append(w.astype(F32) * hid[cc * P_NKEYS:(cc + 1) * P_NKEYS, :])
    p = jnp.concatenate(parts, axis=0)
    y_ref[...] += _dot(p.T.astype(BF16), v_ref[...])


def peer_main(h2, u_bf, v_bf, r2, cnt, e1, e2, l, tb):
    t = h2.shape[0]
    ec = 512
    gspec = pl.BlockSpec((P_HEADS, P_NKEYS, tb), lambda i, e: (0, 0, i))
    return pl.pallas_call(
        functools.partial(_peer_main_kernel, ec=ec),
        grid=(t // tb, P_NEXP // ec),
        in_specs=[
            pl.BlockSpec((tb, D_MODEL), lambda i, e: (i, 0)),
            pl.BlockSpec((None, ec, D_MODEL), lambda i, e: (l, e, 0)),
            pl.BlockSpec((None, ec, D_MODEL), lambda i, e: (l, e, 0)),
            gspec, gspec, gspec, gspec,
        ],
        out_specs=pl.BlockSpec((tb, D_MODEL), lambda i, e: (i, 0)),
        out_shape=jax.ShapeDtypeStruct((t, D_MODEL), F32),
        compiler_params=_cp(("arbitrary", "arbitrary")),
        name="peer_main",
    )(h2, u_bf, v_bf, r2, cnt, e1, e2)


def _rmsnorm_x(x, g):
    return x * lax.rsqrt(jnp.mean(x * x, axis=-1, keepdims=True) + EPS) * g


def _masked_softmax_x(s, valid):
    s = jnp.where(valid, s, NEG)
    m = jnp.max(s, axis=-1, keepdims=True)
    e = jnp.where(valid, jnp.exp(s - m), 0.0)
    return e / jnp.maximum(jnp.sum(e, axis=-1, keepdims=True), 1e-30)


def sample_mixers(z1, z2, qn, kva, kvb, P, cache_l, win_buf, gla_s0, page_table):
    db = DEC_BATCH
    z1 = z1[:db].astype(F32)
    z2 = z2[:db]
    a_u = jax.nn.gelu(z1[:, Z_AU:Z_AU + A_WIDTH])
    a_v = jax.nn.gelu(z1[:, Z_AV:Z_AV + A_WIDTH])
    mu = jnp.mean(a_v, axis=-1, keepdims=True)
    var = jnp.mean(jnp.square(a_v - mu), axis=-1, keepdims=True)
    a_v = (a_v - mu) * lax.rsqrt(var + EPS) * P['a_ln_g'] + P['a_ln_b']
    w00 = jnp.repeat(P['a_ws'][:, 0, 0], A_GW)
    b0 = jnp.repeat(P['a_bs'][:, 0], A_GW)
    o_a = a_u * (a_v * w00 + b0)
    b_g = z2[:, :B_GATE_RANK]
    log_a = jax.nn.log_sigmoid(b_g @ P['b_gw2'] + P['b_gb']) / B_GATE_TAU
    q = z1[:, Z_BQ:Z_BQ + 256].reshape(db, B_HEADS, B_DK) * (B_DK ** -0.5)
    k = z1[:, Z_BK:Z_BK + 256].reshape(db, B_HEADS, B_DK)
    v = z1[:, Z_BV:Z_BV + B_WIDTH].reshape(db, B_HEADS, B_DV)
    bl = log_a.reshape(db, B_HEADS, B_DK)
    qd = q * jnp.exp(bl)
    kd = k * jnp.exp(-bl)
    att = jnp.sum(qd * kd, axis=-1, keepdims=True)
    o = att * v + jnp.einsum('bhd,bhdv->bhv', qd, gla_s0)
    s_new = gla_s0 * jnp.exp(bl)[..., None] + k[..., None] * v[:, :, None, :]
    o_b = _rmsnorm_x(o, P['b_on_g']).reshape(db, B_WIDTH) * jax.nn.silu(z1[:, Z_BR:Z_BR + B_WIDTH])
    qs = qn[:db].astype(F32).reshape(db, C_KV_HEADS, C_REP, C_HD)
    kv = jnp.concatenate([kva[:db], kvb[:db]], axis=1).reshape(db, 6, C_KV_HEADS, C_HD)
    pos = PAST_LEN
    past = cache_l[page_table][:, :, :, 0:2].reshape(db, PAST_LEN, 2, C_KV_HEADS, C_HD)
    nseg = PAST_LEN // CMP_STRIDE
    seg = past.reshape(db, nseg, CMP_STRIDE, 2, C_KV_HEADS, C_HD)
    cks = []
    for i in range(2):
        first = jnp.einsum('bnigd,i->bngd', seg[:, :, :, i], P['cmp_pool'][i, :CMP_STRIDE])
        second = jnp.einsum('bnigd,i->bngd', seg[:, :, :, i], P['cmp_pool'][i, CMP_STRIDE:])
        pooled = first[:, :-1] + second[:, 1:]
        cks.append(jax.nn.gelu(pooled @ P['cmp_w1'][i] + P['cmp_b1'][i]) @ P['cmp_w2'][i] + P['cmp_b2'][i])
    ck, cv = cks
    ncmp = ck.shape[1]
    s = jnp.einsum('bgrd,bngd->bgrn', qs, ck) * (C_HD ** -0.5)
    end = jnp.arange(ncmp, dtype=jnp.int32) * CMP_STRIDE + (CMP_LEN - 1)
    p = _masked_softmax_x(s, (end <= pos)[None, None, None, :])
    o_cmp = jnp.einsum('bgrn,bngd->bgrd', p, cv)
    imp = jnp.sum(p, axis=2)
    n_slc = -(-(PAST_LEN + 1) // SLC_BLK)
    r4 = SLC_BLK // CMP_STRIDE
    pp = jnp.pad(imp, ((0, 0), (0, 0), (1, r4 * n_slc + r4 - 1 - ncmp)))
    blk = pp[..., :r4 * n_slc].reshape(db, C_KV_HEADS, n_slc, r4).sum(-1) + pp[..., r4::r4]
    j = jnp.arange(n_slc, dtype=jnp.int32)
    cur = pos // SLC_BLK
    forced = (j == 0) | (j == cur) | (j == cur - 1)
    score = jnp.where(j * SLC_BLK <= pos, blk + jnp.where(forced, FORCE_BONUS, 0.0), NEG)
    top, idx = lax.top_k(score, SLC_TOPK)
    ok = top > 0.5 * NEG
    n_past_blk = PAST_LEN // SLC_BLK
    bpp = PAGE_SIZE // SLC_BLK
    jc = jnp.minimum(idx, n_past_blk - 1)
    phys = page_table[jnp.arange(db)[:, None, None], jc // bpp][..., None]
    prow = (jc % bpp)[..., None] * SLC_BLK + jnp.arange(SLC_BLK, dtype=jnp.int32)
    gi = jnp.arange(C_KV_HEADS)[None, :, None, None]
    k_old = cache_l[phys, prow, 2, gi]
    v_old = cache_l[phys, prow, 3, gi]
    is_new = (idx >= n_past_blk)[..., None, None]
    first_row = (jnp.arange(SLC_BLK) == 0)[None, None, None, :, None]
    newk = jnp.where(first_row, kv[:, 2][:, :, None, None, :], 0.0)
    newv = jnp.where(first_row, kv[:, 3][:, :, None, None, :], 0.0)
    kb = jnp.where(is_new, newk, k_old)
    vb = jnp.where(is_new, newv, v_old)
    rows = idx[..., None] * SLC_BLK + jnp.arange(SLC_BLK, dtype=jnp.int32)
    s = jnp.einsum('bgrd,bgkld->bgrkl', qs, kb) * (C_HD ** -0.5)
    valid = (ok[..., None] & (rows <= pos)).reshape(db, C_KV_HEADS, 1, SLC_TOPK * SLC_BLK)
    p = _masked_softmax_x(s.reshape(db, C_KV_HEADS, C_REP, SLC_TOPK * SLC_BLK), valid)
    o_slc = jnp.einsum('bgrm,bgmd->bgrd', p, vb.reshape(db, C_KV_HEADS, SLC_TOPK * SLC_BLK, C_HD))
    wb = win_buf.shape[1]
    kk = jnp.concatenate([win_buf[:, :, 0], kv[:, 4][:, None]], axis=1)
    vv = jnp.concatenate([win_buf[:, :, 1], kv[:, 5][:, None]], axis=1)
    kp = jnp.concatenate([PAST_LEN - wb + jnp.arange(wb, dtype=jnp.int32), jnp.array([pos], jnp.int32)])
    s = jnp.einsum('bgrd,bkgd->bgrk', qs, kk) * (C_HD ** -0.5)
    p = _masked_softmax_x(s, ((kp <= pos) & (kp > pos - WINDOW))[None, None, None, :])
    o_win = jnp.einsum('bgrk,bkgd->bgrd', p, vv)
    gc = jax.nn.sigmoid(z2[:, Z2_CG:Z2_CG + 3 * C_HEADS]).reshape(db, 3, C_KV_HEADS, C_REP, 1)
    o_c = (gc[:, 0] * o_cmp + gc[:, 1] * o_slc + gc[:, 2] * o_win).reshape(db, C_WIDTH)

    def pad(a):
        return jnp.pad(a, ((0, T_S - db), (0, 0))).astype(BF16)

    return pad(o_a), pad(o_b), pad(o_c), a_v, s_new


def _rope_tables(pos):
    half = C_ROT // 2
    inv = jnp.float32(ROPE_THETA) ** (-jnp.arange(half, dtype=F32) / half)
    ang = pos.astype(F32)[:, None] * inv[None, :]
    cos = jnp.cos(ang)
    sin = jnp.sin(ang)
    n = pos.shape[0]
    ones = jnp.ones((n, C_HD - C_ROT), F32)
    zeros = jnp.zeros((n, C_HD - half), F32)
    c = jnp.concatenate([cos, cos, ones], axis=1)
    s1 = jnp.concatenate([-sin, zeros], axis=1)
    s2 = jnp.concatenate([jnp.zeros((n, half), F32), sin, jnp.zeros((n, C_HD - C_ROT), F32)], axis=1)
    return c, s1, s2


def _m5_table():
    m5 = np.zeros((N_CMP, N_CMP), np.float32)
    for j in range(SEQ // SLC_BLK):
        for n in range(4 * j - 1, 4 * j + 4):
            if 0 <= n < N_CMP - 1:
                m5[n, j] = 1.0
    return m5


def _permute_w_in(w_in):
    o = np.cumsum([0, 512, 512, 256, 256, 512, 16, 512, 1024, 1536, 24, 6144])
    seg = lambda i: w_in[..., o[i]:o[i + 1]]
    w1 = jnp.concatenate([seg(7), seg(0), seg(1), seg(4), seg(6), seg(8), seg(2), seg(3), seg(10)], axis=-1)
    pad = jnp.zeros(w_in.shape[:-1] + (Z2_W - 40,), w_in.dtype)
    w2 = jnp.concatenate([seg(5), seg(9), pad], axis=-1)
    return w1.astype(BF16), w2.astype(BF16)


def kernel(x_prompt, x_sample, cache_nsa_kv, state_win_kv, state_gla, page_table, c_prompt, c_sample, ada_w, ada_b, norm1_g, norm2_g, w_in, a_ln_g, a_ln_b, a_ws, a_bs, b_gw2, b_gb, b_on_g, c_qn_g, c_kn_g, cmp_pool, cmp_w1, cmp_b1, cmp_w2, cmp_b2, w_br_a, w_br_b, w_br_c, w_out, p_wq, p_qn_g, p_subkeys, p_u, p_v):
    tm_p, tm_s = 512, T_S
    w1_all, w2_all = _permute_w_in(w_in)
    wa_all = w_br_a.astype(BF16)
    wb_all = w_br_b.astype(BF16)
    wc_all = w_br_c.astype(BF16)
    wo_all = w_out.astype(BF16)
    wq_all = p_wq.astype(BF16)
    u_all = p_u.astype(BF16)
    v_all = p_v.astype(BF16)
    row3 = lambda a: a.reshape(DEPTH, 1, a.shape[-1])
    g1_all, g2_all = row3(norm1_g), row3(norm2_g)
    lng_all, lnb_all = row3(a_ln_g), row3(a_ln_b)
    bst_all = jnp.swapaxes(a_bs, 1, 2)
    gw2p_all = jnp.pad(b_gw2, ((0, 0), (0, Z2_W - B_GATE_RANK), (0, 0))).astype(BF16)
    gb_all, on_all = row3(b_gb), row3(b_on_g)
    gq_all, pqg_all = row3(c_qn_g), row3(p_qn_g)
    cb1_all = cmp_b1.reshape(DEPTH, 2, 1, C_HD)
    cb2_all = cmp_b2.reshape(DEPTH, 2, 1, C_HD)
    c40 = jnp.concatenate([jnp.repeat(c_prompt, N_SEQ_ROWS, axis=0), c_sample], axis=0)
    mod = adaln_table(c40, ada_w, ada_b)
    rope_p = _rope_tables(jnp.arange(SEQ, dtype=jnp.int32))
    rope_s = _rope_tables(jnp.full((T_S,), PAST_LEN, jnp.int32))
    m5 = jnp.asarray(_m5_table(), BF16)
    xp = x_prompt.reshape(T_P, D_MODEL)
    xs = jnp.pad(x_sample.reshape(DEC_BATCH, D_MODEL), ((0, T_S - DEC_BATCH), (0, 0)))

    outs = {k: [] for k in ('kv_p', 'kv_s', 'win_p', 'win_s', 'gla_p', 'gla_s', 'cv_s')}
    yp = ys = None
    for l in range(DEPTH):
        P = {
            'a_ln_g': a_ln_g[l], 'a_ln_b': a_ln_b[l], 'a_ws': a_ws[l], 'a_bs': a_bs[l], 'b_gw2': b_gw2[l],
            'b_gb': b_gb[l], 'b_on_g': b_on_g[l], 'cmp_pool': cmp_pool[l], 'cmp_w1': cmp_w1[l],
            'cmp_b1': cmp_b1[l], 'cmp_w2': cmp_w2[l], 'cmp_b2': cmp_b2[l],
        }
        streams = []
        for is_s, x, y, tm in ((False, xp, yp, tm_p), (True, xs, ys, tm_s)):
            if l == 0:
                h = norm_mod(x, g1_all, mod, l, 0, 1, tm, is_s)
            else:
                x, h = resid_norm_mod(x, y, g1_all, mod, l - 1, l, 5, 0, 1, tm, is_s)
            z1 = matmul(h, w1_all, l, tm, 1024, BF16, "w_in")
            z2 = matmul(h, w2_all, l, tm, Z2_W, F32, "w_in_gates")
            rope = rope_s if is_s else rope_p
            qn, kva, kvb = qk_prep(z1, gq_all, c_kn_g, l, rope[0], rope[1], rope[2], tm, is_s)
            streams.append((x, z1, z2, qn, kva, kvb))
        x, z1, z2, qn, kva, kvb = streams[0]
        o_a = chunk_mlp_prompt(z1, lng_all, lnb_all, a_ws, bst_all, l)
        o_b, s_p = gla_prompt(z1, z2, gw2p_all, gb_all, on_all, l)
        ckv = compress_prompt(kva, cmp_pool, cmp_w1, cb1_all, cmp_w2, cb2_all, l)
        o_cmp, selt = cmp_select_prompt(qn, ckv, z2, m5)
        o_slc = flash_prompt(qn, kva, selt, z2, 1)
        o_win = flash_prompt(qn, kvb, selt, z2, 2)
        mixed_p = (o_a, o_b, o_cmp, o_slc, o_win)
        outs['kv_p'].append(kva.reshape(BATCH, SEQ, 4, C_KV_HEADS, C_HD))
        outs['win_p'].append(kvb.reshape(BATCH, SEQ, 2, C_KV_HEADS, C_HD)[:, SEQ - WINDOW:])
        outs['gla_p'].append(s_p)
        xs_, z1s, z2s, qns, kvas, kvbs = streams[1]
        o_as, o_bs, o_cs, a_v_s, s_s = sample_mixers(z1s, z2s, qns, kvas, kvbs, P, cache_nsa_kv[l], state_win_kv[l], state_gla[l], page_table)
        zc = jnp.zeros_like(o_cs)
        mixed_s = (o_as, o_bs, o_cs, zc, zc)
        outs['kv_s'].append(kvas[:DEC_BATCH].reshape(DEC_BATCH, 1, 4, C_KV_HEADS, C_HD))
        outs['win_s'].append(kvbs[:DEC_BATCH].reshape(DEC_BATCH, 1, 2, C_KV_HEADS, C_HD))
        outs['gla_s'].append(s_s)
        outs['cv_s'].append(a_v_s[:, None, :])
        new = []
        for is_s, st, mixed, tm in ((False, streams[0], mixed_p, tm_p), (True, streams[1], mixed_s, tm_s)):
            x, z1 = st[0], st[1]
            mg = merge(*mixed, z1, wa_all, wb_all, wc_all, l, tm)
            att = matmul(mg, wo_all, l, tm, 1024, F32, "w_out")
            x1, h2 = resid_norm_mod(x, att, g2_all, mod, l, l, 2, 3, 4, tm, is_s)
            pq = peer_query(h2, wq_all, pqg_all, l, tm)
            r2, cnt, e1, e2 = peer_topk(pq, p_subkeys, l, tm)
            y = peer_main(h2, u_all, v_all, r2, cnt, e1, e2, l, tm)
            new.append((x1, y))
        (xp, yp), (xs, ys) = new
    xp = resid(xp, yp, mod, DEPTH - 1, 5, tm_p, False)
    xs = resid(xs, ys, mod, DEPTH - 1, 5, tm_s, True)
    return (
        xp.reshape(BATCH, SEQ, D_MODEL),
        xs[:DEC_BATCH].reshape(DEC_BATCH, 1, D_MODEL),
        jnp.stack(outs['kv_p']),
        jnp.stack(outs['kv_s']),
        jnp.stack(outs['win_p']),
        jnp.stack(outs['win_s']),
        jnp.stack(outs['gla_p']),
        jnp.stack(outs['gla_s']),
        jnp.stack(outs['cv_s']),
    )
```

```python
import functools

import numpy as np
import jax
import jax.numpy as jnp
from jax import lax
from jax.experimental import pallas as pl
from jax.experimental.pallas import tpu as pltpu

F32 = jnp.float32
BF16 = jnp.bfloat16

D_MODEL = 2048
BATCH = 4
SEQ = 2048
DEPTH = 4
DEC_BATCH = 8
PAST_LEN = 16384
PAGE_SIZE = 128
EPS = 1e-6
NEG = -1.0e30
A_WIDTH = 512
A_GROUPS = 4
A_GW = 128
A_CHUNK = 128
B_HEADS = 4
B_WIDTH = 512
B_DV = 128
B_DK = 64
B_GATE_RANK = 16
B_GATE_TAU = 16.0
B_CHUNK = 64
C_HEADS = 8
C_KV_HEADS = 2
C_HD = 128
C_WIDTH = 1024
C_REP = 4
C_ROT = 32
ROPE_THETA = 500000.0
CMP_LEN = 32
CMP_STRIDE = 16
SLC_BLK = 64
SLC_TOPK = 16
FORCE_BONUS = 1.0e4
WINDOW = 512
P_HEADS = 8
P_NKEYS = 128
P_NEXP = P_NKEYS * P_NKEYS
P_HALF = 128
P_TOPK = 16

T_P = BATCH * SEQ
T_S = 128
N_SEQ_ROWS = 8
N_CMP = SEQ // CMP_STRIDE

Z_CQ = 0
Z_AU = 1024
Z_AV = 1536
Z_BV = 2048
Z_BR = 2560
Z_CKV = 3072
Z_BQ = 4608
Z_BK = 4864
Z_MG = 5120
Z1_W = 11264
Z2_W = 128
Z2_CG = 16
KVA_W = 1024
KVB_W = 512

VMEM_LIMIT = 56 * 1024 * 1024


def _cp(sem, vmem=VMEM_LIMIT):
    return pltpu.CompilerParams(dimension_semantics=sem, vmem_limit_bytes=vmem)


def _lspec(l, *dims):
    nd = len(dims)
    return pl.BlockSpec((None,) + tuple(dims), lambda *_: (l,) + (0,) * nd)


def _gelu(x):
    return 0.5 * x * (1.0 + jnp.tanh(0.7978845608028654 * (x + 0.044715 * x * x * x)))


def _sigmoid(x):
    return 1.0 / (1.0 + jnp.exp(-x))


def _log_sigmoid(x):
    return jnp.minimum(x, 0.0) - jnp.log(1.0 + jnp.exp(-jnp.abs(x)))


def _dot(a, b):
    return jnp.dot(a, b, preferred_element_type=F32)


def _dot_nt(a, b):
    return lax.dot_general(a, b, (((1,), (1,)), ((), ())), preferred_element_type=F32)


def _split3(v):
    hi = v.astype(BF16)
    r1 = v - hi.astype(F32)
    mid = r1.astype(BF16)
    r2 = r1 - mid.astype(F32)
    lo = r2.astype(BF16)
    return hi, mid, lo


def _dot_exact_lhs(sel_bf16, v):
    hi, mid, lo = _split3(v)
    return (_dot(sel_bf16, hi) + _dot(sel_bf16, mid)) + _dot(sel_bf16, lo)


def _dot_exact_rhs(v, sel_bf16):
    hi, mid, lo = _split3(v)
    return (_dot(hi, sel_bf16) + _dot(mid, sel_bf16)) + _dot(lo, sel_bf16)


def _rms(x, g):
    return x * lax.rsqrt(jnp.mean(x * x, axis=-1, keepdims=True) + EPS) * g


def _mod_rows(y, scale8, shift8=None):
    tm, n = y.shape
    y3 = y.reshape(tm // N_SEQ_ROWS, N_SEQ_ROWS, n) * scale8[None]
    if shift8 is not None:
        y3 = y3 + shift8[None]
    return y3.reshape(tm, n)


def _gate_col(gates, col):
    lane = lax.broadcasted_iota(jnp.int32, gates.shape, 1)
    return jnp.sum(jnp.where(lane == col, gates, 0.0), axis=1, keepdims=True)


def _adaln_kernel(c_ref, w_ref, b_ref, o_ref):
    c = c_ref[...]
    s = (c * _sigmoid(c)).astype(BF16)
    o_ref[...] = _dot(s, w_ref[...].astype(BF16)) + b_ref[...]


def adaln_table(c40, ada_w, ada_b):
    tn = 1024
    n = 6 * D_MODEL
    return pl.pallas_call(
        _adaln_kernel,
        grid=(DEPTH, n // tn),
        in_specs=[
            pl.BlockSpec((5 * N_SEQ_ROWS, D_MODEL), lambda l, j: (0, 0)),
            pl.BlockSpec((None, D_MODEL, tn), lambda l, j: (l, 0, j)),
            pl.BlockSpec((None, 1, tn), lambda l, j: (l, 0, j)),
        ],
        out_specs=pl.BlockSpec((None, 5 * N_SEQ_ROWS, tn), lambda l, j: (l, 0, j)),
        out_shape=jax.ShapeDtypeStruct((DEPTH, 5 * N_SEQ_ROWS, n), F32),
        compiler_params=_cp(("arbitrary", "arbitrary")),
        name="adaln_table",
    )(c40, ada_w, ada_b.reshape(DEPTH, 1, n))


def _seq_of(tm, is_sample):
    if is_sample:
        return lambda i: 4
    per = SEQ // tm
    return lambda i: i // per


def _mod_spec(l, sq, k):
    return pl.BlockSpec((None, N_SEQ_ROWS, D_MODEL), lambda i: (l, sq(i), k))


def _norm_mod_kernel(x_ref, g_ref, sh_ref, sc_ref, h_ref):
    y = _rms(x_ref[...], g_ref[...])
    h_ref[...] = _mod_rows(y, 1.0 + sc_ref[...], sh_ref[...]).astype(h_ref.dtype)


def norm_mod(x, g, mod, l, k_shift, k_scale, tm, is_sample):
    t = x.shape[0]
    sq = _seq_of(tm, is_sample)
    return pl.pallas_call(
        _norm_mod_kernel,
        grid=(t // tm,),
        in_specs=[
            pl.BlockSpec((tm, D_MODEL), lambda i: (i, 0)),
            _lspec(l, 1, D_MODEL),
            _mod_spec(l, sq, k_shift),
            _mod_spec(l, sq, k_scale),
        ],
        out_specs=pl.BlockSpec((tm, D_MODEL), lambda i: (i, 0)),
        out_shape=jax.ShapeDtypeStruct((t, D_MODEL), BF16),
        compiler_params=_cp(("arbitrary",)),
        name="norm_mod",
    )(x, g, mod, mod)


def _resid_norm_mod_kernel(x_ref, y_ref, gate_ref, g_ref, sh_ref, sc_ref, xo_ref, h_ref):
    xn = x_ref[...] + _mod_rows(y_ref[...], gate_ref[...])
    xo_ref[...] = xn
    h_ref[...] = _mod_rows(_rms(xn, g_ref[...]), 1.0 + sc_ref[...], sh_ref[...]).astype(h_ref.dtype)


def resid_norm_mod(x, y, g, mod, l_gate, l, k_gate, k_shift, k_scale, tm, is_sample):
    t = x.shape[0]
    sq = _seq_of(tm, is_sample)
    row = pl.BlockSpec((tm, D_MODEL), lambda i: (i, 0))
    return pl.pallas_call(
        _resid_norm_mod_kernel,
        grid=(t // tm,),
        in_specs=[
            row,
            row,
            _mod_spec(l_gate, sq, k_gate),
            _lspec(l, 1, D_MODEL),
            _mod_spec(l, sq, k_shift),
            _mod_spec(l, sq, k_scale),
        ],
        out_specs=[row, row],
        out_shape=[jax.ShapeDtypeStruct((t, D_MODEL), F32), jax.ShapeDtypeStruct((t, D_MODEL), BF16)],
        compiler_params=_cp(("arbitrary",)),
        name="resid_norm_mod",
    )(x, y, mod, g, mod, mod)


def _resid_kernel(x_ref, y_ref, gate_ref, xo_ref):
    xo_ref[...] = x_ref[...] + _mod_rows(y_ref[...], gate_ref[...])


def resid(x, y, mod, l, k_gate, tm, is_sample):
    t = x.shape[0]
    sq = _seq_of(tm, is_sample)
    row = pl.BlockSpec((tm, D_MODEL), lambda i: (i, 0))
    return pl.pallas_call(
        _resid_kernel,
        grid=(t // tm,),
        in_specs=[row, row, _mod_spec(l, sq, k_gate)],
        out_specs=row,
        out_shape=jax.ShapeDtypeStruct((t, D_MODEL), F32),
        compiler_params=_cp(("arbitrary",)),
        name="resid",
    )(x, y, mod)


def _mm_kernel(x_ref, w_ref, o_ref):
    o_ref[...] = _dot(x_ref[...], w_ref[...]).astype(o_ref.dtype)


def matmul(x, w, l, tm, tn, out_dtype, name):
    m, k = x.shape
    n = w.shape[2]
    return pl.pallas_call(
        _mm_kernel,
        grid=(n // tn, m // tm),
        in_specs=[
            pl.BlockSpec((tm, k), lambda j, i: (i, 0)),
            pl.BlockSpec((None, k, tn), lambda j, i: (l, 0, j)),
        ],
        out_specs=pl.BlockSpec((tm, tn), lambda j, i: (i, j)),
        out_shape=jax.ShapeDtypeStruct((m, n), out_dtype),
        compiler_params=_cp(("arbitrary", "arbitrary")),
        name=name,
    )(x, w)


def _wq_kernel(x_ref, w_ref, g_ref, o_ref):
    acc = _dot(x_ref[...], w_ref[...])
    g = g_ref[...]
    for c in range(acc.shape[1] // P_HALF):
        a = acc[:, c * P_HALF:(c + 1) * P_HALF]
        o_ref[:, c * P_HALF:(c + 1) * P_HALF] = _rms(a, g).astype(o_ref.dtype)


def peer_query(h2, wq, qn_g, l, tm):
    m = h2.shape[0]
    tn = 512
    return pl.pallas_call(
        _wq_kernel,
        grid=(D_MODEL // tn, m // tm),
        in_specs=[
            pl.BlockSpec((tm, D_MODEL), lambda j, i: (i, 0)),
            pl.BlockSpec((None, D_MODEL, tn), lambda j, i: (l, 0, j)),
            _lspec(l, 1, P_HALF),
        ],
        out_specs=pl.BlockSpec((tm, tn), lambda j, i: (i, j)),
        out_shape=jax.ShapeDtypeStruct((m, D_MODEL), BF16),
        compiler_params=_cp(("arbitrary", "arbitrary")),
        name="peer_query",
    )(h2, wq, qn_g)


def _qk_prep_kernel(zq_ref, zkv_ref, gq_ref, gk_ref, c_ref, s1_ref, s2_ref, qn_ref, kva_ref, kvb_ref):
    cos = c_ref[...]
    s1 = s1_ref[...]
    s2 = s2_ref[...]

    def norm_rope(x, g):
        xn = _rms(x, g)
        return xn * cos + pltpu.roll(xn, C_HD - C_ROT // 2, 1) * s1 + pltpu.roll(xn, C_ROT // 2, 1) * s2

    gq = gq_ref[...]
    for h in range(C_HEADS):
        x = zq_ref[:, h * C_HD:(h + 1) * C_HD].astype(F32)
        qn_ref[:, h * C_HD:(h + 1) * C_HD] = norm_rope(x, gq).astype(qn_ref.dtype)
    for s in range(6):
        for g in range(C_KV_HEADS):
            c0 = (s * C_KV_HEADS + g) * C_HD
            x = zkv_ref[:, c0:c0 + C_HD].astype(F32)
            if s % 2 == 0:
                x = norm_rope(x, gk_ref[s // 2:s // 2 + 1, :])
            if c0 < KVA_W:
                kva_ref[:, c0:c0 + C_HD] = x
            else:
                kvb_ref[:, c0 - KVA_W:c0 - KVA_W + C_HD] = x


def qk_prep(z1, gq, gk, l, rope_c, rope_s1, rope_s2, tm, is_sample):
    t = z1.shape[0]
    per = 1 if is_sample else SEQ // tm
    tab = pl.BlockSpec((tm, C_HD), lambda i: (i % per, 0))
    return pl.pallas_call(
        _qk_prep_kernel,
        grid=(t // tm,),
        in_specs=[
            pl.BlockSpec((tm, C_WIDTH), lambda i: (i, Z_CQ // C_WIDTH)),
            pl.BlockSpec((tm, 1536), lambda i: (i, Z_CKV // 1536)),
            _lspec(l, 1, C_HD),
            _lspec(l, 3, C_HD),
            tab, tab, tab,
        ],
        out_specs=[
            pl.BlockSpec((tm, C_WIDTH), lambda i: (i, 0)),
            pl.BlockSpec((tm, KVA_W), lambda i: (i, 0)),
            pl.BlockSpec((tm, KVB_W), lambda i: (i, 0)),
        ],
        out_shape=[
            jax.ShapeDtypeStruct((t, C_WIDTH), BF16),
            jax.ShapeDtypeStruct((t, KVA_W), F32),
            jax.ShapeDtypeStruct((t, KVB_W), F32),
        ],
        compiler_params=_cp(("arbitrary",)),
        name="qk_prep",
    )(z1, z1, gq, gk, rope_c, rope_s1, rope_s2)


def _chunk_mlp_kernel(au_ref, av_ref, lng_ref, lnb_ref, ws_ref, bst_ref, o_ref):
    u = _gelu(au_ref[...].astype(F32))
    v = _gelu(av_ref[...].astype(F32))
    mu = jnp.mean(v, axis=-1, keepdims=True)
    vc = v - mu
    var = jnp.mean(vc * vc, axis=-1, keepdims=True)
    v = vc * lax.rsqrt(var + EPS) * lng_ref[...] + lnb_ref[...]
    r = lax.broadcasted_iota(jnp.int32, (A_CHUNK, A_CHUNK), 0)
    c = lax.broadcasted_iota(jnp.int32, (A_CHUNK, A_CHUNK), 1)
    tril = r >= c
    tm = u.shape[0]
    for g in range(A_GROUPS):
        w = jnp.where(tril, ws_ref[g], 0.0).astype(BF16)
        bias = bst_ref[:, g:g + 1]
        for ch in range(tm // A_CHUNK):
            rows = slice(ch * A_CHUNK, (ch + 1) * A_CHUNK)
            cols = slice(g * A_GW, (g + 1) * A_GW)
            mixed = _dot(w, v[rows, cols].astype(BF16)) + bias
            o_ref[rows, cols] = (u[rows, cols] * mixed).astype(o_ref.dtype)


def chunk_mlp_prompt(z1, ln_g, ln_b, ws, bs_t, l):
    tm = 512
    return pl.pallas_call(
        _chunk_mlp_kernel,
        grid=(T_P // tm,),
        in_specs=[
            pl.BlockSpec((tm, A_WIDTH), lambda i: (i, Z_AU // A_WIDTH)),
            pl.BlockSpec((tm, A_WIDTH), lambda i: (i, Z_AV // A_WIDTH)),
            _lspec(l, 1, A_WIDTH),
            _lspec(l, 1, A_WIDTH),
            _lspec(l, A_GROUPS, A_CHUNK, A_CHUNK),
            _lspec(l, A_CHUNK, A_GROUPS),
        ],
        out_specs=pl.BlockSpec((tm, A_WIDTH), lambda i: (i, 0)),
        out_shape=jax.ShapeDtypeStruct((T_P, A_WIDTH), BF16),
        compiler_params=_cp(("arbitrary",)),
        name="chunk_mlp",
    )(z1, z1, ln_g, ln_b, ws, bs_t)


def _col_from_row(row):
    n = row.shape[1]
    r = lax.broadcasted_iota(jnp.int32, (n, n), 0)
    c = lax.broadcasted_iota(jnp.int32, (n, n), 1)
    return jnp.sum(jnp.where(r == c, jnp.broadcast_to(row, (n, n)), 0.0), axis=1, keepdims=True)


def _gla_kernel(q_ref, k_ref, v_ref, r_ref, bg_ref, gw_ref, gb_ref, on_ref, o_ref, s_out_ref, s_sc):
    n = pl.program_id(1)

    @pl.when(n == 0)
    def _():
        s_sc[...] = jnp.zeros_like(s_sc)

    c = B_CHUNK
    ri = lax.broadcasted_iota(jnp.int32, (c, c), 0)
    ci = lax.broadcasted_iota(jnp.int32, (c, c), 1)
    tril = ri >= ci
    eye = jnp.where(ri == ci, 1.0, 0.0).astype(BF16)
    la = _log_sigmoid(_dot(bg_ref[...].astype(BF16), gw_ref[...]) + gb_ref[...]) * (1.0 / B_GATE_TAU)
    bc = _dot_exact_lhs(jnp.where(tril, 1.0, 0.0).astype(BF16), la)
    q = q_ref[...].astype(F32) * (B_DK ** -0.5)
    k = k_ref[...].astype(F32)
    v = v_ref[...].astype(BF16)
    gate = r_ref[...].astype(F32)
    gate = gate * _sigmoid(gate)
    on = on_ref[...]
    for h in range(B_HEADS):
        dk = slice(h * B_DK, (h + 1) * B_DK)
        dv = slice(h * B_DV, (h + 1) * B_DV)
        b = bc[:, dk]
        qd = (q[:, dk] * jnp.exp(b)).astype(BF16)
        kd = (k[:, dk] * jnp.exp(-b)).astype(BF16)
        att = jnp.where(tril, _dot_nt(qd, kd), 0.0)
        s_old = s_sc[h]
        o = _dot(att.astype(BF16), v[:, dv]) + _dot(qd, s_old.astype(BF16))
        blast = b[c - 1:c, :]
        kdec = (k[:, dk] * jnp.exp(blast - b)).astype(BF16)
        kdec_t = _dot_nt(eye, kdec).astype(BF16)
        s_new = s_old * _col_from_row(jnp.exp(blast)) + _dot(kdec_t, v[:, dv])
        s_sc[h] = s_new
        o_ref[:, dv] = (_rms(o, on) * gate[:, dv]).astype(o_ref.dtype)

    @pl.when(n == pl.num_programs(1) - 1)
    def _():
        s_out_ref[...] = s_sc[...]


def gla_prompt(z1, z2, gw2p, gb, on_g, l):
    c = B_CHUNK
    nch = SEQ // c
    row = lambda b, n: b * nch + n
    return pl.pallas_call(
        _gla_kernel,
        grid=(BATCH, nch),
        in_specs=[
            pl.BlockSpec((c, 256), lambda b, n: (row(b, n), Z_BQ // 256)),
            pl.BlockSpec((c, 256), lambda b, n: (row(b, n), Z_BK // 256)),
            pl.BlockSpec((c, B_WIDTH), lambda b, n: (row(b, n), Z_BV // B_WIDTH)),
            pl.BlockSpec((c, B_WIDTH), lambda b, n: (row(b, n), Z_BR // B_WIDTH)),
            pl.BlockSpec((c, Z2_W), lambda b, n: (row(b, n), 0)),
            _lspec(l, Z2_W, 256),
            _lspec(l, 1, 256),
            _lspec(l, 1, B_DV),
        ],
        out_specs=[
            pl.BlockSpec((c, B_WIDTH), lambda b, n: (row(b, n), 0)),
            pl.BlockSpec((None, B_HEADS, B_DK, B_DV), lambda b, n: (b, 0, 0, 0)),
        ],
        out_shape=[
            jax.ShapeDtypeStruct((T_P, B_WIDTH), BF16),
            jax.ShapeDtypeStruct((BATCH, B_HEADS, B_DK, B_DV), F32),
        ],
        scratch_shapes=[pltpu.VMEM((B_HEADS, B_DK, B_DV), F32)],
        compiler_params=_cp(("arbitrary", "arbitrary")),
        name="gla",
    )(z1, z1, z1, z1, z2, gw2p, gb, on_g)


def _compress_kernel(pool_ref, rows_ref, w1_ref, b1_ref, w2_ref, b2_ref, o_ref, *, l):
    kv = pl.program_id(2)
    r = lax.broadcasted_iota(jnp.int32, (N_CMP, SEQ), 0)
    c = lax.broadcasted_iota(jnp.int32, (N_CMP, SEQ), 1)
    d = c - CMP_STRIDE * r
    p = jnp.zeros((N_CMP, SEQ), F32)
    for i in range(CMP_LEN):
        p = jnp.where(d == i, pool_ref[l, kv, i], p)
    rows = rows_ref[...]
    ph, pm, plo = _split3(p)
    rh, rm, rl = _split3(rows)
    pooled = (_dot(ph, rh) + (_dot(ph, rm) + _dot(pm, rh))) + ((_dot(pm, rm) + _dot(ph, rl)) + _dot(plo, rh))
    hid = _gelu(_dot(pooled.astype(BF16), w1_ref[...].astype(BF16)) + b1_ref[...])
    o_ref[...] = _dot(hid.astype(BF16), w2_ref[...].astype(BF16)) + b2_ref[...]


def compress_prompt(kva, pool, w1, b1, w2, b2, l):
    wspec = pl.BlockSpec((None, None, C_HD, C_HD), lambda b, g, kv: (l, kv, 0, 0))
    bspec = pl.BlockSpec((None, None, 1, C_HD), lambda b, g, kv: (l, kv, 0, 0))
    return pl.pallas_call(
        functools.partial(_compress_kernel, l=l),
        grid=(BATCH, C_KV_HEADS, 2),
        in_specs=[
            pl.BlockSpec(memory_space=pltpu.SMEM),
            pl.BlockSpec((SEQ, C_HD), lambda b, g, kv: (b, kv * C_KV_HEADS + g)),
            wspec, bspec, wspec, bspec,
        ],
        out_specs=pl.BlockSpec((None, None, None, N_CMP, C_HD), lambda b, g, kv: (b, g, kv, 0, 0)),
        out_shape=jax.ShapeDtypeStruct((BATCH, C_KV_HEADS, 2, N_CMP, C_HD), F32),
        compiler_params=_cp(("arbitrary", "arbitrary", "arbitrary")),
        name="compress",
    )(pool, kva, w1, b1, w2, b2)


def _softmax_rows(s, valid):
    s = jnp.where(valid, s, NEG)
    m = jnp.max(s, axis=-1, keepdims=True)
    e = jnp.where(valid, jnp.exp(s - m), 0.0)
    return e / jnp.maximum(jnp.sum(e, axis=-1, keepdims=True), 1e-30)


def _cmp_select_kernel(q_ref, ckv_ref, gate_ref, m5_ref, o_ref, selt_ref):
    g = pl.program_id(1)
    qi = pl.program_id(2)
    tq = q_ref.shape[0]
    ck = ckv_ref[0].astype(BF16)
    cv = ckv_ref[1].astype(BF16)
    pos = qi * tq + lax.broadcasted_iota(jnp.int32, (tq, N_CMP), 0)
    j = lax.broadcasted_iota(jnp.int32, (tq, N_CMP), 1)
    valid = j * CMP_STRIDE + (CMP_LEN - 1) <= pos
    gates = gate_ref[...]
    imp = jnp.zeros((tq, N_CMP), F32)
    for r in range(C_REP):
        q = q_ref[:, r * C_HD:(r + 1) * C_HD]
        p = _softmax_rows(_dot_nt(q, ck) * (C_HD ** -0.5), valid)
        imp = imp + p
        o = _dot(p.astype(BF16), cv)
        gcol = _gate_col(gates, Z2_CG + g * C_REP + r)
        o_ref[:, r * C_HD:(r + 1) * C_HD] = (o * _sigmoid(gcol)).astype(o_ref.dtype)
    score = _dot_exact_rhs(imp, m5_ref[...])
    cur = pos // SLC_BLK
    ok_blk = j * SLC_BLK <= pos
    forced = (j == 0) | (j == cur) | (j == cur - 1)
    score = jnp.where(ok_blk, score + jnp.where(forced, FORCE_BONUS, 0.0), NEG)
    score = jnp.where(j < SEQ // SLC_BLK, score, -3.0e38)
    rank = jnp.zeros((tq, N_CMP), F32)
    for jj in range(SEQ // SLC_BLK):
        col = score[:, jj:jj + 1]
        ahead = (col > score) | ((col == score) & (j > jj))
        rank = rank + jnp.where(ahead, 1.0, 0.0)
    sel = (rank < SLC_TOPK) & (score > 0.5 * NEG)
    selt_ref[...] = jnp.where(sel, 1.0, 0.0).T.astype(selt_ref.dtype)


def cmp_select_prompt(qn, ckv, z2, m5):
    tq = 512
    nq = SEQ // tq
    return pl.pallas_call(
        _cmp_select_kernel,
        grid=(BATCH, C_KV_HEADS, nq),
        in_specs=[
            pl.BlockSpec((tq, C_REP * C_HD), lambda b, g, i: (b * nq + i, g)),
            pl.BlockSpec((None, None, 2, N_CMP, C_HD), lambda b, g, i: (b, g, 0, 0, 0)),
            pl.BlockSpec((tq, Z2_W), lambda b, g, i: (b * nq + i, 0)),
            pl.BlockSpec((N_CMP, N_CMP), lambda b, g, i: (0, 0)),
        ],
        out_specs=[
            pl.BlockSpec((tq, C_REP * C_HD), lambda b, g, i: (b * nq + i, g)),
            pl.BlockSpec((None, None, N_CMP, tq), lambda b, g, i: (b, g, 0, i)),
        ],
        out_shape=[
            jax.ShapeDtypeStruct((T_P, C_WIDTH), BF16),
            jax.ShapeDtypeStruct((BATCH, C_KV_HEADS, N_CMP, SEQ), BF16),
        ],
        compiler_params=_cp(("arbitrary", "arbitrary", "arbitrary")),
        name="cmp_select",
    )(qn, ckv, z2, m5)


def _flash_kernel(q_ref, k_ref, v_ref, selt_ref, gate_ref, o_ref, vt_sc, m_sc, l_sc, acc_sc, *, branch, tk):
    g = pl.program_id(1)
    qi = pl.program_id(2)
    tq = q_ref.shape[0]

    @pl.when(qi == 0)
    def _():
        for jt in range(SEQ // tk):
            vt_sc[jt] = v_ref[jt * tk:(jt + 1) * tk, :].T

    q = jnp.concatenate([q_ref[:, r * C_HD:(r + 1) * C_HD] for r in range(C_REP)], axis=0)
    m_sc[...] = jnp.full(m_sc.shape, NEG, F32)
    l_sc[...] = jnp.zeros(l_sc.shape, F32)
    acc_sc[...] = jnp.zeros(acc_sc.shape, F32)
    pos = qi * tq + lax.broadcasted_iota(jnp.int32, (tk, tq), 1)
    hi = ((qi + 1) * tq + tk - 1) // tk
    if branch == 1:
        lo = 0
        selt = selt_ref[...]
    else:
        lo = jnp.maximum(qi * tq - WINDOW, 0) // tk

    def step(kj, carry):
        k0 = pl.multiple_of(kj * tk, tk)
        k = k_ref[pl.ds(k0, tk), :].astype(BF16)
        vt = vt_sc[kj].astype(BF16)
        key = k0 + lax.broadcasted_iota(jnp.int32, (tk, tq), 0)
        if branch == 1:
            kb = (k0 + lax.broadcasted_iota(jnp.int32, (tk, N_CMP), 0)) // SLC_BLK
            expand = jnp.where(kb == lax.broadcasted_iota(jnp.int32, (tk, N_CMP), 1), 1.0, 0.0).astype(BF16)
            valid = (_dot(expand, selt) > 0.5) & (key <= pos)
        else:
            valid = (key <= pos) & (key > pos - WINDOW)
        valid4 = jnp.concatenate([valid] * C_REP, axis=1)
        s = jnp.where(valid4, _dot_nt(k, q) * (C_HD ** -0.5), NEG)
        m_old = m_sc[...]
        m_new = jnp.maximum(m_old, jnp.max(s, axis=0, keepdims=True))
        alpha = jnp.exp(m_old - m_new)
        p = jnp.where(valid4, jnp.exp(s - m_new), 0.0)
        l_sc[...] = alpha * l_sc[...] + jnp.sum(p, axis=0, keepdims=True)
        acc_sc[...] = alpha * acc_sc[...] + _dot(vt, p.astype(BF16))
        m_sc[...] = m_new
        return carry

    lax.fori_loop(lo, hi, step, 0)
    out = (acc_sc[...] / jnp.maximum(l_sc[...], 1e-30)).T
    gates = gate_ref[...]
    for r in range(C_REP):
        gcol = _gate_col(gates, Z2_CG + branch * C_HEADS + g * C_REP + r)
        o_ref[:, r * C_HD:(r + 1) * C_HD] = (out[r * tq:(r + 1) * tq] * _sigmoid(gcol)).astype(o_ref.dtype)


def flash_prompt(qn, kv, selt, z2, branch):
    tq = 256
    tk = 256
    nq = SEQ // tq
    kcol = 2 * C_KV_HEADS if branch == 1 else 0
    kern = functools.partial(_flash_kernel, branch=branch, tk=tk)
    return pl.pallas_call(
        kern,
        grid=(BATCH, C_KV_HEADS, nq),
        in_specs=[
            pl.BlockSpec((tq, C_REP * C_HD), lambda b, g, i: (b * nq + i, g)),
            pl.BlockSpec((SEQ, C_HD), lambda b, g, i: (b, kcol + g)),
            pl.BlockSpec((SEQ, C_HD), lambda b, g, i: (b, kcol + C_KV_HEADS + g)),
            pl.BlockSpec((None, None, N_CMP, tq), lambda b, g, i: (b, g, 0, i)),
            pl.BlockSpec((tq, Z2_W), lambda b, g, i: (b * nq + i, 0)),
        ],
        out_specs=pl.BlockSpec((tq, C_REP * C_HD), lambda b, g, i: (b * nq + i, g)),
        out_shape=jax.ShapeDtypeStruct((T_P, C_WIDTH), BF16),
        scratch_shapes=[
            pltpu.VMEM((SEQ // tk, C_HD, tk), F32),
            pltpu.VMEM((1, C_REP * tq), F32),
            pltpu.VMEM((1, C_REP * tq), F32),
            pltpu.VMEM((C_HD, C_REP * tq), F32),
        ],
        compiler_params=_cp(("arbitrary", "arbitrary", "arbitrary")),
        name="flash_slc" if branch == 1 else "flash_win",
    )(qn, kv, kv, selt, z2)


def _merge_kernel(oa_ref, ob_ref, oc0_ref, oc1_ref, oc2_ref, ga_ref, gb_ref, gc_ref, wa_ref, wb_ref, wc_ref, o_ref):
    oc = (oc0_ref[...].astype(F32) + oc1_ref[...].astype(F32) + oc2_ref[...].astype(F32)).astype(BF16)
    m = _sigmoid(ga_ref[...].astype(F32)) * _dot(oa_ref[...], wa_ref[...])
    m = m + _sigmoid(gb_ref[...].astype(F32)) * _dot(ob_ref[...], wb_ref[...])
    m = m + _sigmoid(gc_ref[...].astype(F32)) * _dot(oc, wc_ref[...])
    o_ref[...] = m.astype(o_ref.dtype)


def merge(o_a, o_b, oc0, oc1, oc2, z1, wa, wb, wc, l, tm):
    t = o_a.shape[0]
    tn = 512
    nj = D_MODEL // tn
    gate = lambda k: pl.BlockSpec((tm, tn), lambda i, j: (i, (Z_MG + k * D_MODEL) // tn + j))
    wspec = lambda kdim: pl.BlockSpec((None, kdim, tn), lambda i, j: (l, 0, j))
    return pl.pallas_call(
        _merge_kernel,
        grid=(t // tm, nj),
        in_specs=[
            pl.BlockSpec((tm, A_WIDTH), lambda i, j: (i, 0)),
            pl.BlockSpec((tm, B_WIDTH), lambda i, j: (i, 0)),
            pl.BlockSpec((tm, C_WIDTH), lambda i, j: (i, 0)),
            pl.BlockSpec((tm, C_WIDTH), lambda i, j: (i, 0)),
            pl.BlockSpec((tm, C_WIDTH), lambda i, j: (i, 0)),
            gate(0), gate(1), gate(2),
            wspec(A_WIDTH), wspec(B_WIDTH), wspec(C_WIDTH),
        ],
        out_specs=pl.BlockSpec((tm, tn), lambda i, j: (i, j)),
        out_shape=jax.ShapeDtypeStruct((t, D_MODEL), BF16),
        compiler_params=_cp(("arbitrary", "arbitrary")),
        name="merge",
    )(o_a, o_b, oc0, oc1, oc2, z1, z1, z1, wa, wb, wc)


N_CELLS = 50
N_CELL_ROWS = 56


def _cell_tables():
    sa = np.zeros((N_CELL_ROWS, P_TOPK), np.float32)
    sb = np.zeros((N_CELL_ROWS, P_TOPK), np.float32)
    r = 0
    for a in range(P_TOPK):
        for b in range(P_TOPK // (a + 1)):
            sa[r, a] = 1.0
            sb[r, b] = 1.0
            r += 1
    assert r == N_CELLS
    return sa, sb


def _top16_cols(s, n_iota):
    rank = jnp.full(s.shape, float(P_TOPK), F32)
    work = s
    top = jnp.zeros((P_TOPK, s.shape[1]), F32)
    r_iota = lax.broadcasted_iota(jnp.int32, top.shape, 0)
    for r in range(P_TOPK):
        m = jnp.max(work, axis=0, keepdims=True)
        idx = jnp.min(jnp.where(work == m, n_iota, 1.0e9), axis=0, keepdims=True)
        hit = n_iota == idx
        rank = jnp.where(hit, float(r), rank)
        work = jnp.where(hit, -jnp.inf, work)
        top = jnp.where(r_iota == r, m, top)
    return rank, top


def _peer_topk_kernel(qn_ref, sk_ref, sa_ref, sb_ref, sat_ref, r2_ref, cnt_ref, e1_ref, e2_ref):
    tb = qn_ref.shape[0]
    lanes = 128
    sk1 = sk_ref[0].astype(BF16)
    sk2 = sk_ref[1].astype(BF16)
    s1_all = _dot_nt(sk1, qn_ref[:, 0:P_HALF])
    s2_all = _dot_nt(sk2, qn_ref[:, P_HALF:2 * P_HALF])
    n_iota = lax.broadcasted_iota(jnp.int32, (P_NKEYS, lanes), 0).astype(F32)
    c_iota = lax.broadcasted_iota(jnp.int32, (N_CELL_ROWS, lanes), 0).astype(F32)
    sa = sa_ref[...]
    sb = sb_ref[...]
    sat = sat_ref[...]
    for t in range(tb // lanes):
        cols = slice(t * lanes, (t + 1) * lanes)
        s1 = s1_all[:, cols]
        s2 = s2_all[:, cols]
        rank1, top1 = _top16_cols(s1, n_iota)
        rank2, top2 = _top16_cols(s2, n_iota)
        cand = _dot_exact_lhs(sa, top1) + _dot_exact_lhs(sb, top2)
        work = jnp.where(c_iota < N_CELLS, cand, -jnp.inf)
        picked = jnp.zeros(work.shape, F32)
        v0 = None
        zsum = None
        for r in range(P_TOPK):
            m = jnp.max(work, axis=0, keepdims=True)
            idx = jnp.min(jnp.where(work == m, c_iota, 1.0e9), axis=0, keepdims=True)
            hit = c_iota == idx
            picked = jnp.where(hit, 1.0, picked)
            work = jnp.where(hit, -jnp.inf, work)
            if r == 0:
                v0 = m
                zsum = jnp.ones_like(m)
            else:
                zsum = zsum + jnp.exp(m - v0)
        cnt = _dot(sat, picked.astype(BF16))
        cntd = jnp.zeros(rank1.shape, F32)
        for a in range(P_TOPK):
            cntd = jnp.where(rank1 == float(a), cnt[a:a + 1, :], cntd)
        e1 = jnp.where(rank1 < float(P_TOPK), jnp.exp(s1 - top1[0:1, :]), 0.0) / zsum
        e2 = jnp.exp(s2 - top2[0:1, :])
        r2_ref[:, cols] = rank2.astype(r2_ref.dtype)
        cnt_ref[:, cols] = cntd
        e1_ref[:, cols] = e1
        e2_ref[:, cols] = e2.astype(e2_ref.dtype)


def peer_topk(qn, subkeys, l, tb):
    t = qn.shape[0]
    sa, sb = _cell_tables()
    out = lambda dt: jax.ShapeDtypeStruct((P_HEADS, P_NKEYS, t), dt)
    ospec = pl.BlockSpec((None, P_NKEYS, tb), lambda i, h: (h, 0, i))
    return pl.pallas_call(
        _peer_topk_kernel,
        grid=(t // tb, P_HEADS),
        in_specs=[
            pl.BlockSpec((tb, 2 * P_HALF), lambda i, h: (i, h)),
            pl.BlockSpec((None, None, 2, P_NKEYS, P_HALF), lambda i, h: (l, h, 0, 0, 0)),
            pl.BlockSpec((N_CELL_ROWS, P_TOPK), lambda i, h: (0, 0)),
            pl.BlockSpec((N_CELL_ROWS, P_TOPK), lambda i, h: (0, 0)),
            pl.BlockSpec((P_TOPK, N_CELL_ROWS), lambda i, h: (0, 0)),
        ],
        out_specs=[ospec, ospec, ospec, ospec],
        out_shape=[out(BF16), out(F32), out(F32), out(BF16)],
        compiler_params=_cp(("arbitrary", "arbitrary")),
        name="peer_topk",
    )(qn, subkeys, jnp.asarray(sa, BF16), jnp.asarray(sb, BF16), jnp.asarray(sa.T, BF16))


PEER_EC = 512
PEER_NGRP = P_NEXP // PEER_EC


def _peer_main_kernel(h_ref, ua_ref, ub_ref, va_ref, vb_ref, r2_ref, cnt_ref, e1_ref, e2_ref, y_ref, pa_sc, pb_sc):
    e = pl.program_id(1)
    last = pl.num_programs(1) - 1
    nchunk = PEER_EC // P_NKEYS

    @pl.when(e == 0)
    def _():
        y_ref[...] = jnp.zeros_like(y_ref)
        pb_sc[...] = jnp.zeros_like(pb_sc)

    def gated(scores, grp):
        hid = _gelu(scores)
        zero = jnp.zeros((), BF16)
        parts = []
        for cc in range(nchunk):
            c = grp * nchunk + cc
            w = None
            for h in range(P_HEADS):
                e1 = e1_ref[h, pl.ds(c, 1), :].astype(BF16)
                cn = cnt_ref[h, pl.ds(c, 1), :].astype(BF16)
                term = jnp.where(r2_ref[h] < cn, e2_ref[h], zero) * e1
                w = term if w is None else w + term
            parts.append(w.astype(F32) * hid[cc * P_NKEYS:(cc + 1) * P_NKEYS, :])
        return jnp.concatenate(parts, axis=0).T.astype(BF16)

    @pl.when(e < last)
    def _():
        sa = _dot_nt(ua_ref[...], h_ref[...])
        y_ref[...] += _dot(pb_sc[...], va_ref[...])
        sb = _dot_nt(ub_ref[...], h_ref[...])
        pa_sc[...] = gated(sa, 2 * e)
        y_ref[...] += _dot(pa_sc[...], vb_ref[...])
        pb_sc[...] = gated(sb, 2 * e + 1)

    @pl.when(e == last)
    def _():
        y_ref[...] += _dot(pb_sc[...], va_ref[...])


def peer_main(h2, u_bf, v_bf, r2, cnt, e1, e2, l, tb):
    t = h2.shape[0]
    nstep = PEER_NGRP // 2 + 1
    gspec = pl.BlockSpec((P_HEADS, P_NKEYS, tb), lambda i, e: (0, 0, i))
    grp = lambda f: pl.BlockSpec((None, PEER_EC, D_MODEL), lambda i, e: (l, jnp.clip(f(e), 0, PEER_NGRP - 1), 0))
    return pl.pallas_call(
        _peer_main_kernel,
        grid=(t // tb, nstep),
        in_specs=[
            pl.BlockSpec((tb, D_MODEL), lambda i, e: (i, 0)),
            grp(lambda e: 2 * e), grp(lambda e: 2 * e + 1),
            grp(lambda e: 2 * e - 1), grp(lambda e: 2 * e),
            gspec, gspec, gspec, gspec,
        ],
        out_specs=pl.BlockSpec((tb, D_MODEL), lambda i, e: (i, 0)),
        out_shape=jax.ShapeDtypeStruct((t, D_MODEL), F32),
        scratch_shapes=[pltpu.VMEM((tb, PEER_EC), BF16), pltpu.VMEM((tb, PEER_EC), BF16)],
        compiler_params=_cp(("arbitrary", "arbitrary")),
        name="peer_main",
    )(h2, u_bf, u_bf, v_bf, v_bf, r2, cnt, e1, e2)


N_PAGES = PAST_LEN // PAGE_SIZE
N_SEG = PAST_LEN // CMP_STRIDE
PAGES_PER_STEP = 8
SEG_PER_STEP = PAGES_PER_STEP * PAGE_SIZE // CMP_STRIDE
N_PAST_BLK = PAST_LEN // SLC_BLK
SLC_LANES = 384
S_ROWS = 16


def _row_of(x, b):
    r = lax.broadcasted_iota(jnp.int32, x.shape, 0)
    return jnp.sum(jnp.where(r == b, x, 0.0), axis=0, keepdims=True)


def _sample_pool_kernel(pt_ref, *refs):
    pages = refs[:PAGES_PER_STEP]
    wf_ref, ws_ref, o_ref, carry_sc = refs[PAGES_PER_STEP:]
    pc = pl.program_id(2)

    @pl.when(pc == 0)
    def _():
        carry_sc[...] = jnp.zeros_like(carry_sc)

    wf = wf_ref[...]
    ws = ws_ref[...]
    fs, ss = [], []
    for r in pages:
        xs = r[...].reshape(PAGE_SIZE // CMP_STRIDE, CMP_STRIDE, 4 * C_HD)
        fs.append(jnp.sum(xs * wf[None], axis=1))
        ss.append(jnp.sum(xs * ws[None], axis=1))
    first = jnp.concatenate(fs, axis=0)
    second = jnp.concatenate(ss, axis=0)
    row = lax.broadcasted_iota(jnp.int32, first.shape, 0)
    o_ref[...] = jnp.where(row == 0, carry_sc[...], pltpu.roll(first, 1, 0)) + second
    carry_sc[...] = first[SEG_PER_STEP - 1:, :]


def sample_pool(cache4, pt_flat, wf_all, ws_all):
    def page(k):
        return pl.BlockSpec((None, None, PAGE_SIZE, 4 * C_HD),
                            lambda l, b, pc, pt: (l, pt[b * N_PAGES + pc * PAGES_PER_STEP + k], 0, 0))

    wspec = pl.BlockSpec((None, CMP_STRIDE, 4 * C_HD), lambda l, b, pc, pt: (l, 0, 0))
    grid_spec = pltpu.PrefetchScalarGridSpec(
        num_scalar_prefetch=1,
        grid=(DEPTH, DEC_BATCH, N_PAGES // PAGES_PER_STEP),
        in_specs=[page(k) for k in range(PAGES_PER_STEP)] + [wspec, wspec],
        out_specs=pl.BlockSpec((None, None, SEG_PER_STEP, 4 * C_HD), lambda l, b, pc, pt: (l, b, pc, 0)),
        scratch_shapes=[pltpu.VMEM((1, 4 * C_HD), F32)],
    )
    return pl.pallas_call(
        _sample_pool_kernel,
        grid_spec=grid_spec,
        out_shape=jax.ShapeDtypeStruct((DEPTH, DEC_BATCH, N_SEG, 4 * C_HD), F32),
        compiler_params=_cp(("arbitrary", "arbitrary", "arbitrary")),
        name="sample_pool",
    )(pt_flat, *([cache4] * PAGES_PER_STEP), wf_all, ws_all)


def _sample_cmp_mlp_kernel(x_ref, w1_ref, b1_ref, w2_ref, b2_ref, o_ref):
    for kv in range(2):
        w1 = w1_ref[kv].astype(BF16)
        w2 = w2_ref[kv].astype(BF16)
        for g in range(C_KV_HEADS):
            cols = slice((kv * C_KV_HEADS + g) * C_HD, (kv * C_KV_HEADS + g + 1) * C_HD)
            hid = _gelu(_dot(x_ref[:, cols].astype(BF16), w1) + b1_ref[kv])
            o_ref[:, cols] = _dot(hid.astype(BF16), w2) + b2_ref[kv]


def sample_cmp_mlp(pooled, w1, b1, w2, b2):
    blk = pl.BlockSpec((None, None, N_SEG, 4 * C_HD), lambda l, b: (l, b, 0, 0))
    wspec = pl.BlockSpec((None, 2, C_HD, C_HD), lambda l, b: (l, 0, 0, 0))
    bspec = pl.BlockSpec((None, 2, 1, C_HD), lambda l, b: (l, 0, 0, 0))
    return pl.pallas_call(
        _sample_cmp_mlp_kernel,
        grid=(DEPTH, DEC_BATCH),
        in_specs=[blk, wspec, bspec, wspec, bspec],
        out_specs=blk,
        out_shape=jax.ShapeDtypeStruct(pooled.shape, F32),
        compiler_params=_cp(("arbitrary", "arbitrary")),
        name="sample_cmp_mlp",
    )(pooled, w1, b1, w2, b2)


def _sample_cmp_kernel(q_ref, ckv_ref, gate_ref, m5_ref, o_ref, idx_ref, ok_ref):
    b = pl.program_id(0)
    qrow = _row_of(q_ref[...].astype(F32), b)
    grow = _row_of(gate_ref[...], b)
    m_i = lax.broadcasted_iota(jnp.int32, (8, N_SEG), 1)
    valid = (m_i >= 1) & ((m_i - 1) * CMP_STRIDE + (CMP_LEN - 1) <= PAST_LEN)
    lane = lax.broadcasted_iota(jnp.int32, (1, Z2_W), 1)
    j = lax.broadcasted_iota(jnp.int32, (8, SLC_LANES), 1)
    jf = j.astype(F32)
    cur = PAST_LEN // SLC_BLK
    forced = (j == 0) | (j == cur) | (j == cur - 1)
    idx_row = jnp.zeros((1, Z2_W), F32)
    ok_row = jnp.zeros((1, Z2_W), F32)
    for g in range(C_KV_HEADS):
        ck = ckv_ref[:, g * C_HD:(g + 1) * C_HD].astype(BF16)
        cv = ckv_ref[:, (C_KV_HEADS + g) * C_HD:(C_KV_HEADS + g + 1) * C_HD].astype(BF16)
        imp = jnp.zeros((8, N_SEG), F32)
        for r in range(C_REP):
            h = g * C_REP + r
            q8 = jnp.broadcast_to(qrow[:, h * C_HD:(h + 1) * C_HD], (8, C_HD)).astype(BF16)
            p = _softmax_rows(_dot_nt(q8, ck) * (C_HD ** -0.5), valid)
            imp = imp + p
            o = _dot(p.astype(BF16), cv)
            gate = jnp.sum(jnp.where(lane == Z2_CG + h, grow, 0.0), axis=1, keepdims=True)
            o_ref[:, h * C_HD:(h + 1) * C_HD] = o[0:1] * _sigmoid(gate)
        score = _dot_exact_rhs(imp, m5_ref[...])
        score = jnp.where(j * SLC_BLK <= PAST_LEN, score + jnp.where(forced, FORCE_BONUS, 0.0), -3.0e38)
        for r in range(SLC_TOPK):
            m = jnp.max(score, axis=1, keepdims=True)
            idx = jnp.min(jnp.where(score == m, jf, 1.0e9), axis=1, keepdims=True)
            slot = lane == g * SLC_TOPK + r
            idx_row = jnp.where(slot, idx[0:1], idx_row)
            ok_row = jnp.where(slot, jnp.where(m[0:1] > 0.5 * NEG, 1.0, 0.0), ok_row)
            score = jnp.where(jf == idx, -3.4e38, score)
    idx_ref[...] = idx_row.astype(jnp.int32)
    ok_ref[...] = ok_row.astype(jnp.int32)


def sample_cmp(qn, ckv_all, z2, m5s, l):
    small = lambda n, dt: jax.ShapeDtypeStruct((DEC_BATCH, 1, n), dt)
    ospec = lambda n: pl.BlockSpec((None, 1, n), lambda b: (b, 0, 0))
    return pl.pallas_call(
        _sample_cmp_kernel,
        grid=(DEC_BATCH,),
        in_specs=[
            pl.BlockSpec((S_ROWS, C_WIDTH), lambda b: (0, 0)),
            pl.BlockSpec((None, None, N_SEG, 4 * C_HD), lambda b: (l, b, 0, 0)),
            pl.BlockSpec((S_ROWS, Z2_W), lambda b: (0, 0)),
            pl.BlockSpec((N_SEG, SLC_LANES), lambda b: (0, 0)),
        ],
        out_specs=[ospec(C_WIDTH), ospec(Z2_W), ospec(Z2_W)],
        out_shape=[small(C_WIDTH, F32), small(Z2_W, jnp.int32), small(Z2_W, jnp.int32)],
        compiler_params=_cp(("arbitrary",)),
        name="sample_cmp",
    )(qn, ckv_all, z2, m5s)


def _sample_attend_kernel(phys_ref, rb_ref, idx_ref, ok_ref, *refs):
    kblk = refs[:SLC_TOPK]
    vblk = refs[SLC_TOPK:2 * SLC_TOPK]
    (q_ref, ks_ref, vs_ref, kwn_ref, vwn_ref, kw_ref, vw_ref, gate_ref,
     oslc_ref, owin_ref, kcat_sc, vcat_sc) = refs[2 * SLC_TOPK:]
    b = pl.program_id(0)
    g = pl.program_id(1)
    base = (b * C_KV_HEADS + g) * SLC_TOPK
    scale = C_HD ** -0.5
    qrow = _row_of(q_ref[...].astype(F32), b)
    r8 = lax.broadcasted_iota(jnp.int32, (8, C_HD), 0)
    q4 = jnp.zeros((8, C_HD), F32)
    for r in range(C_REP):
        q4 = jnp.where(r8 == r, jnp.broadcast_to(qrow[:, r * C_HD:(r + 1) * C_HD], (8, C_HD)), q4)
    q4 = q4.astype(BF16)
    r64 = lax.broadcasted_iota(jnp.int32, (SLC_BLK, C_HD), 0)
    newk = jnp.where(r64 == 0, jnp.broadcast_to(_row_of(ks_ref[...], b), (SLC_BLK, C_HD)), 0.0)
    newv = jnp.where(r64 == 0, jnp.broadcast_to(_row_of(vs_ref[...], b), (SLC_BLK, C_HD)), 0.0)
    lane = lax.broadcasted_iota(jnp.int32, (8, SLC_TOPK * SLC_BLK), 1)
    rowpos = lane % SLC_BLK
    okv = jnp.zeros(lane.shape, jnp.int32)
    for k in range(SLC_TOPK):
        idx_k = idx_ref[base + k]
        is_new = jnp.full((SLC_BLK, C_HD), idx_k, jnp.int32) >= N_PAST_BLK
        kcat_sc[k * SLC_BLK:(k + 1) * SLC_BLK, :] = jnp.where(is_new, newk, kblk[k][...])
        vcat_sc[k * SLC_BLK:(k + 1) * SLC_BLK, :] = jnp.where(is_new, newv, vblk[k][...])
        in_k = lane // SLC_BLK == k
        rowpos = jnp.where(in_k, rowpos + idx_k * SLC_BLK, rowpos)
        okv = jnp.where(in_k, ok_ref[base + k], okv)
    s = _dot_nt(q4, kcat_sc[...].astype(BF16)) * scale
    p = _softmax_rows(s, (okv > 0) & (rowpos <= PAST_LEN))
    o_slc = _dot(p.astype(BF16), vcat_sc[...].astype(BF16))
    kp = PAST_LEN - WINDOW + lax.broadcasted_iota(jnp.int32, (8, WINDOW), 1)
    validw = (kp <= PAST_LEN) & (kp > PAST_LEN - WINDOW)
    sw = jnp.where(validw, _dot_nt(q4, kw_ref[...].astype(BF16)) * scale, NEG)
    knew = _row_of(kwn_ref[...], b).astype(BF16).astype(F32)
    vnew = _row_of(vwn_ref[...], b)
    sn = jnp.sum(q4.astype(F32) * knew, axis=1, keepdims=True) * scale
    m = jnp.maximum(jnp.max(sw, axis=1, keepdims=True), sn)
    ew = jnp.where(validw, jnp.exp(sw - m), 0.0)
    en = jnp.exp(sn - m)
    den = jnp.maximum(jnp.sum(ew, axis=1, keepdims=True) + en, 1e-30)
    o_win = (_dot(ew.astype(BF16), vw_ref[...].astype(BF16)) + en * vnew) / den
    grow = _row_of(gate_ref[...], b)
    lane_g = lax.broadcasted_iota(jnp.int32, grow.shape, 1)
    r1 = lax.broadcasted_iota(jnp.int32, (8, 1), 0)
    for branch, o, o_ref in ((1, o_slc, oslc_ref), (2, o_win, owin_ref)):
        gcol = jnp.zeros((8, 1), F32)
        for r in range(C_REP):
            col = Z2_CG + branch * C_HEADS + g * C_REP + r
            gcol = jnp.where(r1 == r, jnp.sum(jnp.where(lane_g == col, grow, 0.0), axis=1, keepdims=True), gcol)
        o_ref[...] = o * _sigmoid(gcol)


def sample_attend(cache4, win4, phys, rb, idx, ok, qn, kva, kvb, z2, l):
    def cblk(k, slot):
        return pl.BlockSpec((None, None, SLC_BLK, C_HD),
                            lambda b, g, ph, rbr, ix, okr: (l, ph[(b * C_KV_HEADS + g) * SLC_TOPK + k],
                                                            rbr[(b * C_KV_HEADS + g) * SLC_TOPK + k], slot * C_KV_HEADS + g))

    col = lambda w, f: pl.BlockSpec((S_ROWS, w), lambda b, g, *_: (0, f(g)))
    wbuf = lambda kv: pl.BlockSpec((None, None, WINDOW, C_HD), lambda b, g, *_: (l, b, 0, kv * C_KV_HEADS + g))
    ospec = pl.BlockSpec((None, None, 8, C_HD), lambda b, g, *_: (b, g, 0, 0))
    grid_spec = pltpu.PrefetchScalarGridSpec(
        num_scalar_prefetch=4,
        grid=(DEC_BATCH, C_KV_HEADS),
        in_specs=[cblk(k, 2) for k in range(SLC_TOPK)] + [cblk(k, 3) for k in range(SLC_TOPK)] + [
            col(C_REP * C_HD, lambda g: g),
            col(C_HD, lambda g: 2 * C_KV_HEADS + g),
            col(C_HD, lambda g: 3 * C_KV_HEADS + g),
            col(C_HD, lambda g: g),
            col(C_HD, lambda g: C_KV_HEADS + g),
            wbuf(0), wbuf(1),
            pl.BlockSpec((S_ROWS, Z2_W), lambda b, g, *_: (0, 0)),
        ],
        out_specs=[ospec, ospec],
        scratch_shapes=[pltpu.VMEM((SLC_TOPK * SLC_BLK, C_HD), F32), pltpu.VMEM((SLC_TOPK * SLC_BLK, C_HD), F32)],
    )
    out = jax.ShapeDtypeStruct((DEC_BATCH, C_KV_HEADS, 8, C_HD), F32)
    return pl.pallas_call(
        _sample_attend_kernel,
        grid_spec=grid_spec,
        out_shape=[out, out],
        compiler_params=_cp(("arbitrary", "arbitrary")),
        name="sample_attend",
    )(phys, rb, idx, ok, *([cache4] * (2 * SLC_TOPK)), qn, kva, kva, kvb, kvb, win4, win4, z2)


def _sample_ab_kernel(au_ref, av_ref, bq_ref, bk_ref, bv_ref, br_ref, z2_ref, lng_ref, lnb_ref, w00_ref, b0_ref,
                      gw_ref, gb_ref, on_ref, s0_ref, oa_ref, ob_ref, av_out_ref, s_ref):
    u = _gelu(au_ref[...].astype(F32))
    v = _gelu(av_ref[...].astype(F32))
    mu = jnp.mean(v, axis=-1, keepdims=True)
    vc = v - mu
    v = vc * lax.rsqrt(jnp.mean(vc * vc, axis=-1, keepdims=True) + EPS) * lng_ref[...] + lnb_ref[...]
    av_out_ref[...] = v
    oa_ref[...] = u * (v * w00_ref[...] + b0_ref[...])
    la = _log_sigmoid(_dot(z2_ref[...].astype(BF16), gw_ref[...]) + gb_ref[...]) * (1.0 / B_GATE_TAU)
    q = bq_ref[...].astype(F32) * (B_DK ** -0.5)
    k = bk_ref[...].astype(F32)
    vv = bv_ref[...].astype(F32)
    gate = br_ref[...].astype(F32)
    gate = gate * _sigmoid(gate)
    qd = q * jnp.exp(la)
    kd = k * jnp.exp(-la)
    on = on_ref[...]
    ob_ref[...] = jnp.zeros_like(ob_ref)
    for b in range(DEC_BATCH):
        for h in range(B_HEADS):
            dk = slice(h * B_DK, (h + 1) * B_DK)
            dv = slice(h * B_DV, (h + 1) * B_DV)
            qd_r = qd[b:b + 1, dk]
            v_r = vv[b:b + 1, dv]
            att = jnp.sum(qd_r * kd[b:b + 1, dk], axis=1, keepdims=True)
            s0 = s0_ref[b, h]
            o = att * v_r + _dot(jnp.broadcast_to(qd_r, (8, B_DK)).astype(BF16), s0.astype(BF16))[0:1]
            s_ref[b, h] = s0 * _col_from_row(jnp.exp(la[b:b + 1, dk])) + _col_from_row(k[b:b + 1, dk]) * v_r
            ob_ref[b:b + 1, dv] = _rms(o, on) * gate[b:b + 1, dv]


def sample_ab(z1, z2, lng, lnb, w00, b0, gw2p, gb, on_g, state_gla, l):
    blk = lambda w, off: pl.BlockSpec((S_ROWS, w), lambda i: (0, off // w))
    row = lambda n: jax.ShapeDtypeStruct((S_ROWS, n), F32)
    rspec = lambda n: pl.BlockSpec((S_ROWS, n), lambda i: (0, 0))
    sspec = pl.BlockSpec((None, DEC_BATCH, B_HEADS, B_DK, B_DV), lambda i: (l, 0, 0, 0, 0))
    return pl.pallas_call(
        _sample_ab_kernel,
        grid=(1,),
        in_specs=[
            blk(A_WIDTH, Z_AU), blk(A_WIDTH, Z_AV), blk(256, Z_BQ), blk(256, Z_BK), blk(B_WIDTH, Z_BV),
            blk(B_WIDTH, Z_BR), blk(Z2_W, 0),
            _lspec(l, 1, A_WIDTH), _lspec(l, 1, A_WIDTH), _lspec(l, 1, A_WIDTH), _lspec(l, 1, A_WIDTH),
            _lspec(l, Z2_W, 256), _lspec(l, 1, 256), _lspec(l, 1, B_DV),
            sspec,
        ],
        out_specs=[rspec(A_WIDTH), rspec(B_WIDTH), rspec(A_WIDTH),
                   pl.BlockSpec((DEC_BATCH, B_HEADS, B_DK, B_DV), lambda i: (0, 0, 0, 0))],
        out_shape=[row(A_WIDTH), row(B_WIDTH), row(A_WIDTH),
                   jax.ShapeDtypeStruct((DEC_BATCH, B_HEADS, B_DK, B_DV), F32)],
        compiler_params=_cp(("arbitrary",)),
        name="sample_ab",
    )(z1, z1, z1, z1, z1, z1, z2, lng, lnb, w00, b0, gw2p, gb, on_g, state_gla)


def _m5s_table():
    m5 = np.zeros((N_SEG, SLC_LANES), np.float32)
    for j in range(N_PAST_BLK + 1):
        for m in range(max(4 * j, 1), min(4 * j + 4, N_SEG - 1) + 1):
            m5[m, j] = 1.0
    return m5


def _rope_tables(pos):
    half = C_ROT // 2
    inv = jnp.float32(ROPE_THETA) ** (-jnp.arange(half, dtype=F32) / half)
    ang = pos.astype(F32)[:, None] * inv[None, :]
    cos = jnp.cos(ang)
    sin = jnp.sin(ang)
    n = pos.shape[0]
    ones = jnp.ones((n, C_HD - C_ROT), F32)
    zeros = jnp.zeros((n, C_HD - half), F32)
    c = jnp.concatenate([cos, cos, ones], axis=1)
    s1 = jnp.concatenate([-sin, zeros], axis=1)
    s2 = jnp.concatenate([jnp.zeros((n, half), F32), sin, jnp.zeros((n, C_HD - C_ROT), F32)], axis=1)
    return c, s1, s2


def _m5_table():
    m5 = np.zeros((N_CMP, N_CMP), np.float32)
    for j in range(SEQ // SLC_BLK):
        for n in range(4 * j - 1, 4 * j + 4):
            if 0 <= n < N_CMP - 1:
                m5[n, j] = 1.0
    return m5


def _permute_w_in(w_in):
    o = np.cumsum([0, 512, 512, 256, 256, 512, 16, 512, 1024, 1536, 24, 6144])
    seg = lambda i: w_in[..., o[i]:o[i + 1]]
    w1 = jnp.concatenate([seg(7), seg(0), seg(1), seg(4), seg(6), seg(8), seg(2), seg(3), seg(10)], axis=-1)
    pad = jnp.zeros(w_in.shape[:-1] + (Z2_W - 40,), w_in.dtype)
    w2 = jnp.concatenate([seg(5), seg(9), pad], axis=-1)
    return w1.astype(BF16), w2.astype(BF16)


def kernel(x_prompt, x_sample, cache_nsa_kv, state_win_kv, state_gla, page_table, c_prompt, c_sample, ada_w, ada_b, norm1_g, norm2_g, w_in, a_ln_g, a_ln_b, a_ws, a_bs, b_gw2, b_gb, b_on_g, c_qn_g, c_kn_g, cmp_pool, cmp_w1, cmp_b1, cmp_w2, cmp_b2, w_br_a, w_br_b, w_br_c, w_out, p_wq, p_qn_g, p_subkeys, p_u, p_v):
    tm_p, tm_s = 512, T_S
    w1_all, w2_all = _permute_w_in(w_in)
    wa_all = w_br_a.astype(BF16)
    wb_all = w_br_b.astype(BF16)
    wc_all = w_br_c.astype(BF16)
    wo_all = w_out.astype(BF16)
    wq_all = p_wq.astype(BF16)
    u_all = p_u.astype(BF16)
    v_all = p_v.astype(BF16)
    row3 = lambda a: a.reshape(DEPTH, 1, a.shape[-1])
    g1_all, g2_all = row3(norm1_g), row3(norm2_g)
    lng_all, lnb_all = row3(a_ln_g), row3(a_ln_b)
    bst_all = jnp.swapaxes(a_bs, 1, 2)
    gw2p_all = jnp.pad(b_gw2, ((0, 0), (0, Z2_W - B_GATE_RANK), (0, 0))).astype(BF16)
    gb_all, on_all = row3(b_gb), row3(b_on_g)
    gq_all, pqg_all = row3(c_qn_g), row3(p_qn_g)
    cb1_all = cmp_b1.reshape(DEPTH, 2, 1, C_HD)
    cb2_all = cmp_b2.reshape(DEPTH, 2, 1, C_HD)
    c40 = jnp.concatenate([jnp.repeat(c_prompt, N_SEQ_ROWS, axis=0), c_sample], axis=0)
    mod = adaln_table(c40, ada_w, ada_b)
    rope_p = _rope_tables(jnp.arange(SEQ, dtype=jnp.int32))
    rope_s = _rope_tables(jnp.full((T_S,), PAST_LEN, jnp.int32))
    m5 = jnp.asarray(_m5_table(), BF16)
    xp = x_prompt.reshape(T_P, D_MODEL)
    xs = jnp.pad(x_sample.reshape(DEC_BATCH, D_MODEL), ((0, T_S - DEC_BATCH), (0, 0)))
    cache4 = cache_nsa_kv.reshape(DEPTH, -1, PAGE_SIZE, 4 * C_KV_HEADS * C_HD)
    win4 = state_win_kv.reshape(DEPTH, DEC_BATCH, WINDOW, 2 * C_KV_HEADS * C_HD)
    kv_of_col = jnp.repeat(jnp.arange(2), C_KV_HEADS * C_HD)
    wf_all = jnp.swapaxes(cmp_pool[:, :, :CMP_STRIDE], 1, 2)[:, :, kv_of_col]
    ws_all = jnp.swapaxes(cmp_pool[:, :, CMP_STRIDE:], 1, 2)[:, :, kv_of_col]
    w00_all = jnp.repeat(a_ws[:, :, 0, 0], A_GW, axis=1)[:, None, :]
    b0_all = jnp.repeat(a_bs[:, :, 0], A_GW, axis=1)[:, None, :]
    m5s = jnp.asarray(_m5s_table(), BF16)
    ckv_s = sample_cmp_mlp(sample_pool(cache4, page_table.reshape(-1), wf_all, ws_all), cmp_w1, cb1_all, cmp_w2, cb2_all)

    outs = {k: [] for k in ('kv_p', 'kv_s', 'win_p', 'win_s', 'gla_p', 'gla_s', 'cv_s')}
    yp = ys = None
    for l in range(DEPTH):
        streams = []
        for is_s, x, y, tm in ((False, xp, yp, tm_p), (True, xs, ys, tm_s)):
            if l == 0:
                h = norm_mod(x, g1_all, mod, l, 0, 1, tm, is_s)
            else:
                x, h = resid_norm_mod(x, y, g1_all, mod, l - 1, l, 5, 0, 1, tm, is_s)
            z1 = matmul(h, w1_all, l, tm, 1024, BF16, "w_in")
            z2 = matmul(h, w2_all, l, tm, Z2_W, F32, "w_in_gates")
            rope = rope_s if is_s else rope_p
            qn, kva, kvb = qk_prep(z1, gq_all, c_kn_g, l, rope[0], rope[1], rope[2], tm, is_s)
            streams.append((x, z1, z2, qn, kva, kvb))
        x, z1, z2, qn, kva, kvb = streams[0]
        o_a = chunk_mlp_prompt(z1, lng_all, lnb_all, a_ws, bst_all, l)
        o_b, s_p = gla_prompt(z1, z2, gw2p_all, gb_all, on_all, l)
        ckv = compress_prompt(kva, cmp_pool, cmp_w1, cb1_all, cmp_w2, cb2_all, l)
        o_cmp, selt = cmp_select_prompt(qn, ckv, z2, m5)
        o_slc = flash_prompt(qn, kva, selt, z2, 1)
        o_win = flash_prompt(qn, kvb, selt, z2, 2)
        mixed_p = (o_a, o_b, o_cmp, o_slc, o_win)
        outs['kv_p'].append(kva.reshape(BATCH, SEQ, 4, C_KV_HEADS, C_HD))
        outs['win_p'].append(kvb.reshape(BATCH, SEQ, 2, C_KV_HEADS, C_HD)[:, SEQ - WINDOW:])
        outs['gla_p'].append(s_p)
        xs_, z1s, z2s, qns, kvas, kvbs = streams[1]
        oa_s, ob_s, av_s, s_s = sample_ab(z1s, z2s, lng_all, lnb_all, w00_all, b0_all, gw2p_all, gb_all, on_all, state_gla, l)
        ocmp_s, idx, ok = sample_cmp(qns, ckv_s, z2s, m5s, l)
        nsel = C_KV_HEADS * SLC_TOPK
        idx_f = idx[:, 0, :nsel].reshape(-1)
        ok_f = ok[:, 0, :nsel].reshape(-1)
        jc = jnp.minimum(idx_f, N_PAST_BLK - 1)
        bpp = PAGE_SIZE // SLC_BLK
        phys = page_table[jnp.repeat(jnp.arange(DEC_BATCH), nsel), jc // bpp]
        oslc_s, owin_s = sample_attend(cache4, win4, phys, jc % bpp, idx_f, ok_f, qns, kvas, kvbs, z2s, l)
        pad_s = lambda a: jnp.pad(a, ((0, T_S - a.shape[0]), (0, 0))).astype(BF16)
        heads = lambda o: o[:, :, :C_REP].reshape(DEC_BATCH, C_WIDTH)
        mixed_s = (pad_s(oa_s), pad_s(ob_s), pad_s(ocmp_s.reshape(DEC_BATCH, C_WIDTH)), pad_s(heads(oslc_s)), pad_s(heads(owin_s)))
        outs['kv_s'].append(kvas[:DEC_BATCH].reshape(DEC_BATCH, 1, 4, C_KV_HEADS, C_HD))
        outs['win_s'].append(kvbs[:DEC_BATCH].reshape(DEC_BATCH, 1, 2, C_KV_HEADS, C_HD))
        outs['gla_s'].append(s_s)
        outs['cv_s'].append(av_s[:DEC_BATCH, None, :])
        new = []
        for is_s, st, mixed, tm in ((False, streams[0], mixed_p, tm_p), (True, streams[1], mixed_s, tm_s)):
            x, z1 = st[0], st[1]
            mg = merge(*mixed, z1, wa_all, wb_all, wc_all, l, tm)
            att = matmul(mg, wo_all, l, tm, 1024, F32, "w_out")
            x1, h2 = resid_norm_mod(x, att, g2_all, mod, l, l, 2, 3, 4, tm, is_s)
            pq = peer_query(h2, wq_all, pqg_all, l, tm)
            r2, cnt, e1, e2 = peer_topk(pq, p_subkeys, l, tm)
            y = peer_main(h2, u_all, v_all, r2, cnt, e1, e2, l, tm)
            new.append((x1, y))
        (xp, yp), (xs, ys) = new
    xp = resid(xp, yp, mod, DEPTH - 1, 5, tm_p, False)
    xs = resid(xs, ys, mod, DEPTH - 1, 5, tm_s, True)
    return (
        xp.reshape(BATCH, SEQ, D_MODEL),
        xs[:DEC_BATCH].reshape(DEC_BATCH, 1, D_MODEL),
        jnp.stack(outs['kv_p']),
        jnp.stack(outs['kv_s']),
        jnp.stack(outs['win_p']),
        jnp.stack(outs['win_s']),
        jnp.stack(outs['gla_p']),
        jnp.stack(outs['gla_s']),
        jnp.stack(outs['cv_s']),
    )
```

```python
import functools

import numpy as np
import jax
import jax.numpy as jnp
from jax import lax
from jax.experimental import pallas as pl
from jax.experimental.pallas import tpu as pltpu

F32 = jnp.float32
BF16 = jnp.bfloat16

D_MODEL = 2048
BATCH = 4
SEQ = 2048
DEPTH = 4
DEC_BATCH = 8
PAST_LEN = 16384
PAGE_SIZE = 128
EPS = 1e-6
NEG = -1.0e30
A_WIDTH = 512
A_GROUPS = 4
A_GW = 128
A_CHUNK = 128
B_HEADS = 4
B_WIDTH = 512
B_DV = 128
B_DK = 64
B_GATE_RANK = 16
B_GATE_TAU = 16.0
B_CHUNK = 64
C_HEADS = 8
C_KV_HEADS = 2
C_HD = 128
C_WIDTH = 1024
C_REP = 4
C_ROT = 32
ROPE_THETA = 500000.0
CMP_LEN = 32
CMP_STRIDE = 16
SLC_BLK = 64
SLC_TOPK = 16
FORCE_BONUS = 1.0e4
WINDOW = 512
P_HEADS = 8
P_NKEYS = 128
P_NEXP = P_NKEYS * P_NKEYS
P_HALF = 128
P_TOPK = 16

T_P = BATCH * SEQ
T_S = 128
N_SEQ_ROWS = 8
N_CMP = SEQ // CMP_STRIDE

Z_CQ = 0
Z_AU = 1024
Z_AV = 1536
Z_BV = 2048
Z_BR = 2560
Z_CKV = 3072
Z_BQ = 4608
Z_BK = 4864
Z_MG = 5120
Z1_W = 11264
Z2_W = 128
Z2_CG = 16
KVA_W = 1024
KVB_W = 512

VMEM_LIMIT = 56 * 1024 * 1024


def _cp(sem, vmem=VMEM_LIMIT):
    return pltpu.CompilerParams(dimension_semantics=sem, vmem_limit_bytes=vmem)


def _lspec(l, *dims):
    nd = len(dims)
    return pl.BlockSpec((None,) + tuple(dims), lambda *_: (l,) + (0,) * nd)


def _gelu(x):
    return 0.5 * x * (1.0 + jnp.tanh(0.7978845608028654 * (x + 0.044715 * x * x * x)))


def _sigmoid(x):
    return 1.0 / (1.0 + jnp.exp(-x))


def _log_sigmoid(x):
    return jnp.minimum(x, 0.0) - jnp.log(1.0 + jnp.exp(-jnp.abs(x)))


def _dot(a, b):
    return jnp.dot(a, b, preferred_element_type=F32)


def _dot_nt(a, b):
    return lax.dot_general(a, b, (((1,), (1,)), ((), ())), preferred_element_type=F32)


def _split3(v):
    hi = v.astype(BF16)
    r1 = v - hi.astype(F32)
    mid = r1.astype(BF16)
    r2 = r1 - mid.astype(F32)
    lo = r2.astype(BF16)
    return hi, mid, lo


def _dot_exact_lhs(sel_bf16, v):
    hi, mid, lo = _split3(v)
    return (_dot(sel_bf16, hi) + _dot(sel_bf16, mid)) + _dot(sel_bf16, lo)


def _dot_exact_rhs(v, sel_bf16):
    hi, mid, lo = _split3(v)
    return (_dot(hi, sel_bf16) + _dot(mid, sel_bf16)) + _dot(lo, sel_bf16)


def _rms(x, g):
    return x * lax.rsqrt(jnp.mean(x * x, axis=-1, keepdims=True) + EPS) * g


def _mod_rows(y, scale8, shift8=None):
    tm, n = y.shape
    y3 = y.reshape(tm // N_SEQ_ROWS, N_SEQ_ROWS, n) * scale8[None]
    if shift8 is not None:
        y3 = y3 + shift8[None]
    return y3.reshape(tm, n)


def _gate_col(gates, col):
    lane = lax.broadcasted_iota(jnp.int32, gates.shape, 1)
    return jnp.sum(jnp.where(lane == col, gates, 0.0), axis=1, keepdims=True)


def _adaln_kernel(c_ref, w_ref, b_ref, o_ref):
    c = c_ref[...]
    s = (c * _sigmoid(c)).astype(BF16)
    o_ref[...] = _dot(s, w_ref[...].astype(BF16)) + b_ref[...]


def adaln_table(c40, ada_w, ada_b):
    tn = 1024
    n = 6 * D_MODEL
    return pl.pallas_call(
        _adaln_kernel,
        grid=(DEPTH, n // tn),
        in_specs=[
            pl.BlockSpec((5 * N_SEQ_ROWS, D_MODEL), lambda l, j: (0, 0)),
            pl.BlockSpec((None, D_MODEL, tn), lambda l, j: (l, 0, j)),
            pl.BlockSpec((None, 1, tn), lambda l, j: (l, 0, j)),
        ],
        out_specs=pl.BlockSpec((None, 5 * N_SEQ_ROWS, tn), lambda l, j: (l, 0, j)),
        out_shape=jax.ShapeDtypeStruct((DEPTH, 5 * N_SEQ_ROWS, n), F32),
        compiler_params=_cp(("arbitrary", "arbitrary")),
        name="adaln_table",
    )(c40, ada_w, ada_b.reshape(DEPTH, 1, n))


def _seq_of(tm, is_sample):
    if is_sample:
        return lambda i: 4
    per = SEQ // tm
    return lambda i: i // per


def _mod_spec(l, sq, k):
    return pl.BlockSpec((None, N_SEQ_ROWS, D_MODEL), lambda i: (l, sq(i), k))


def _norm_mod_kernel(x_ref, g_ref, sh_ref, sc_ref, h_ref):
    y = _rms(x_ref[...], g_ref[...])
    h_ref[...] = _mod_rows(y, 1.0 + sc_ref[...], sh_ref[...]).astype(h_ref.dtype)


def norm_mod(x, g, mod, l, k_shift, k_scale, tm, is_sample):
    t = x.shape[0]
    sq = _seq_of(tm, is_sample)
    return pl.pallas_call(
        _norm_mod_kernel,
        grid=(t // tm,),
        in_specs=[
            pl.BlockSpec((tm, D_MODEL), lambda i: (i, 0)),
            _lspec(l, 1, D_MODEL),
            _mod_spec(l, sq, k_shift),
            _mod_spec(l, sq, k_scale),
        ],
        out_specs=pl.BlockSpec((tm, D_MODEL), lambda i: (i, 0)),
        out_shape=jax.ShapeDtypeStruct((t, D_MODEL), BF16),
        compiler_params=_cp(("arbitrary",)),
        name="norm_mod",
    )(x, g, mod, mod)


def _resid_norm_mod_kernel(x_ref, y_ref, gate_ref, g_ref, sh_ref, sc_ref, xo_ref, h_ref):
    xn = x_ref[...] + _mod_rows(y_ref[...], gate_ref[...])
    xo_ref[...] = xn
    h_ref[...] = _mod_rows(_rms(xn, g_ref[...]), 1.0 + sc_ref[...], sh_ref[...]).astype(h_ref.dtype)


def resid_norm_mod(x, y, g, mod, l_gate, l, k_gate, k_shift, k_scale, tm, is_sample):
    t = x.shape[0]
    sq = _seq_of(tm, is_sample)
    row = pl.BlockSpec((tm, D_MODEL), lambda i: (i, 0))
    return pl.pallas_call(
        _resid_norm_mod_kernel,
        grid=(t // tm,),
        in_specs=[
            row,
            row,
            _mod_spec(l_gate, sq, k_gate),
            _lspec(l, 1, D_MODEL),
            _mod_spec(l, sq, k_shift),
            _mod_spec(l, sq, k_scale),
        ],
        out_specs=[row, row],
        out_shape=[jax.ShapeDtypeStruct((t, D_MODEL), F32), jax.ShapeDtypeStruct((t, D_MODEL), BF16)],
        compiler_params=_cp(("arbitrary",)),
        name="resid_norm_mod",
    )(x, y, mod, g, mod, mod)


def _resid_kernel(x_ref, y_ref, gate_ref, xo_ref):
    xo_ref[...] = x_ref[...] + _mod_rows(y_ref[...], gate_ref[...])


def resid(x, y, mod, l, k_gate, tm, is_sample):
    t = x.shape[0]
    sq = _seq_of(tm, is_sample)
    row = pl.BlockSpec((tm, D_MODEL), lambda i: (i, 0))
    return pl.pallas_call(
        _resid_kernel,
        grid=(t // tm,),
        in_specs=[row, row, _mod_spec(l, sq, k_gate)],
        out_specs=row,
        out_shape=jax.ShapeDtypeStruct((t, D_MODEL), F32),
        compiler_params=_cp(("arbitrary",)),
        name="resid",
    )(x, y, mod)


def _mm_kernel(x_ref, w_ref, o_ref):
    o_ref[...] = _dot(x_ref[...], w_ref[...]).astype(o_ref.dtype)


def matmul(x, w, l, tm, tn, out_dtype, name):
    m, k = x.shape
    n = w.shape[2]
    return pl.pallas_call(
        _mm_kernel,
        grid=(n // tn, m // tm),
        in_specs=[
            pl.BlockSpec((tm, k), lambda j, i: (i, 0)),
            pl.BlockSpec((None, k, tn), lambda j, i: (l, 0, j)),
        ],
        out_specs=pl.BlockSpec((tm, tn), lambda j, i: (i, j)),
        out_shape=jax.ShapeDtypeStruct((m, n), out_dtype),
        compiler_params=_cp(("arbitrary", "arbitrary")),
        name=name,
    )(x, w)


def _wq_kernel(x_ref, w_ref, g_ref, o_ref):
    acc = _dot(x_ref[...], w_ref[...])
    g = g_ref[...]
    for c in range(acc.shape[1] // P_HALF):
        a = acc[:, c * P_HALF:(c + 1) * P_HALF]
        o_ref[:, c * P_HALF:(c + 1) * P_HALF] = _rms(a, g).astype(o_ref.dtype)


def peer_query(h2, wq, qn_g, l, tm):
    m = h2.shape[0]
    tn = 512
    return pl.pallas_call(
        _wq_kernel,
        grid=(D_MODEL // tn, m // tm),
        in_specs=[
            pl.BlockSpec((tm, D_MODEL), lambda j, i: (i, 0)),
            pl.BlockSpec((None, D_MODEL, tn), lambda j, i: (l, 0, j)),
            _lspec(l, 1, P_HALF),
        ],
        out_specs=pl.BlockSpec((tm, tn), lambda j, i: (i, j)),
        out_shape=jax.ShapeDtypeStruct((m, D_MODEL), BF16),
        compiler_params=_cp(("arbitrary", "arbitrary")),
        name="peer_query",
    )(h2, wq, qn_g)


def _qk_prep_kernel(zq_ref, zkv_ref, gq_ref, gk_ref, c_ref, s1_ref, s2_ref, qn_ref, kva_ref, kvb_ref):
    cos = c_ref[...]
    s1 = s1_ref[...]
    s2 = s2_ref[...]

    def norm_rope(x, g):
        xn = _rms(x, g)
        return xn * cos + pltpu.roll(xn, C_HD - C_ROT // 2, 1) * s1 + pltpu.roll(xn, C_ROT // 2, 1) * s2

    gq = gq_ref[...]
    for h in range(C_HEADS):
        x = zq_ref[:, h * C_HD:(h + 1) * C_HD].astype(F32)
        qn_ref[:, h * C_HD:(h + 1) * C_HD] = norm_rope(x, gq).astype(qn_ref.dtype)
    for s in range(6):
        for g in range(C_KV_HEADS):
            c0 = (s * C_KV_HEADS + g) * C_HD
            x = zkv_ref[:, c0:c0 + C_HD].astype(F32)
            if s % 2 == 0:
                x = norm_rope(x, gk_ref[s // 2:s // 2 + 1, :])
            if c0 < KVA_W:
                kva_ref[:, c0:c0 + C_HD] = x
            else:
                kvb_ref[:, c0 - KVA_W:c0 - KVA_W + C_HD] = x


def qk_prep(z1, gq, gk, l, rope_c, rope_s1, rope_s2, tm, is_sample):
    t = z1.shape[0]
    per = 1 if is_sample else SEQ // tm
    tab = pl.BlockSpec((tm, C_HD), lambda i: (i % per, 0))
    return pl.pallas_call(
        _qk_prep_kernel,
        grid=(t // tm,),
        in_specs=[
            pl.BlockSpec((tm, C_WIDTH), lambda i: (i, Z_CQ // C_WIDTH)),
            pl.BlockSpec((tm, 1536), lambda i: (i, Z_CKV // 1536)),
            _lspec(l, 1, C_HD),
            _lspec(l, 3, C_HD),
            tab, tab, tab,
        ],
        out_specs=[
            pl.BlockSpec((tm, C_WIDTH), lambda i: (i, 0)),
            pl.BlockSpec((tm, KVA_W), lambda i: (i, 0)),
            pl.BlockSpec((tm, KVB_W), lambda i: (i, 0)),
        ],
        out_shape=[
            jax.ShapeDtypeStruct((t, C_WIDTH), BF16),
            jax.ShapeDtypeStruct((t, KVA_W), F32),
            jax.ShapeDtypeStruct((t, KVB_W), F32),
        ],
        compiler_params=_cp(("arbitrary",)),
        name="qk_prep",
    )(z1, z1, gq, gk, rope_c, rope_s1, rope_s2)


def _chunk_mlp_kernel(au_ref, av_ref, lng_ref, lnb_ref, ws_ref, bst_ref, o_ref):
    u = _gelu(au_ref[...].astype(F32))
    v = _gelu(av_ref[...].astype(F32))
    mu = jnp.mean(v, axis=-1, keepdims=True)
    vc = v - mu
    var = jnp.mean(vc * vc, axis=-1, keepdims=True)
    v = vc * lax.rsqrt(var + EPS) * lng_ref[...] + lnb_ref[...]
    r = lax.broadcasted_iota(jnp.int32, (A_CHUNK, A_CHUNK), 0)
    c = lax.broadcasted_iota(jnp.int32, (A_CHUNK, A_CHUNK), 1)
    tril = r >= c
    tm = u.shape[0]
    for g in range(A_GROUPS):
        w = jnp.where(tril, ws_ref[g], 0.0).astype(BF16)
        bias = bst_ref[:, g:g + 1]
        for ch in range(tm // A_CHUNK):
            rows = slice(ch * A_CHUNK, (ch + 1) * A_CHUNK)
            cols = slice(g * A_GW, (g + 1) * A_GW)
            mixed = _dot(w, v[rows, cols].astype(BF16)) + bias
            o_ref[rows, cols] = (u[rows, cols] * mixed).astype(o_ref.dtype)


def chunk_mlp_prompt(z1, ln_g, ln_b, ws, bs_t, l):
    tm = 512
    return pl.pallas_call(
        _chunk_mlp_kernel,
        grid=(T_P // tm,),
        in_specs=[
            pl.BlockSpec((tm, A_WIDTH), lambda i: (i, Z_AU // A_WIDTH)),
            pl.BlockSpec((tm, A_WIDTH), lambda i: (i, Z_AV // A_WIDTH)),
            _lspec(l, 1, A_WIDTH),
            _lspec(l, 1, A_WIDTH),
            _lspec(l, A_GROUPS, A_CHUNK, A_CHUNK),
            _lspec(l, A_CHUNK, A_GROUPS),
        ],
        out_specs=pl.BlockSpec((tm, A_WIDTH), lambda i: (i, 0)),
        out_shape=jax.ShapeDtypeStruct((T_P, A_WIDTH), BF16),
        compiler_params=_cp(("arbitrary",)),
        name="chunk_mlp",
    )(z1, z1, ln_g, ln_b, ws, bs_t)


def _col_from_row(row):
    n = row.shape[1]
    r = lax.broadcasted_iota(jnp.int32, (n, n), 0)
    c = lax.broadcasted_iota(jnp.int32, (n, n), 1)
    return jnp.sum(jnp.where(r == c, jnp.broadcast_to(row, (n, n)), 0.0), axis=1, keepdims=True)


def _gla_kernel(q_ref, k_ref, v_ref, r_ref, bg_ref, gw_ref, gb_ref, on_ref, o_ref, s_out_ref, s_sc):
    n = pl.program_id(1)

    @pl.when(n == 0)
    def _():
        s_sc[...] = jnp.zeros_like(s_sc)

    c = B_CHUNK
    ri = lax.broadcasted_iota(jnp.int32, (c, c), 0)
    ci = lax.broadcasted_iota(jnp.int32, (c, c), 1)
    tril = ri >= ci
    eye = jnp.where(ri == ci, 1.0, 0.0).astype(BF16)
    la = _log_sigmoid(_dot(bg_ref[...].astype(BF16), gw_ref[...]) + gb_ref[...]) * (1.0 / B_GATE_TAU)
    bc = _dot_exact_lhs(jnp.where(tril, 1.0, 0.0).astype(BF16), la)
    q = q_ref[...].astype(F32) * (B_DK ** -0.5)
    k = k_ref[...].astype(F32)
    v = v_ref[...].astype(BF16)
    gate = r_ref[...].astype(F32)
    gate = gate * _sigmoid(gate)
    on = on_ref[...]
    for h in range(B_HEADS):
        dk = slice(h * B_DK, (h + 1) * B_DK)
        dv = slice(h * B_DV, (h + 1) * B_DV)
        b = bc[:, dk]
        qd = (q[:, dk] * jnp.exp(b)).astype(BF16)
        kd = (k[:, dk] * jnp.exp(-b)).astype(BF16)
        att = jnp.where(tril, _dot_nt(qd, kd), 0.0)
        s_old = s_sc[h]
        o = _dot(att.astype(BF16), v[:, dv]) + _dot(qd, s_old.astype(BF16))
        blast = b[c - 1:c, :]
        kdec = (k[:, dk] * jnp.exp(blast - b)).astype(BF16)
        kdec_t = _dot_nt(eye, kdec).astype(BF16)
        s_new = s_old * _col_from_row(jnp.exp(blast)) + _dot(kdec_t, v[:, dv])
        s_sc[h] = s_new
        o_ref[:, dv] = (_rms(o, on) * gate[:, dv]).astype(o_ref.dtype)

    @pl.when(n == pl.num_programs(1) - 1)
    def _():
        s_out_ref[...] = s_sc[...]


def gla_prompt(z1, z2, gw2p, gb, on_g, l):
    c = B_CHUNK
    nch = SEQ // c
    row = lambda b, n: b * nch + n
    return pl.pallas_call(
        _gla_kernel,
        grid=(BATCH, nch),
        in_specs=[
            pl.BlockSpec((c, 256), lambda b, n: (row(b, n), Z_BQ // 256)),
            pl.BlockSpec((c, 256), lambda b, n: (row(b, n), Z_BK // 256)),
            pl.BlockSpec((c, B_WIDTH), lambda b, n: (row(b, n), Z_BV // B_WIDTH)),
            pl.BlockSpec((c, B_WIDTH), lambda b, n: (row(b, n), Z_BR // B_WIDTH)),
            pl.BlockSpec((c, Z2_W), lambda b, n: (row(b, n), 0)),
            _lspec(l, Z2_W, 256),
            _lspec(l, 1, 256),
            _lspec(l, 1, B_DV),
        ],
        out_specs=[
            pl.BlockSpec((c, B_WIDTH), lambda b, n: (row(b, n), 0)),
            pl.BlockSpec((None, B_HEADS, B_DK, B_DV), lambda b, n: (b, 0, 0, 0)),
        ],
        out_shape=[
            jax.ShapeDtypeStruct((T_P, B_WIDTH), BF16),
            jax.ShapeDtypeStruct((BATCH, B_HEADS, B_DK, B_DV), F32),
        ],
        scratch_shapes=[pltpu.VMEM((B_HEADS, B_DK, B_DV), F32)],
        compiler_params=_cp(("arbitrary", "arbitrary")),
        name="gla",
    )(z1, z1, z1, z1, z2, gw2p, gb, on_g)


def _compress_kernel(pool_ref, rows_ref, w1_ref, b1_ref, w2_ref, b2_ref, o_ref, *, l):
    kv = pl.program_id(2)
    r = lax.broadcasted_iota(jnp.int32, (N_CMP, SEQ), 0)
    c = lax.broadcasted_iota(jnp.int32, (N_CMP, SEQ), 1)
    d = c - CMP_STRIDE * r
    p = jnp.zeros((N_CMP, SEQ), F32)
    for i in range(CMP_LEN):
        p = jnp.where(d == i, pool_ref[l, kv, i], p)
    rows = rows_ref[...]
    ph, pm, plo = _split3(p)
    rh, rm, rl = _split3(rows)
    pooled = (_dot(ph, rh) + (_dot(ph, rm) + _dot(pm, rh))) + ((_dot(pm, rm) + _dot(ph, rl)) + _dot(plo, rh))
    hid = _gelu(_dot(pooled.astype(BF16), w1_ref[...].astype(BF16)) + b1_ref[...])
    o_ref[...] = _dot(hid.astype(BF16), w2_ref[...].astype(BF16)) + b2_ref[...]


def compress_prompt(kva, pool, w1, b1, w2, b2, l):
    wspec = pl.BlockSpec((None, None, C_HD, C_HD), lambda b, g, kv: (l, kv, 0, 0))
    bspec = pl.BlockSpec((None, None, 1, C_HD), lambda b, g, kv: (l, kv, 0, 0))
    return pl.pallas_call(
        functools.partial(_compress_kernel, l=l),
        grid=(BATCH, C_KV_HEADS, 2),
        in_specs=[
            pl.BlockSpec(memory_space=pltpu.SMEM),
            pl.BlockSpec((SEQ, C_HD), lambda b, g, kv: (b, kv * C_KV_HEADS + g)),
            wspec, bspec, wspec, bspec,
        ],
        out_specs=pl.BlockSpec((None, None, None, N_CMP, C_HD), lambda b, g, kv: (b, g, kv, 0, 0)),
        out_shape=jax.ShapeDtypeStruct((BATCH, C_KV_HEADS, 2, N_CMP, C_HD), F32),
        compiler_params=_cp(("arbitrary", "arbitrary", "arbitrary")),
        name="compress",
    )(pool, kva, w1, b1, w2, b2)


def _softmax_rows(s, valid):
    s = jnp.where(valid, s, NEG)
    m = jnp.max(s, axis=-1, keepdims=True)
    e = jnp.where(valid, jnp.exp(s - m), 0.0)
    return e / jnp.maximum(jnp.sum(e, axis=-1, keepdims=True), 1e-30)


def _cmp_select_kernel(q_ref, ckv_ref, gate_ref, m5_ref, o_ref, selt_ref):
    g = pl.program_id(1)
    qi = pl.program_id(2)
    tq = q_ref.shape[0]
    ck = ckv_ref[0].astype(BF16)
    cv = ckv_ref[1].astype(BF16)
    pos = qi * tq + lax.broadcasted_iota(jnp.int32, (tq, N_CMP), 0)
    j = lax.broadcasted_iota(jnp.int32, (tq, N_CMP), 1)
    valid = j * CMP_STRIDE + (CMP_LEN - 1) <= pos
    gates = gate_ref[...]
    imp = jnp.zeros((tq, N_CMP), F32)
    for r in range(C_REP):
        q = q_ref[:, r * C_HD:(r + 1) * C_HD]
        p = _softmax_rows(_dot_nt(q, ck) * (C_HD ** -0.5), valid)
        imp = imp + p
        o = _dot(p.astype(BF16), cv)
        gcol = _gate_col(gates, Z2_CG + g * C_REP + r)
        o_ref[:, r * C_HD:(r + 1) * C_HD] = (o * _sigmoid(gcol)).astype(o_ref.dtype)
    score = _dot_exact_rhs(imp, m5_ref[...])
    cur = pos // SLC_BLK
    ok_blk = j * SLC_BLK <= pos
    forced = (j == 0) | (j == cur) | (j == cur - 1)
    score = jnp.where(ok_blk, score + jnp.where(forced, FORCE_BONUS, 0.0), NEG)
    score = jnp.where(j < SEQ // SLC_BLK, score, -3.0e38)
    rank = jnp.zeros((tq, N_CMP), F32)
    for jj in range(SEQ // SLC_BLK):
        col = score[:, jj:jj + 1]
        ahead = (col > score) | ((col == score) & (j > jj))
        rank = rank + jnp.where(ahead, 1.0, 0.0)
    sel = (rank < SLC_TOPK) & (score > 0.5 * NEG)
    selt_ref[...] = jnp.where(sel, 1.0, 0.0).T.astype(selt_ref.dtype)


def cmp_select_prompt(qn, ckv, z2, m5):
    tq = 512
    nq = SEQ // tq
    return pl.pallas_call(
        _cmp_select_kernel,
        grid=(BATCH, C_KV_HEADS, nq),
        in_specs=[
            pl.BlockSpec((tq, C_REP * C_HD), lambda b, g, i: (b * nq + i, g)),
            pl.BlockSpec((None, None, 2, N_CMP, C_HD), lambda b, g, i: (b, g, 0, 0, 0)),
            pl.BlockSpec((tq, Z2_W), lambda b, g, i: (b * nq + i, 0)),
            pl.BlockSpec((N_CMP, N_CMP), lambda b, g, i: (0, 0)),
        ],
        out_specs=[
            pl.BlockSpec((tq, C_REP * C_HD), lambda b, g, i: (b * nq + i, g)),
            pl.BlockSpec((None, None, N_CMP, tq), lambda b, g, i: (b, g, 0, i)),
        ],
        out_shape=[
            jax.ShapeDtypeStruct((T_P, C_WIDTH), BF16),
            jax.ShapeDtypeStruct((BATCH, C_KV_HEADS, N_CMP, SEQ), BF16),
        ],
        compiler_params=_cp(("arbitrary", "arbitrary", "arbitrary")),
        name="cmp_select",
    )(qn, ckv, z2, m5)


def _flash_kernel(q_ref, k_ref, v_ref, selt_ref, gate_ref, o_ref, vt_sc, m_sc, l_sc, acc_sc, *, branch, tk):
    g = pl.program_id(1)
    qi = pl.program_id(2)
    tq = q_ref.shape[0]

    @pl.when(qi == 0)
    def _():
        for jt in range(SEQ // tk):
            vt_sc[jt] = v_ref[jt * tk:(jt + 1) * tk, :].T

    q = jnp.concatenate([q_ref[:, r * C_HD:(r + 1) * C_HD] for r in range(C_REP)], axis=0)
    m_sc[...] = jnp.full(m_sc.shape, NEG, F32)
    l_sc[...] = jnp.zeros(l_sc.shape, F32)
    acc_sc[...] = jnp.zeros(acc_sc.shape, F32)
    pos = qi * tq + lax.broadcasted_iota(jnp.int32, (tk, tq), 1)
    hi = ((qi + 1) * tq + tk - 1) // tk
    if branch == 1:
        lo = 0
        selt = selt_ref[...]
    else:
        lo = jnp.maximum(qi * tq - WINDOW, 0) // tk

    def step(kj, carry):
        k0 = pl.multiple_of(kj * tk, tk)
        k = k_ref[pl.ds(k0, tk), :].astype(BF16)
        vt = vt_sc[kj].astype(BF16)
        key = k0 + lax.broadcasted_iota(jnp.int32, (tk, tq), 0)
        if branch == 1:
            kb = (k0 + lax.broadcasted_iota(jnp.int32, (tk, N_CMP), 0)) // SLC_BLK
            expand = jnp.where(kb == lax.broadcasted_iota(jnp.int32, (tk, N_CMP), 1), 1.0, 0.0).astype(BF16)
            valid = (_dot(expand, selt) > 0.5) & (key <= pos)
        else:
            valid = (key <= pos) & (key > pos - WINDOW)
        valid4 = jnp.concatenate([valid] * C_REP, axis=1)
        s = jnp.where(valid4, _dot_nt(k, q) * (C_HD ** -0.5), NEG)
        m_old = m_sc[...]
        m_new = jnp.maximum(m_old, jnp.max(s, axis=0, keepdims=True))
        alpha = jnp.exp(m_old - m_new)
        p = jnp.where(valid4, jnp.exp(s - m_new), 0.0)
        l_sc[...] = alpha * l_sc[...] + jnp.sum(p, axis=0, keepdims=True)
        acc_sc[...] = alpha * acc_sc[...] + _dot(vt, p.astype(BF16))
        m_sc[...] = m_new
        return carry

    lax.fori_loop(lo, hi, step, 0)
    out = (acc_sc[...] / jnp.maximum(l_sc[...], 1e-30)).T
    gates = gate_ref[...]
    for r in range(C_REP):
        gcol = _gate_col(gates, Z2_CG + branch * C_HEADS + g * C_REP + r)
        o_ref[:, r * C_HD:(r + 1) * C_HD] = (out[r * tq:(r + 1) * tq] * _sigmoid(gcol)).astype(o_ref.dtype)


def flash_prompt(qn, kv, selt, z2, branch):
    tq = 256
    tk = 256
    nq = SEQ // tq
    kcol = 2 * C_KV_HEADS if branch == 1 else 0
    kern = functools.partial(_flash_kernel, branch=branch, tk=tk)
    return pl.pallas_call(
        kern,
        grid=(BATCH, C_KV_HEADS, nq),
        in_specs=[
            pl.BlockSpec((tq, C_REP * C_HD), lambda b, g, i: (b * nq + i, g)),
            pl.BlockSpec((SEQ, C_HD), lambda b, g, i: (b, kcol + g)),
            pl.BlockSpec((SEQ, C_HD), lambda b, g, i: (b, kcol + C_KV_HEADS + g)),
            pl.BlockSpec((None, None, N_CMP, tq), lambda b, g, i: (b, g, 0, i)),
            pl.BlockSpec((tq, Z2_W), lambda b, g, i: (b * nq + i, 0)),
        ],
        out_specs=pl.BlockSpec((tq, C_REP * C_HD), lambda b, g, i: (b * nq + i, g)),
        out_shape=jax.ShapeDtypeStruct((T_P, C_WIDTH), BF16),
        scratch_shapes=[
            pltpu.VMEM((SEQ // tk, C_HD, tk), F32),
            pltpu.VMEM((1, C_REP * tq), F32),
            pltpu.VMEM((1, C_REP * tq), F32),
            pltpu.VMEM((C_HD, C_REP * tq), F32),
        ],
        compiler_params=_cp(("arbitrary", "arbitrary", "arbitrary")),
        name="flash_slc" if branch == 1 else "flash_win",
    )(qn, kv, kv, selt, z2)


def _merge_kernel(oa_ref, ob_ref, oc0_ref, oc1_ref, oc2_ref, ga_ref, gb_ref, gc_ref, wa_ref, wb_ref, wc_ref, o_ref):
    oc = (oc0_ref[...].astype(F32) + oc1_ref[...].astype(F32) + oc2_ref[...].astype(F32)).astype(BF16)
    m = _sigmoid(ga_ref[...].astype(F32)) * _dot(oa_ref[...], wa_ref[...])
    m = m + _sigmoid(gb_ref[...].astype(F32)) * _dot(ob_ref[...], wb_ref[...])
    m = m + _sigmoid(gc_ref[...].astype(F32)) * _dot(oc, wc_ref[...])
    o_ref[...] = m.astype(o_ref.dtype)


def merge(o_a, o_b, oc0, oc1, oc2, z1, wa, wb, wc, l, tm):
    t = o_a.shape[0]
    tn = 512
    nj = D_MODEL // tn
    gate = lambda k: pl.BlockSpec((tm, tn), lambda i, j: (i, (Z_MG + k * D_MODEL) // tn + j))
    wspec = lambda kdim: pl.BlockSpec((None, kdim, tn), lambda i, j: (l, 0, j))
    return pl.pallas_call(
        _merge_kernel,
        grid=(t // tm, nj),
        in_specs=[
            pl.BlockSpec((tm, A_WIDTH), lambda i, j: (i, 0)),
            pl.BlockSpec((tm, B_WIDTH), lambda i, j: (i, 0)),
            pl.BlockSpec((tm, C_WIDTH), lambda i, j: (i, 0)),
            pl.BlockSpec((tm, C_WIDTH), lambda i, j: (i, 0)),
            pl.BlockSpec((tm, C_WIDTH), lambda i, j: (i, 0)),
            gate(0), gate(1), gate(2),
            wspec(A_WIDTH), wspec(B_WIDTH), wspec(C_WIDTH),
        ],
        out_specs=pl.BlockSpec((tm, tn), lambda i, j: (i, j)),
        out_shape=jax.ShapeDtypeStruct((t, D_MODEL), BF16),
        compiler_params=_cp(("arbitrary", "arbitrary")),
        name="merge",
    )(o_a, o_b, oc0, oc1, oc2, z1, z1, z1, wa, wb, wc)


N_CELLS = 50
N_CELL_ROWS = 56


def _cell_tables():
    sa = np.zeros((N_CELL_ROWS, P_TOPK), np.float32)
    sb = np.zeros((N_CELL_ROWS, P_TOPK), np.float32)
    r = 0
    for a in range(P_TOPK):
        for b in range(P_TOPK // (a + 1)):
            sa[r, a] = 1.0
            sb[r, b] = 1.0
            r += 1
    assert r == N_CELLS
    return sa, sb


def _top16_cols(s, row_iota, exact):
    rank = jnp.full(s.shape, float(P_TOPK), F32)
    work = s
    top = jnp.zeros((P_TOPK, s.shape[1]), F32)
    r_iota = lax.broadcasted_iota(jnp.int32, top.shape, 0)
    for r in range(P_TOPK):
        m = jnp.max(work, axis=0, keepdims=True)
        if exact:
            idx = jnp.min(jnp.where(work == m, row_iota, 1.0e9), axis=0, keepdims=True)
            hit = row_iota == idx
        else:
            hit = work == m
        rank = jnp.where(hit, float(r), rank)
        work = jnp.where(hit, -jnp.inf, work)
        top = jnp.where(r_iota == r, m, top)
    n_out = jnp.sum(jnp.where(rank < float(P_TOPK), 1.0, 0.0), axis=0, keepdims=True)
    return rank, top, n_out


def _peer_topk_kernel(qn_ref, sk_ref, sa_ref, sb_ref, sat_ref, r2_ref, cnt_ref, e1_ref, e2_ref):
    tb = qn_ref.shape[0]
    lanes = 128
    sk1 = sk_ref[0].astype(BF16)
    sk2 = sk_ref[1].astype(BF16)
    s1_all = _dot_nt(sk1, qn_ref[:, 0:P_HALF])
    s2_all = _dot_nt(sk2, qn_ref[:, P_HALF:2 * P_HALF])
    n_iota = lax.broadcasted_iota(jnp.int32, (P_NKEYS, lanes), 0).astype(F32)
    c_iota = lax.broadcasted_iota(jnp.int32, (N_CELL_ROWS, lanes), 0).astype(F32)
    sa = sa_ref[...]
    sb = sb_ref[...]
    sat = sat_ref[...]

    def slab(t, exact):
        cols = slice(t * lanes, (t + 1) * lanes)
        s1 = s1_all[:, cols]
        s2 = s2_all[:, cols]
        rank1, top1, n1 = _top16_cols(s1, n_iota, exact)
        rank2, top2, n2 = _top16_cols(s2, n_iota, exact)
        cand = _dot_exact_lhs(sa, top1) + _dot_exact_lhs(sb, top2)
        rankc, topc, nc = _top16_cols(jnp.where(c_iota < N_CELLS, cand, -jnp.inf), c_iota, exact)
        picked = jnp.where(rankc < float(P_TOPK), 1.0, 0.0)
        zsum = jnp.sum(jnp.exp(topc - topc[0:1, :]), axis=0, keepdims=True)
        cnt = _dot(sat, picked.astype(BF16))
        cntd = jnp.zeros(rank1.shape, F32)
        for a in range(P_TOPK):
            cntd = jnp.where(rank1 == float(a), cnt[a:a + 1, :], cntd)
        e1 = jnp.where(rank1 < float(P_TOPK), jnp.exp(s1 - top1[0:1, :]), 0.0) / zsum
        e2 = jnp.exp(s2 - top2[0:1, :])
        r2_ref[:, cols] = rank2.astype(r2_ref.dtype)
        cnt_ref[:, cols] = cntd
        e1_ref[:, cols] = e1
        e2_ref[:, cols] = e2.astype(e2_ref.dtype)
        return jnp.abs(n1 - P_TOPK) + jnp.abs(n2 - P_TOPK) + jnp.abs(nc - P_TOPK)

    for t in range(tb // lanes):
        tied = jnp.max(slab(t, False)) > 0.0

        @pl.when(tied)
        def _():
            slab(t, True)


def peer_topk(qn, subkeys, l, tb):
    t = qn.shape[0]
    sa, sb = _cell_tables()
    out = lambda dt: jax.ShapeDtypeStruct((P_HEADS, P_NKEYS, t), dt)
    ospec = pl.BlockSpec((None, P_NKEYS, tb), lambda i, h: (h, 0, i))
    return pl.pallas_call(
        _peer_topk_kernel,
        grid=(t // tb, P_HEADS),
        in_specs=[
            pl.BlockSpec((tb, 2 * P_HALF), lambda i, h: (i, h)),
            pl.BlockSpec((None, None, 2, P_NKEYS, P_HALF), lambda i, h: (l, h, 0, 0, 0)),
            pl.BlockSpec((N_CELL_ROWS, P_TOPK), lambda i, h: (0, 0)),
            pl.BlockSpec((N_CELL_ROWS, P_TOPK), lambda i, h: (0, 0)),
            pl.BlockSpec((P_TOPK, N_CELL_ROWS), lambda i, h: (0, 0)),
        ],
        out_specs=[ospec, ospec, ospec, ospec],
        out_shape=[out(BF16), out(F32), out(F32), out(BF16)],
        compiler_params=_cp(("arbitrary", "arbitrary")),
        name="peer_topk",
    )(qn, subkeys, jnp.asarray(sa, BF16), jnp.asarray(sb, BF16), jnp.asarray(sa.T, BF16))


PEER_EC = 512
PEER_NGRP = P_NEXP // PEER_EC


def _peer_main_kernel(h_ref, ua_ref, ub_ref, va_ref, vb_ref, r2_ref, cnt_ref, e1_ref, e2_ref, y_ref, pa_sc, pb_sc):
    e = pl.program_id(1)
    last = pl.num_programs(1) - 1
    nchunk = PEER_EC // P_NKEYS

    @pl.when(e == 0)
    def _():
        y_ref[...] = jnp.zeros_like(y_ref)
        pb_sc[...] = jnp.zeros_like(pb_sc)

    def gated(scores, grp):
        hid = _gelu(scores)
        zero = jnp.zeros((), BF16)
        parts = []
        for cc in range(nchunk):
            c = grp * nchunk + cc
            w = None
            for h in range(P_HEADS):
                e1 = e1_ref[h, pl.ds(c, 1), :].astype(BF16)
                cn = cnt_ref[h, pl.ds(c, 1), :].astype(BF16)
                term = jnp.where(r2_ref[h] < cn, e2_ref[h], zero) * e1
                w = term if w is None else w + term
            parts.append(w.astype(F32) * hid[cc * P_NKEYS:(cc + 1) * P_NKEYS, :])
        return jnp.concatenate(parts, axis=0).T.astype(BF16)

    @pl.when(e < last)
    def _():
        sa = _dot_nt(ua_ref[...], h_ref[...])
        y_ref[...] += _dot(pb_sc[...], va_ref[...])
        sb = _dot_nt(ub_ref[...], h_ref[...])
        pa_sc[...] = gated(sa, 2 * e)
        y_ref[...] += _dot(pa_sc[...], vb_ref[...])
        pb_sc[...] = gated(sb, 2 * e + 1)

    @pl.when(e == last)
    def _():
        y_ref[...] += _dot(pb_sc[...], va_ref[...])


def peer_main(h2, u_bf, v_bf, r2, cnt, e1, e2, l, tb):
    t = h2.shape[0]
    nstep = PEER_NGRP // 2 + 1
    gspec = pl.BlockSpec((P_HEADS, P_NKEYS, tb), lambda i, e: (0, 0, i))
    grp = lambda f: pl.BlockSpec((None, PEER_EC, D_MODEL), lambda i, e: (l, jnp.clip(f(e), 0, PEER_NGRP - 1), 0))
    return pl.pallas_call(
        _peer_main_kernel,
        grid=(t // tb, nstep),
        in_specs=[
            pl.BlockSpec((tb, D_MODEL), lambda i, e: (i, 0)),
            grp(lambda e: 2 * e), grp(lambda e: 2 * e + 1),
            grp(lambda e: 2 * e - 1), grp(lambda e: 2 * e),
            gspec, gspec, gspec, gspec,
        ],
        out_specs=pl.BlockSpec((tb, D_MODEL), lambda i, e: (i, 0)),
        out_shape=jax.ShapeDtypeStruct((t, D_MODEL), F32),
        scratch_shapes=[pltpu.VMEM((tb, PEER_EC), BF16), pltpu.VMEM((tb, PEER_EC), BF16)],
        compiler_params=_cp(("arbitrary", "arbitrary")),
        name="peer_main",
    )(h2, u_bf, u_bf, v_bf, v_bf, r2, cnt, e1, e2)


N_PAGES = PAST_LEN // PAGE_SIZE
N_SEG = PAST_LEN // CMP_STRIDE
PAGES_PER_STEP = 8
SEG_PER_STEP = PAGES_PER_STEP * PAGE_SIZE // CMP_STRIDE
N_PAST_BLK = PAST_LEN // SLC_BLK
SLC_LANES = 384
S_ROWS = 16


def _row_of(x, b):
    r = lax.broadcasted_iota(jnp.int32, x.shape, 0)
    return jnp.sum(jnp.where(r == b, x, 0.0), axis=0, keepdims=True)


def _sample_pool_kernel(pt_ref, *refs):
    pages = refs[:PAGES_PER_STEP]
    wf_ref, ws_ref, o_ref, carry_sc = refs[PAGES_PER_STEP:]
    pc = pl.program_id(2)

    @pl.when(pc == 0)
    def _():
        carry_sc[...] = jnp.zeros_like(carry_sc)

    wf = wf_ref[...]
    ws = ws_ref[...]
    fs, ss = [], []
    for r in pages:
        xs = r[...].reshape(PAGE_SIZE // CMP_STRIDE, CMP_STRIDE, 2, C_KV_HEADS, C_HD)
        fs.append(jnp.sum(xs * wf[None], axis=1))
        ss.append(jnp.sum(xs * ws[None], axis=1))
    first = jnp.concatenate(fs, axis=0)
    second = jnp.concatenate(ss, axis=0)
    o_ref[...] = jnp.concatenate([carry_sc[...], first[:SEG_PER_STEP - 1]], axis=0) + second
    carry_sc[...] = first[SEG_PER_STEP - 1:]


def sample_pool(cache, pt_flat, wf_all, ws_all):
    def page(k):
        return pl.BlockSpec((None, None, PAGE_SIZE, 2, C_KV_HEADS, C_HD),
                            lambda l, b, pc, pt: (l, pt[b * N_PAGES + pc * PAGES_PER_STEP + k], 0, 0, 0, 0))

    wspec = pl.BlockSpec((None, CMP_STRIDE, 2, C_KV_HEADS, C_HD), lambda l, b, pc, pt: (l, 0, 0, 0, 0))
    grid_spec = pltpu.PrefetchScalarGridSpec(
        num_scalar_prefetch=1,
        grid=(DEPTH, DEC_BATCH, N_PAGES // PAGES_PER_STEP),
        in_specs=[page(k) for k in range(PAGES_PER_STEP)] + [wspec, wspec],
        out_specs=pl.BlockSpec((None, None, SEG_PER_STEP, 2, C_KV_HEADS, C_HD), lambda l, b, pc, pt: (l, b, pc, 0, 0, 0)),
        scratch_shapes=[pltpu.VMEM((1, 2, C_KV_HEADS, C_HD), F32)],
    )
    return pl.pallas_call(
        _sample_pool_kernel,
        grid_spec=grid_spec,
        out_shape=jax.ShapeDtypeStruct((DEPTH, DEC_BATCH, N_SEG, 2, C_KV_HEADS, C_HD), F32),
        compiler_params=_cp(("arbitrary", "arbitrary", "arbitrary")),
        name="sample_pool",
    )(pt_flat, *([cache] * PAGES_PER_STEP), wf_all, ws_all)


def _sample_cmp_mlp_kernel(x_ref, w1_ref, b1_ref, w2_ref, b2_ref, o_ref):
    for kv in range(2):
        w1 = w1_ref[kv].astype(BF16)
        w2 = w2_ref[kv].astype(BF16)
        for g in range(C_KV_HEADS):
            hid = _gelu(_dot(x_ref[:, kv, g, :].astype(BF16), w1) + b1_ref[kv])
            o_ref[kv * C_KV_HEADS + g] = _dot(hid.astype(BF16), w2) + b2_ref[kv]


def sample_cmp_mlp(pooled, w1, b1, w2, b2):
    wspec = pl.BlockSpec((None, 2, C_HD, C_HD), lambda l, b: (l, 0, 0, 0))
    bspec = pl.BlockSpec((None, 2, 1, C_HD), lambda l, b: (l, 0, 0, 0))
    return pl.pallas_call(
        _sample_cmp_mlp_kernel,
        grid=(DEPTH, DEC_BATCH),
        in_specs=[pl.BlockSpec((None, None, N_SEG, 2, C_KV_HEADS, C_HD), lambda l, b: (l, b, 0, 0, 0, 0)),
                  wspec, bspec, wspec, bspec],
        out_specs=pl.BlockSpec((None, None, 2 * C_KV_HEADS, N_SEG, C_HD), lambda l, b: (l, b, 0, 0, 0)),
        out_shape=jax.ShapeDtypeStruct((DEPTH, DEC_BATCH, 2 * C_KV_HEADS, N_SEG, C_HD), F32),
        compiler_params=_cp(("arbitrary", "arbitrary")),
        name="sample_cmp_mlp",
    )(pooled, w1, b1, w2, b2)


def _sample_cmp_kernel(q_ref, ckv_ref, gate_ref, m5_ref, o_ref, idx_ref, ok_ref):
    b = pl.program_id(0)
    qrow = _row_of(q_ref[...].astype(F32), b)
    grow = _row_of(gate_ref[...], b)
    m_i = lax.broadcasted_iota(jnp.int32, (8, N_SEG), 1)
    valid = (m_i >= 1) & ((m_i - 1) * CMP_STRIDE + (CMP_LEN - 1) <= PAST_LEN)
    lane = lax.broadcasted_iota(jnp.int32, (1, Z2_W), 1)
    j = lax.broadcasted_iota(jnp.int32, (8, SLC_LANES), 1)
    jf = j.astype(F32)
    cur = PAST_LEN // SLC_BLK
    forced = (j == 0) | (j == cur) | (j == cur - 1)
    idx_row = jnp.zeros((1, Z2_W), F32)
    ok_row = jnp.zeros((1, Z2_W), F32)
    for g in range(C_KV_HEADS):
        ck = ckv_ref[g].astype(BF16)
        cv = ckv_ref[C_KV_HEADS + g].astype(BF16)
        imp = jnp.zeros((8, N_SEG), F32)
        for r in range(C_REP):
            h = g * C_REP + r
            q8 = jnp.broadcast_to(qrow[:, h * C_HD:(h + 1) * C_HD], (8, C_HD)).astype(BF16)
            p = _softmax_rows(_dot_nt(q8, ck) * (C_HD ** -0.5), valid)
            imp = imp + p
            o = _dot(p.astype(BF16), cv)
            gate = jnp.sum(jnp.where(lane == Z2_CG + h, grow, 0.0), axis=1, keepdims=True)
            o_ref[:, h * C_HD:(h + 1) * C_HD] = o[0:1] * _sigmoid(gate)
        score = _dot_exact_rhs(imp, m5_ref[...])
        score = jnp.where(j * SLC_BLK <= PAST_LEN, score + jnp.where(forced, FORCE_BONUS, 0.0), -3.0e38)
        for r in range(SLC_TOPK):
            m = jnp.max(score, axis=1, keepdims=True)
            idx = jnp.min(jnp.where(score == m, jf, 1.0e9), axis=1, keepdims=True)
            slot = lane == g * SLC_TOPK + r
            idx_row = jnp.where(slot, idx[0:1], idx_row)
            ok_row = jnp.where(slot, jnp.where(m[0:1] > 0.5 * NEG, 1.0, 0.0), ok_row)
            score = jnp.where(jf == idx, -3.4e38, score)
    idx_ref[...] = idx_row.astype(jnp.int32)
    ok_ref[...] = ok_row.astype(jnp.int32)


def sample_cmp(qn, ckv_all, z2, m5s, l):
    small = lambda n, dt: jax.ShapeDtypeStruct((DEC_BATCH, 1, n), dt)
    ospec = lambda n: pl.BlockSpec((None, 1, n), lambda b: (b, 0, 0))
    return pl.pallas_call(
        _sample_cmp_kernel,
        grid=(DEC_BATCH,),
        in_specs=[
            pl.BlockSpec((S_ROWS, C_WIDTH), lambda b: (0, 0)),
            pl.BlockSpec((None, None, 2 * C_KV_HEADS, N_SEG, C_HD), lambda b: (l, b, 0, 0, 0)),
            pl.BlockSpec((S_ROWS, Z2_W), lambda b: (0, 0)),
            pl.BlockSpec((N_SEG, SLC_LANES), lambda b: (0, 0)),
        ],
        out_specs=[ospec(C_WIDTH), ospec(Z2_W), ospec(Z2_W)],
        out_shape=[small(C_WIDTH, F32), small(Z2_W, jnp.int32), small(Z2_W, jnp.int32)],
        compiler_params=_cp(("arbitrary",)),
        name="sample_cmp",
    )(qn, ckv_all, z2, m5s)


def _sample_attend_kernel(phys_ref, rb_ref, idx_ref, ok_ref, *refs):
    nblk = C_KV_HEADS * SLC_TOPK
    blks = refs[:nblk]
    q_ref, kva_ref, kvb_ref, win_ref, gate_ref, oslc_ref, owin_ref, kcat_sc, vcat_sc = refs[nblk:]
    b = pl.program_id(0)
    scale = C_HD ** -0.5
    qall = _row_of(q_ref[...].astype(F32), b)
    kva_row = _row_of(kva_ref[...], b)
    kvb_row = _row_of(kvb_ref[...], b)
    grow = _row_of(gate_ref[...], b)
    lane_g = lax.broadcasted_iota(jnp.int32, grow.shape, 1)
    r8 = lax.broadcasted_iota(jnp.int32, (8, C_HD), 0)
    r1 = lax.broadcasted_iota(jnp.int32, (8, 1), 0)
    r64 = lax.broadcasted_iota(jnp.int32, (SLC_BLK, C_HD), 0)
    lane = lax.broadcasted_iota(jnp.int32, (8, SLC_TOPK * SLC_BLK), 1)
    kp = PAST_LEN - WINDOW + lax.broadcasted_iota(jnp.int32, (8, WINDOW), 1)
    validw = (kp <= PAST_LEN) & (kp > PAST_LEN - WINDOW)
    seg = lambda row, i: row[:, i * C_HD:(i + 1) * C_HD]
    for g in range(C_KV_HEADS):
        base = (b * C_KV_HEADS + g) * SLC_TOPK
        q4 = jnp.zeros((8, C_HD), F32)
        for r in range(C_REP):
            q4 = jnp.where(r8 == r, jnp.broadcast_to(seg(qall, g * C_REP + r), (8, C_HD)), q4)
        q4 = q4.astype(BF16)
        newk = jnp.where(r64 == 0, jnp.broadcast_to(seg(kva_row, 2 * C_KV_HEADS + g), (SLC_BLK, C_HD)), 0.0)
        newv = jnp.where(r64 == 0, jnp.broadcast_to(seg(kva_row, 3 * C_KV_HEADS + g), (SLC_BLK, C_HD)), 0.0)
        rowpos = lane % SLC_BLK
        okv = jnp.zeros(lane.shape, jnp.int32)
        for k in range(SLC_TOPK):
            idx_k = idx_ref[base + k]
            is_new = jnp.full((SLC_BLK, C_HD), idx_k, jnp.int32) >= N_PAST_BLK
            blk = blks[g * SLC_TOPK + k]
            kcat_sc[k * SLC_BLK:(k + 1) * SLC_BLK, :] = jnp.where(is_new, newk, blk[:, 0, g, :])
            vcat_sc[k * SLC_BLK:(k + 1) * SLC_BLK, :] = jnp.where(is_new, newv, blk[:, 1, g, :])
            in_k = lane // SLC_BLK == k
            rowpos = jnp.where(in_k, rowpos + idx_k * SLC_BLK, rowpos)
            okv = jnp.where(in_k, ok_ref[base + k], okv)
        s = _dot_nt(q4, kcat_sc[...].astype(BF16)) * scale
        p = _softmax_rows(s, (okv > 0) & (rowpos <= PAST_LEN))
        o_slc = _dot(p.astype(BF16), vcat_sc[...].astype(BF16))
        sw = jnp.where(validw, _dot_nt(q4, win_ref[:, 0, g, :].astype(BF16)) * scale, NEG)
        knew = seg(kvb_row, g).astype(BF16).astype(F32)
        vnew = seg(kvb_row, C_KV_HEADS + g)
        sn = jnp.sum(q4.astype(F32) * knew, axis=1, keepdims=True) * scale
        m = jnp.maximum(jnp.max(sw, axis=1, keepdims=True), sn)
        ew = jnp.where(validw, jnp.exp(sw - m), 0.0)
        en = jnp.exp(sn - m)
        den = jnp.maximum(jnp.sum(ew, axis=1, keepdims=True) + en, 1e-30)
        o_win = (_dot(ew.astype(BF16), win_ref[:, 1, g, :].astype(BF16)) + en * vnew) / den
        for branch, o, o_ref in ((1, o_slc, oslc_ref), (2, o_win, owin_ref)):
            gcol = jnp.zeros((8, 1), F32)
            for r in range(C_REP):
                col = Z2_CG + branch * C_HEADS + g * C_REP + r
                gcol = jnp.where(r1 == r, jnp.sum(jnp.where(lane_g == col, grow, 0.0), axis=1, keepdims=True), gcol)
            o_ref[g] = o * _sigmoid(gcol)


def sample_attend(cache, win_buf, phys, rb, idx, ok, qn, kva, kvb, z2, l):
    nblk = C_KV_HEADS * SLC_TOPK

    def cblk(j):
        return pl.BlockSpec((None, None, SLC_BLK, 2, C_KV_HEADS, C_HD),
                            lambda b, ph, rbr, ix, okr: (l, ph[b * nblk + j], rbr[b * nblk + j], 1, 0, 0))

    rows = lambda w: pl.BlockSpec((S_ROWS, w), lambda b, *_: (0, 0))
    ospec = pl.BlockSpec((None, C_KV_HEADS, 8, C_HD), lambda b, *_: (b, 0, 0, 0))
    grid_spec = pltpu.PrefetchScalarGridSpec(
        num_scalar_prefetch=4,
        grid=(DEC_BATCH,),
        in_specs=[cblk(j) for j in range(nblk)] + [
            rows(C_WIDTH), rows(KVA_W), rows(KVB_W),
            pl.BlockSpec((None, None, WINDOW, 2, C_KV_HEADS, C_HD), lambda b, *_: (l, b, 0, 0, 0, 0)),
            rows(Z2_W),
        ],
        out_specs=[ospec, ospec],
        scratch_shapes=[pltpu.VMEM((SLC_TOPK * SLC_BLK, C_HD), F32), pltpu.VMEM((SLC_TOPK * SLC_BLK, C_HD), F32)],
    )
    out = jax.ShapeDtypeStruct((DEC_BATCH, C_KV_HEADS, 8, C_HD), F32)
    return pl.pallas_call(
        _sample_attend_kernel,
        grid_spec=grid_spec,
        out_shape=[out, out],
        compiler_params=_cp(("arbitrary",)),
        name="sample_attend",
    )(phys, rb, idx, ok, *([cache] * nblk), qn, kva, kvb, win_buf, z2)


def _sample_ab_kernel(au_ref, av_ref, bq_ref, bk_ref, bv_ref, br_ref, z2_ref, lng_ref, lnb_ref, w00_ref, b0_ref,
                      gw_ref, gb_ref, on_ref, s0_ref, oa_ref, ob_ref, av_out_ref, s_ref):
    u = _gelu(au_ref[...].astype(F32))
    v = _gelu(av_ref[...].astype(F32))
    mu = jnp.mean(v, axis=-1, keepdims=True)
    vc = v - mu
    v = vc * lax.rsqrt(jnp.mean(vc * vc, axis=-1, keepdims=True) + EPS) * lng_ref[...] + lnb_ref[...]
    av_out_ref[...] = v
    oa_ref[...] = u * (v * w00_ref[...] + b0_ref[...])
    la = _log_sigmoid(_dot(z2_ref[...].astype(BF16), gw_ref[...]) + gb_ref[...]) * (1.0 / B_GATE_TAU)
    q = bq_ref[...].astype(F32) * (B_DK ** -0.5)
    k = bk_ref[...].astype(F32)
    vv = bv_ref[...].astype(F32)
    gate = br_ref[...].astype(F32)
    gate = gate * _sigmoid(gate)
    qd = q * jnp.exp(la)
    kd = k * jnp.exp(-la)
    on = on_ref[...]
    ob_ref[...] = jnp.zeros_like(ob_ref)
    for b in range(DEC_BATCH):
        for h in range(B_HEADS):
            dk = slice(h * B_DK, (h + 1) * B_DK)
            dv = slice(h * B_DV, (h + 1) * B_DV)
            qd_r = qd[b:b + 1, dk]
            v_r = vv[b:b + 1, dv]
            att = jnp.sum(qd_r * kd[b:b + 1, dk], axis=1, keepdims=True)
            s0 = s0_ref[b, h]
            o = att * v_r + _dot(jnp.broadcast_to(qd_r, (8, B_DK)).astype(BF16), s0.astype(BF16))[0:1]
            s_ref[b, h] = s0 * _col_from_row(jnp.exp(la[b:b + 1, dk])) + _col_from_row(k[b:b + 1, dk]) * v_r
            ob_ref[b:b + 1, dv] = _rms(o, on) * gate[b:b + 1, dv]


def sample_ab(z1, z2, lng, lnb, w00, b0, gw2p, gb, on_g, state_gla, l):
    blk = lambda w, off: pl.BlockSpec((S_ROWS, w), lambda i: (0, off // w))
    row = lambda n: jax.ShapeDtypeStruct((S_ROWS, n), F32)
    rspec = lambda n: pl.BlockSpec((S_ROWS, n), lambda i: (0, 0))
    sspec = pl.BlockSpec((None, DEC_BATCH, B_HEADS, B_DK, B_DV), lambda i: (l, 0, 0, 0, 0))
    return pl.pallas_call(
        _sample_ab_kernel,
        grid=(1,),
        in_specs=[
            blk(A_WIDTH, Z_AU), blk(A_WIDTH, Z_AV), blk(256, Z_BQ), blk(256, Z_BK), blk(B_WIDTH, Z_BV),
            blk(B_WIDTH, Z_BR), blk(Z2_W, 0),
            _lspec(l, 1, A_WIDTH), _lspec(l, 1, A_WIDTH), _lspec(l, 1, A_WIDTH), _lspec(l, 1, A_WIDTH),
            _lspec(l, Z2_W, 256), _lspec(l, 1, 256), _lspec(l, 1, B_DV),
            sspec,
        ],
        out_specs=[rspec(A_WIDTH), rspec(B_WIDTH), rspec(A_WIDTH),
                   pl.BlockSpec((DEC_BATCH, B_HEADS, B_DK, B_DV), lambda i: (0, 0, 0, 0))],
        out_shape=[row(A_WIDTH), row(B_WIDTH), row(A_WIDTH),
                   jax.ShapeDtypeStruct((DEC_BATCH, B_HEADS, B_DK, B_DV), F32)],
        compiler_params=_cp(("arbitrary",)),
        name="sample_ab",
    )(z1, z1, z1, z1, z1, z1, z2, lng, lnb, w00, b0, gw2p, gb, on_g, state_gla)


def _m5s_table():
    m5 = np.zeros((N_SEG, SLC_LANES), np.float32)
    for j in range(N_PAST_BLK + 1):
        for m in range(max(4 * j, 1), min(4 * j + 4, N_SEG - 1) + 1):
            m5[m, j] = 1.0
    return m5


def _rope_tables(pos):
    half = C_ROT // 2
    inv = jnp.float32(ROPE_THETA) ** (-jnp.arange(half, dtype=F32) / half)
    ang = pos.astype(F32)[:, None] * inv[None, :]
    cos = jnp.cos(ang)
    sin = jnp.sin(ang)
    n = pos.shape[0]
    ones = jnp.ones((n, C_HD - C_ROT), F32)
    zeros = jnp.zeros((n, C_HD - half), F32)
    c = jnp.concatenate([cos, cos, ones], axis=1)
    s1 = jnp.concatenate([-sin, zeros], axis=1)
    s2 = jnp.concatenate([jnp.zeros((n, half), F32), sin, jnp.zeros((n, C_HD - C_ROT), F32)], axis=1)
    return c, s1, s2


def _m5_table():
    m5 = np.zeros((N_CMP, N_CMP), np.float32)
    for j in range(SEQ // SLC_BLK):
        for n in range(4 * j - 1, 4 * j + 4):
            if 0 <= n < N_CMP - 1:
                m5[n, j] = 1.0
    return m5


def _permute_w_in(w_in):
    o = np.cumsum([0, 512, 512, 256, 256, 512, 16, 512, 1024, 1536, 24, 6144])
    seg = lambda i: w_in[..., o[i]:o[i + 1]]
    w1 = jnp.concatenate([seg(7), seg(0), seg(1), seg(4), seg(6), seg(8), seg(2), seg(3), seg(10)], axis=-1)
    pad = jnp.zeros(w_in.shape[:-1] + (Z2_W - 40,), w_in.dtype)
    w2 = jnp.concatenate([seg(5), seg(9), pad], axis=-1)
    return w1.astype(BF16), w2.astype(BF16)


def kernel(x_prompt, x_sample, cache_nsa_kv, state_win_kv, state_gla, page_table, c_prompt, c_sample, ada_w, ada_b, norm1_g, norm2_g, w_in, a_ln_g, a_ln_b, a_ws, a_bs, b_gw2, b_gb, b_on_g, c_qn_g, c_kn_g, cmp_pool, cmp_w1, cmp_b1, cmp_w2, cmp_b2, w_br_a, w_br_b, w_br_c, w_out, p_wq, p_qn_g, p_subkeys, p_u, p_v):
    tm_p, tm_s = 512, T_S
    w1_all, w2_all = _permute_w_in(w_in)
    wa_all = w_br_a.astype(BF16)
    wb_all = w_br_b.astype(BF16)
    wc_all = w_br_c.astype(BF16)
    wo_all = w_out.astype(BF16)
    wq_all = p_wq.astype(BF16)
    u_all = p_u.astype(BF16)
    v_all = p_v.astype(BF16)
    row3 = lambda a: a.reshape(DEPTH, 1, a.shape[-1])
    g1_all, g2_all = row3(norm1_g), row3(norm2_g)
    lng_all, lnb_all = row3(a_ln_g), row3(a_ln_b)
    bst_all = jnp.swapaxes(a_bs, 1, 2)
    gw2p_all = jnp.pad(b_gw2, ((0, 0), (0, Z2_W - B_GATE_RANK), (0, 0))).astype(BF16)
    gb_all, on_all = row3(b_gb), row3(b_on_g)
    gq_all, pqg_all = row3(c_qn_g), row3(p_qn_g)
    cb1_all = cmp_b1.reshape(DEPTH, 2, 1, C_HD)
    cb2_all = cmp_b2.reshape(DEPTH, 2, 1, C_HD)
    c40 = jnp.concatenate([jnp.repeat(c_prompt, N_SEQ_ROWS, axis=0), c_sample], axis=0)
    mod = adaln_table(c40, ada_w, ada_b)
    rope_p = _rope_tables(jnp.arange(SEQ, dtype=jnp.int32))
    rope_s = _rope_tables(jnp.full((T_S,), PAST_LEN, jnp.int32))
    m5 = jnp.asarray(_m5_table(), BF16)
    xp = x_prompt.reshape(T_P, D_MODEL)
    xs = jnp.pad(x_sample.reshape(DEC_BATCH, D_MODEL), ((0, T_S - DEC_BATCH), (0, 0)))
    pool_w = lambda p: jnp.broadcast_to(jnp.swapaxes(p, 1, 2)[:, :, :, None, None], (DEPTH, CMP_STRIDE, 2, C_KV_HEADS, C_HD))
    wf_all = pool_w(cmp_pool[:, :, :CMP_STRIDE])
    ws_all = pool_w(cmp_pool[:, :, CMP_STRIDE:])
    w00_all = jnp.repeat(a_ws[:, :, 0, 0], A_GW, axis=1)[:, None, :]
    b0_all = jnp.repeat(a_bs[:, :, 0], A_GW, axis=1)[:, None, :]
    m5s = jnp.asarray(_m5s_table(), BF16)
    ckv_s = sample_cmp_mlp(sample_pool(cache_nsa_kv, page_table.reshape(-1), wf_all, ws_all), cmp_w1, cb1_all, cmp_w2, cb2_all)

    outs = {k: [] for k in ('kv_p', 'kv_s', 'win_p', 'win_s', 'gla_p', 'gla_s', 'cv_s')}
    yp = ys = None
    for l in range(DEPTH):
        streams = []
        for is_s, x, y, tm in ((False, xp, yp, tm_p), (True, xs, ys, tm_s)):
            if l == 0:
                h = norm_mod(x, g1_all, mod, l, 0, 1, tm, is_s)
            else:
                x, h = resid_norm_mod(x, y, g1_all, mod, l - 1, l, 5, 0, 1, tm, is_s)
            z1 = matmul(h, w1_all, l, tm, 1024, BF16, "w_in")
            z2 = matmul(h, w2_all, l, tm, Z2_W, F32, "w_in_gates")
            rope = rope_s if is_s else rope_p
            qn, kva, kvb = qk_prep(z1, gq_all, c_kn_g, l, rope[0], rope[1], rope[2], tm, is_s)
            streams.append((x, z1, z2, qn, kva, kvb))
        x, z1, z2, qn, kva, kvb = streams[0]
        o_a = chunk_mlp_prompt(z1, lng_all, lnb_all, a_ws, bst_all, l)
        o_b, s_p = gla_prompt(z1, z2, gw2p_all, gb_all, on_all, l)
        ckv = compress_prompt(kva, cmp_pool, cmp_w1, cb1_all, cmp_w2, cb2_all, l)
        o_cmp, selt = cmp_select_prompt(qn, ckv, z2, m5)
        o_slc = flash_prompt(qn, kva, selt, z2, 1)
        o_win = flash_prompt(qn, kvb, selt, z2, 2)
        mixed_p = (o_a, o_b, o_cmp, o_slc, o_win)
        outs['kv_p'].append(kva.reshape(BATCH, SEQ, 4, C_KV_HEADS, C_HD))
        outs['win_p'].append(kvb.reshape(BATCH, SEQ, 2, C_KV_HEADS, C_HD)[:, SEQ - WINDOW:])
        outs['gla_p'].append(s_p)
        xs_, z1s, z2s, qns, kvas, kvbs = streams[1]
        oa_s, ob_s, av_s, s_s = sample_ab(z1s, z2s, lng_all, lnb_all, w00_all, b0_all, gw2p_all, gb_all, on_all, state_gla, l)
        ocmp_s, idx, ok = sample_cmp(qns, ckv_s, z2s, m5s, l)
        nsel = C_KV_HEADS * SLC_TOPK
        idx_f = idx[:, 0, :nsel].reshape(-1)
        ok_f = ok[:, 0, :nsel].reshape(-1)
        jc = jnp.minimum(idx_f, N_PAST_BLK - 1)
        bpp = PAGE_SIZE // SLC_BLK
        phys = page_table[jnp.repeat(jnp.arange(DEC_BATCH), nsel), jc // bpp]
        oslc_s, owin_s = sample_attend(cache_nsa_kv, state_win_kv, phys, jc % bpp, idx_f, ok_f, qns, kvas, kvbs, z2s, l)
        pad_s = lambda a: jnp.pad(a, ((0, T_S - a.shape[0]), (0, 0))).astype(BF16)
        heads = lambda o: o[:, :, :C_REP].reshape(DEC_BATCH, C_WIDTH)
        mixed_s = (pad_s(oa_s), pad_s(ob_s), pad_s(ocmp_s.reshape(DEC_BATCH, C_WIDTH)), pad_s(heads(oslc_s)), pad_s(heads(owin_s)))
        outs['kv_s'].append(kvas[:DEC_BATCH].reshape(DEC_BATCH, 1, 4, C_KV_HEADS, C_HD))
        outs['win_s'].append(kvbs[:DEC_BATCH].reshape(DEC_BATCH, 1, 2, C_KV_HEADS, C_HD))
        outs['gla_s'].append(s_s)
        outs['cv_s'].append(av_s[:DEC_BATCH, None, :])
        new = []
        for is_s, st, mixed, tm in ((False, streams[0], mixed_p, tm_p), (True, streams[1], mixed_s, tm_s)):
            x, z1 = st[0], st[1]
            mg = merge(*mixed, z1, wa_all, wb_all, wc_all, l, tm)
            att = matmul(mg, wo_all, l, tm, 1024, F32, "w_out")
            x1, h2 = resid_norm_mod(x, att, g2_all, mod, l, l, 2, 3, 4, tm, is_s)
            pq = peer_query(h2, wq_all, pqg_all, l, tm)
            r2, cnt, e1, e2 = peer_topk(pq, p_subkeys, l, tm)
            y = peer_main(h2, u_all, v_all, r2, cnt, e1, e2, l, tm)
            new.append((x1, y))
        (xp, yp), (xs, ys) = new
    xp = resid(xp, yp, mod, DEPTH - 1, 5, tm_p, False)
    xs = resid(xs, ys, mod, DEPTH - 1, 5, tm_s, True)
    return (
        xp.reshape(BATCH, SEQ, D_MODEL),
        xs[:DEC_BATCH].reshape(DEC_BATCH, 1, D_MODEL),
        jnp.stack(outs['kv_p']),
        jnp.stack(outs['kv_s']),
        jnp.stack(outs['win_p']),
        jnp.stack(outs['win_s']),
        jnp.stack(outs['gla_p']),
        jnp.stack(outs['gla_s']),
        jnp.stack(outs['cv_s']),
    )
```

```python
import functools

import numpy as np
import jax
import jax.numpy as jnp
from jax import lax
from jax.experimental import pallas as pl
from jax.experimental.pallas import tpu as pltpu

F32 = jnp.float32
BF16 = jnp.bfloat16

D_MODEL = 2048
BATCH = 4
SEQ = 2048
DEPTH = 4
DEC_BATCH = 8
PAST_LEN = 16384
PAGE_SIZE = 128
EPS = 1e-6
NEG = -1.0e30
A_WIDTH = 512
A_GROUPS = 4
A_GW = 128
A_CHUNK = 128
B_HEADS = 4
B_WIDTH = 512
B_DV = 128
B_DK = 64
B_GATE_RANK = 16
B_GATE_TAU = 16.0
B_CHUNK = 64
C_HEADS = 8
C_KV_HEADS = 2
C_HD = 128
C_WIDTH = 1024
C_REP = 4
C_ROT = 32
ROPE_THETA = 500000.0
CMP_LEN = 32
CMP_STRIDE = 16
SLC_BLK = 64
SLC_TOPK = 16
FORCE_BONUS = 1.0e4
WINDOW = 512
P_HEADS = 8
P_NKEYS = 128
P_NEXP = P_NKEYS * P_NKEYS
P_HALF = 128
P_TOPK = 16

T_P = BATCH * SEQ
T_S = 128
N_SEQ_ROWS = 8
N_CMP = SEQ // CMP_STRIDE

Z_CQ = 0
Z_AU = 1024
Z_AV = 1536
Z_BV = 2048
Z_BR = 2560
Z_CKV = 3072
Z_BQ = 4608
Z_BK = 4864
Z_MG = 5120
Z1_W = 11264
Z2_W = 128
Z2_CG = 16
KVA_W = 1024
KVB_W = 512

VMEM_LIMIT = 56 * 1024 * 1024


def _cp(sem, vmem=VMEM_LIMIT):
    return pltpu.CompilerParams(dimension_semantics=sem, vmem_limit_bytes=vmem)


def _lspec(l, *dims):
    nd = len(dims)
    return pl.BlockSpec((None,) + tuple(dims), lambda *_: (l,) + (0,) * nd)


def _gelu(x):
    return 0.5 * x * (1.0 + jnp.tanh(0.7978845608028654 * (x + 0.044715 * x * x * x)))


def _sigmoid(x):
    return 0.5 * jnp.tanh(0.5 * x) + 0.5


def _log_sigmoid(x):
    return jnp.minimum(x, 0.0) - jnp.log(1.0 + jnp.exp(-jnp.abs(x)))


def _dot(a, b):
    return jnp.dot(a, b, preferred_element_type=F32)


def _dot_nt(a, b):
    return lax.dot_general(a, b, (((1,), (1,)), ((), ())), preferred_element_type=F32)


def _split3(v):
    hi = v.astype(BF16)
    r1 = v - hi.astype(F32)
    mid = r1.astype(BF16)
    r2 = r1 - mid.astype(F32)
    lo = r2.astype(BF16)
    return hi, mid, lo


def _dot_exact_lhs(sel_bf16, v):
    hi, mid, lo = _split3(v)
    return (_dot(sel_bf16, hi) + _dot(sel_bf16, mid)) + _dot(sel_bf16, lo)


def _dot_exact_rhs(v, sel_bf16):
    hi, mid, lo = _split3(v)
    return (_dot(hi, sel_bf16) + _dot(mid, sel_bf16)) + _dot(lo, sel_bf16)


def _rms(x, g):
    return x * lax.rsqrt(jnp.mean(x * x, axis=-1, keepdims=True) + EPS) * g


def _mod_rows(y, scale8, shift8=None):
    tm, n = y.shape
    y3 = y.reshape(tm // N_SEQ_ROWS, N_SEQ_ROWS, n) * scale8[None]
    if shift8 is not None:
        y3 = y3 + shift8[None]
    return y3.reshape(tm, n)


def _gate_col(gates, col):
    lane = lax.broadcasted_iota(jnp.int32, gates.shape, 1)
    return jnp.sum(jnp.where(lane == col, gates, 0.0), axis=1, keepdims=True)


def _adaln_kernel(c_ref, w_ref, b_ref, o_ref):
    c = c_ref[...]
    s = (c * _sigmoid(c)).astype(BF16)
    o_ref[...] = _dot(s, w_ref[...].astype(BF16)) + b_ref[...]


def adaln_table(c40, ada_w, ada_b):
    tn = 1024
    n = 6 * D_MODEL
    return pl.pallas_call(
        _adaln_kernel,
        grid=(DEPTH, n // tn),
        in_specs=[
            pl.BlockSpec((5 * N_SEQ_ROWS, D_MODEL), lambda l, j: (0, 0)),
            pl.BlockSpec((None, D_MODEL, tn), lambda l, j: (l, 0, j)),
            pl.BlockSpec((None, 1, tn), lambda l, j: (l, 0, j)),
        ],
        out_specs=pl.BlockSpec((None, 5 * N_SEQ_ROWS, tn), lambda l, j: (l, 0, j)),
        out_shape=jax.ShapeDtypeStruct((DEPTH, 5 * N_SEQ_ROWS, n), F32),
        compiler_params=_cp(("arbitrary", "arbitrary")),
        name="adaln_table",
    )(c40, ada_w, ada_b.reshape(DEPTH, 1, n))


def _seq_of(tm, is_sample):
    if is_sample:
        return lambda i: 4
    per = SEQ // tm
    return lambda i: i // per


def _mod_spec(l, sq, k):
    return pl.BlockSpec((None, N_SEQ_ROWS, D_MODEL), lambda i: (l, sq(i), k))


def _norm_mod_kernel(x_ref, g_ref, sh_ref, sc_ref, h_ref):
    y = _rms(x_ref[...], g_ref[...])
    h_ref[...] = _mod_rows(y, 1.0 + sc_ref[...], sh_ref[...]).astype(h_ref.dtype)


def norm_mod(x, g, mod, l, k_shift, k_scale, tm, is_sample):
    t = x.shape[0]
    sq = _seq_of(tm, is_sample)
    return pl.pallas_call(
        _norm_mod_kernel,
        grid=(t // tm,),
        in_specs=[
            pl.BlockSpec((tm, D_MODEL), lambda i: (i, 0)),
            _lspec(l, 1, D_MODEL),
            _mod_spec(l, sq, k_shift),
            _mod_spec(l, sq, k_scale),
        ],
        out_specs=pl.BlockSpec((tm, D_MODEL), lambda i: (i, 0)),
        out_shape=jax.ShapeDtypeStruct((t, D_MODEL), BF16),
        compiler_params=_cp(("arbitrary",)),
        name="norm_mod",
    )(x, g, mod, mod)


def _resid_norm_mod_kernel(x_ref, y_ref, gate_ref, g_ref, sh_ref, sc_ref, xo_ref, h_ref):
    xn = x_ref[...] + _mod_rows(y_ref[...], gate_ref[...])
    xo_ref[...] = xn
    h_ref[...] = _mod_rows(_rms(xn, g_ref[...]), 1.0 + sc_ref[...], sh_ref[...]).astype(h_ref.dtype)


def resid_norm_mod(x, y, g, mod, l_gate, l, k_gate, k_shift, k_scale, tm, is_sample):
    t = x.shape[0]
    sq = _seq_of(tm, is_sample)
    row = pl.BlockSpec((tm, D_MODEL), lambda i: (i, 0))
    return pl.pallas_call(
        _resid_norm_mod_kernel,
        grid=(t // tm,),
        in_specs=[
            row,
            row,
            _mod_spec(l_gate, sq, k_gate),
            _lspec(l, 1, D_MODEL),
            _mod_spec(l, sq, k_shift),
            _mod_spec(l, sq, k_scale),
        ],
        out_specs=[row, row],
        out_shape=[jax.ShapeDtypeStruct((t, D_MODEL), F32), jax.ShapeDtypeStruct((t, D_MODEL), BF16)],
        compiler_params=_cp(("arbitrary",)),
        name="resid_norm_mod",
    )(x, y, mod, g, mod, mod)


def _resid_kernel(x_ref, y_ref, gate_ref, xo_ref):
    xo_ref[...] = x_ref[...] + _mod_rows(y_ref[...], gate_ref[...])


def resid(x, y, mod, l, k_gate, tm, is_sample):
    t = x.shape[0]
    sq = _seq_of(tm, is_sample)
    row = pl.BlockSpec((tm, D_MODEL), lambda i: (i, 0))
    return pl.pallas_call(
        _resid_kernel,
        grid=(t // tm,),
        in_specs=[row, row, _mod_spec(l, sq, k_gate)],
        out_specs=row,
        out_shape=jax.ShapeDtypeStruct((t, D_MODEL), F32),
        compiler_params=_cp(("arbitrary",)),
        name="resid",
    )(x, y, mod)


def _mm_kernel(x_ref, w_ref, o_ref):
    o_ref[...] = _dot(x_ref[...], w_ref[...]).astype(o_ref.dtype)


def matmul(x, w, l, tm, tn, out_dtype, name):
    m, k = x.shape
    n = w.shape[2]
    return pl.pallas_call(
        _mm_kernel,
        grid=(n // tn, m // tm),
        in_specs=[
            pl.BlockSpec((tm, k), lambda j, i: (i, 0)),
            pl.BlockSpec((None, k, tn), lambda j, i: (l, 0, j)),
        ],
        out_specs=pl.BlockSpec((tm, tn), lambda j, i: (i, j)),
        out_shape=jax.ShapeDtypeStruct((m, n), out_dtype),
        compiler_params=_cp(("arbitrary", "arbitrary")),
        name=name,
    )(x, w)


def _wq_kernel(x_ref, w_ref, g_ref, o_ref):
    acc = _dot(x_ref[...], w_ref[...])
    g = g_ref[...]
    for c in range(acc.shape[1] // P_HALF):
        a = acc[:, c * P_HALF:(c + 1) * P_HALF]
        o_ref[:, c * P_HALF:(c + 1) * P_HALF] = _rms(a, g).astype(o_ref.dtype)


def peer_query(h2, wq, qn_g, l, tm):
    m = h2.shape[0]
    tn = 512
    return pl.pallas_call(
        _wq_kernel,
        grid=(D_MODEL // tn, m // tm),
        in_specs=[
            pl.BlockSpec((tm, D_MODEL), lambda j, i: (i, 0)),
            pl.BlockSpec((None, D_MODEL, tn), lambda j, i: (l, 0, j)),
            _lspec(l, 1, P_HALF),
        ],
        out_specs=pl.BlockSpec((tm, tn), lambda j, i: (i, j)),
        out_shape=jax.ShapeDtypeStruct((m, D_MODEL), BF16),
        compiler_params=_cp(("arbitrary", "arbitrary")),
        name="peer_query",
    )(h2, wq, qn_g)


def _qk_prep_kernel(zq_ref, zkv_ref, gq_ref, gk_ref, c_ref, s1_ref, s2_ref, qn_ref, kva_ref, kvb_ref):
    cos = c_ref[...]
    s1 = s1_ref[...]
    s2 = s2_ref[...]

    def norm_rope(x, g):
        xn = _rms(x, g)
        return xn * cos + pltpu.roll(xn, C_HD - C_ROT // 2, 1) * s1 + pltpu.roll(xn, C_ROT // 2, 1) * s2

    gq = gq_ref[...]
    for h in range(C_HEADS):
        x = zq_ref[:, h * C_HD:(h + 1) * C_HD].astype(F32)
        qn_ref[:, h * C_HD:(h + 1) * C_HD] = norm_rope(x, gq).astype(qn_ref.dtype)
    for s in range(6):
        for g in range(C_KV_HEADS):
            c0 = (s * C_KV_HEADS + g) * C_HD
            x = zkv_ref[:, c0:c0 + C_HD].astype(F32)
            if s % 2 == 0:
                x = norm_rope(x, gk_ref[s // 2:s // 2 + 1, :])
            if c0 < KVA_W:
                kva_ref[:, c0:c0 + C_HD] = x
            else:
                kvb_ref[:, c0 - KVA_W:c0 - KVA_W + C_HD] = x


def qk_prep(z1, gq, gk, l, rope_c, rope_s1, rope_s2, tm, is_sample):
    t = z1.shape[0]
    per = 1 if is_sample else SEQ // tm
    tab = pl.BlockSpec((tm, C_HD), lambda i: (i % per, 0))
    return pl.pallas_call(
        _qk_prep_kernel,
        grid=(t // tm,),
        in_specs=[
            pl.BlockSpec((tm, C_WIDTH), lambda i: (i, Z_CQ // C_WIDTH)),
            pl.BlockSpec((tm, 1536), lambda i: (i, Z_CKV // 1536)),
            _lspec(l, 1, C_HD),
            _lspec(l, 3, C_HD),
            tab, tab, tab,
        ],
        out_specs=[
            pl.BlockSpec((tm, C_WIDTH), lambda i: (i, 0)),
            pl.BlockSpec((tm, KVA_W), lambda i: (i, 0)),
            pl.BlockSpec((tm, KVB_W), lambda i: (i, 0)),
        ],
        out_shape=[
            jax.ShapeDtypeStruct((t, C_WIDTH), BF16),
            jax.ShapeDtypeStruct((t, KVA_W), F32),
            jax.ShapeDtypeStruct((t, KVB_W), F32),
        ],
        compiler_params=_cp(("arbitrary",)),
        name="qk_prep",
    )(z1, z1, gq, gk, rope_c, rope_s1, rope_s2)


def _chunk_mlp_kernel(au_ref, av_ref, lng_ref, lnb_ref, ws_ref, bst_ref, o_ref):
    u = _gelu(au_ref[...].astype(F32))
    v = _gelu(av_ref[...].astype(F32))
    mu = jnp.mean(v, axis=-1, keepdims=True)
    vc = v - mu
    var = jnp.mean(vc * vc, axis=-1, keepdims=True)
    v = vc * lax.rsqrt(var + EPS) * lng_ref[...] + lnb_ref[...]
    r = lax.broadcasted_iota(jnp.int32, (A_CHUNK, A_CHUNK), 0)
    c = lax.broadcasted_iota(jnp.int32, (A_CHUNK, A_CHUNK), 1)
    tril = r >= c
    tm = u.shape[0]
    for g in range(A_GROUPS):
        w = jnp.where(tril, ws_ref[g], 0.0).astype(BF16)
        bias = bst_ref[:, g:g + 1]
        for ch in range(tm // A_CHUNK):
            rows = slice(ch * A_CHUNK, (ch + 1) * A_CHUNK)
            cols = slice(g * A_GW, (g + 1) * A_GW)
            mixed = _dot(w, v[rows, cols].astype(BF16)) + bias
            o_ref[rows, cols] = (u[rows, cols] * mixed).astype(o_ref.dtype)


def chunk_mlp_prompt(z1, ln_g, ln_b, ws, bs_t, l):
    tm = 512
    return pl.pallas_call(
        _chunk_mlp_kernel,
        grid=(T_P // tm,),
        in_specs=[
            pl.BlockSpec((tm, A_WIDTH), lambda i: (i, Z_AU // A_WIDTH)),
            pl.BlockSpec((tm, A_WIDTH), lambda i: (i, Z_AV // A_WIDTH)),
            _lspec(l, 1, A_WIDTH),
            _lspec(l, 1, A_WIDTH),
            _lspec(l, A_GROUPS, A_CHUNK, A_CHUNK),
            _lspec(l, A_CHUNK, A_GROUPS),
        ],
        out_specs=pl.BlockSpec((tm, A_WIDTH), lambda i: (i, 0)),
        out_shape=jax.ShapeDtypeStruct((T_P, A_WIDTH), BF16),
        compiler_params=_cp(("arbitrary",)),
        name="chunk_mlp",
    )(z1, z1, ln_g, ln_b, ws, bs_t)


def _col_from_row(row):
    n = row.shape[1]
    r = lax.broadcasted_iota(jnp.int32, (n, n), 0)
    c = lax.broadcasted_iota(jnp.int32, (n, n), 1)
    return jnp.sum(jnp.where(r == c, jnp.broadcast_to(row, (n, n)), 0.0), axis=1, keepdims=True)


def _gla_kernel(q_ref, k_ref, v_ref, r_ref, bg_ref, gw_ref, gb_ref, on_ref, o_ref, s_out_ref, s_sc):
    n = pl.program_id(0)

    @pl.when(n == 0)
    def _():
        s_sc[...] = jnp.zeros_like(s_sc)

    c = B_CHUNK
    ri = lax.broadcasted_iota(jnp.int32, (c, c), 0)
    ci = lax.broadcasted_iota(jnp.int32, (c, c), 1)
    tril = ri >= ci
    eye = jnp.where(ri == ci, 1.0, 0.0).astype(BF16)
    ones_tril = jnp.where(tril, 1.0, 0.0).astype(BF16)
    on = on_ref[...]
    for sq in range(BATCH):
        la = _log_sigmoid(_dot(bg_ref[sq].astype(BF16), gw_ref[...]) + gb_ref[...]) * (1.0 / B_GATE_TAU)
        bc = _dot_exact_lhs(ones_tril, la)
        q = q_ref[sq].astype(F32) * (B_DK ** -0.5)
        k = k_ref[sq].astype(F32)
        v = v_ref[sq].astype(BF16)
        gate = r_ref[sq].astype(F32)
        gate = gate * _sigmoid(gate)
        for h in range(B_HEADS):
            dk = slice(h * B_DK, (h + 1) * B_DK)
            dv = slice(h * B_DV, (h + 1) * B_DV)
            b = bc[:, dk]
            qd = (q[:, dk] * jnp.exp(b)).astype(BF16)
            kd = (k[:, dk] * jnp.exp(-b)).astype(BF16)
            att = jnp.where(tril, _dot_nt(qd, kd), 0.0)
            s_old = s_sc[sq, h]
            o = _dot(att.astype(BF16), v[:, dv]) + _dot(qd, s_old.astype(BF16))
            blast = b[c - 1:c, :]
            kdec = (k[:, dk] * jnp.exp(blast - b)).astype(BF16)
            kdec_t = _dot_nt(eye, kdec).astype(BF16)
            s_sc[sq, h] = s_old * _col_from_row(jnp.exp(blast)) + _dot(kdec_t, v[:, dv])
            o_ref[sq, :, dv] = (_rms(o, on) * gate[:, dv]).astype(o_ref.dtype)

    @pl.when(n == pl.num_programs(0) - 1)
    def _():
        s_out_ref[...] = s_sc[...]


def gla_prompt(z1, z2, gw2p, gb, on_g, l):
    c = B_CHUNK
    z1b = z1.reshape(BATCH, SEQ, Z1_W)
    z2b = z2.reshape(BATCH, SEQ, Z2_W)
    blk = lambda w, off: pl.BlockSpec((BATCH, c, w), lambda n: (0, n, off // w))
    state = jax.ShapeDtypeStruct((BATCH, B_HEADS, B_DK, B_DV), F32)
    o_b, s_p = pl.pallas_call(
        _gla_kernel,
        grid=(SEQ // c,),
        in_specs=[
            blk(256, Z_BQ), blk(256, Z_BK), blk(B_WIDTH, Z_BV), blk(B_WIDTH, Z_BR), blk(Z2_W, 0),
            _lspec(l, Z2_W, 256),
            _lspec(l, 1, 256),
            _lspec(l, 1, B_DV),
        ],
        out_specs=[
            pl.BlockSpec((BATCH, c, B_WIDTH), lambda n: (0, n, 0)),
            pl.BlockSpec((BATCH, B_HEADS, B_DK, B_DV), lambda n: (0, 0, 0, 0)),
        ],
        out_shape=[jax.ShapeDtypeStruct((BATCH, SEQ, B_WIDTH), BF16), state],
        scratch_shapes=[pltpu.VMEM((BATCH, B_HEADS, B_DK, B_DV), F32)],
        compiler_params=_cp(("arbitrary",)),
        name="gla",
    )(z1b, z1b, z1b, z1b, z2b, gw2p, gb, on_g)
    return o_b.reshape(T_P, B_WIDTH), s_p


def _compress_kernel(pool_ref, rows_ref, w1_ref, b1_ref, w2_ref, b2_ref, o_ref, *, l):
    kv = pl.program_id(2)
    r = lax.broadcasted_iota(jnp.int32, (N_CMP, SEQ), 0)
    c = lax.broadcasted_iota(jnp.int32, (N_CMP, SEQ), 1)
    d = c - CMP_STRIDE * r
    p = jnp.zeros((N_CMP, SEQ), F32)
    for i in range(CMP_LEN):
        p = jnp.where(d == i, pool_ref[l, kv, i], p)
    rows = rows_ref[...]
    ph, pm, plo = _split3(p)
    rh, rm, rl = _split3(rows)
    pooled = (_dot(ph, rh) + (_dot(ph, rm) + _dot(pm, rh))) + ((_dot(pm, rm) + _dot(ph, rl)) + _dot(plo, rh))
    hid = _gelu(_dot(pooled.astype(BF16), w1_ref[...].astype(BF16)) + b1_ref[...])
    o_ref[...] = _dot(hid.astype(BF16), w2_ref[...].astype(BF16)) + b2_ref[...]


def compress_prompt(kva, pool, w1, b1, w2, b2, l):
    wspec = pl.BlockSpec((None, None, C_HD, C_HD), lambda b, g, kv: (l, kv, 0, 0))
    bspec = pl.BlockSpec((None, None, 1, C_HD), lambda b, g, kv: (l, kv, 0, 0))
    return pl.pallas_call(
        functools.partial(_compress_kernel, l=l),
        grid=(BATCH, C_KV_HEADS, 2),
        in_specs=[
            pl.BlockSpec(memory_space=pltpu.SMEM),
            pl.BlockSpec((SEQ, C_HD), lambda b, g, kv: (b, kv * C_KV_HEADS + g)),
            wspec, bspec, wspec, bspec,
        ],
        out_specs=pl.BlockSpec((None, None, None, N_CMP, C_HD), lambda b, g, kv: (b, g, kv, 0, 0)),
        out_shape=jax.ShapeDtypeStruct((BATCH, C_KV_HEADS, 2, N_CMP, C_HD), F32),
        compiler_params=_cp(("arbitrary", "arbitrary", "arbitrary")),
        name="compress",
    )(pool, kva, w1, b1, w2, b2)


def _softmax_rows(s, valid):
    s = jnp.where(valid, s, NEG)
    m = jnp.max(s, axis=-1, keepdims=True)
    e = jnp.where(valid, jnp.exp(s - m), 0.0)
    return e / jnp.maximum(jnp.sum(e, axis=-1, keepdims=True), 1e-30)


def _cmp_select_kernel(q_ref, ckv_ref, gate_ref, m5_ref, o_ref, selt_ref):
    g = pl.program_id(1)
    qi = pl.program_id(2)
    tq = q_ref.shape[0]
    ck = ckv_ref[0].astype(BF16)
    cv = ckv_ref[1].astype(BF16)
    pos = qi * tq + lax.broadcasted_iota(jnp.int32, (tq, N_CMP), 0)
    j = lax.broadcasted_iota(jnp.int32, (tq, N_CMP), 1)
    valid = j * CMP_STRIDE + (CMP_LEN - 1) <= pos
    gates = gate_ref[...]
    imp = jnp.zeros((tq, N_CMP), F32)
    for r in range(C_REP):
        q = q_ref[:, r * C_HD:(r + 1) * C_HD]
        p = _softmax_rows(_dot_nt(q, ck) * (C_HD ** -0.5), valid)
        imp = imp + p
        o = _dot(p.astype(BF16), cv)
        gcol = _gate_col(gates, Z2_CG + g * C_REP + r)
        o_ref[:, r * C_HD:(r + 1) * C_HD] = (o * _sigmoid(gcol)).astype(o_ref.dtype)
    nslc = SEQ // SLC_BLK
    hi, mid, lo = _split3(imp)
    m5t = m5_ref[...]
    score = ((_dot_nt(m5t, hi) + _dot_nt(m5t, mid)) + _dot_nt(m5t, lo))[0:nslc, :]
    jt = lax.broadcasted_iota(jnp.int32, (nslc, tq), 0)
    post = qi * tq + lax.broadcasted_iota(jnp.int32, (nslc, tq), 1)
    cur = post // SLC_BLK
    forced = (jt == 0) | (jt == cur) | (jt == cur - 1)
    score = jnp.where(jt * SLC_BLK <= post, score + jnp.where(forced, FORCE_BONUS, 0.0), NEG)
    rank = jnp.zeros((nslc, tq), F32)
    for jj in range(nslc):
        row = score[jj:jj + 1, :]
        ahead = (row > score) | ((row == score) & (jt > jj))
        rank = rank + jnp.where(ahead, 1.0, 0.0)
    sel = (rank < SLC_TOPK) & (score > 0.5 * NEG)
    selt_ref[0:nslc, :] = jnp.where(sel, 1.0, 0.0).astype(selt_ref.dtype)
    selt_ref[nslc:, :] = jnp.zeros((N_CMP - nslc, tq), selt_ref.dtype)


def cmp_select_prompt(qn, ckv, z2, m5):
    tq = 512
    nq = SEQ // tq
    return pl.pallas_call(
        _cmp_select_kernel,
        grid=(BATCH, C_KV_HEADS, nq),
        in_specs=[
            pl.BlockSpec((tq, C_REP * C_HD), lambda b, g, i: (b * nq + i, g)),
            pl.BlockSpec((None, None, 2, N_CMP, C_HD), lambda b, g, i: (b, g, 0, 0, 0)),
            pl.BlockSpec((tq, Z2_W), lambda b, g, i: (b * nq + i, 0)),
            pl.BlockSpec((N_CMP, N_CMP), lambda b, g, i: (0, 0)),
        ],
        out_specs=[
            pl.BlockSpec((tq, C_REP * C_HD), lambda b, g, i: (b * nq + i, g)),
            pl.BlockSpec((None, None, N_CMP, tq), lambda b, g, i: (b, g, 0, i)),
        ],
        out_shape=[
            jax.ShapeDtypeStruct((T_P, C_WIDTH), BF16),
            jax.ShapeDtypeStruct((BATCH, C_KV_HEADS, N_CMP, SEQ), BF16),
        ],
        compiler_params=_cp(("arbitrary", "arbitrary", "arbitrary")),
        name="cmp_select",
    )(qn, ckv, z2, m5)


def _flash_kernel(q_ref, k_ref, v_ref, selt_ref, gate_ref, o_ref, vt_sc, m_sc, l_sc, acc_sc, *, branch, tk):
    g = pl.program_id(1)
    qi = pl.program_id(2)
    tq = q_ref.shape[0]

    @pl.when(qi == 0)
    def _():
        for jt in range(SEQ // tk):
            vt_sc[jt] = v_ref[jt * tk:(jt + 1) * tk, :].T

    q = jnp.concatenate([q_ref[:, r * C_HD:(r + 1) * C_HD] for r in range(C_REP)], axis=0)
    m_sc[...] = jnp.full(m_sc.shape, NEG, F32)
    l_sc[...] = jnp.zeros(l_sc.shape, F32)
    acc_sc[...] = jnp.zeros(acc_sc.shape, F32)
    pos = qi * tq + lax.broadcasted_iota(jnp.int32, (tk, tq), 1)
    hi = ((qi + 1) * tq + tk - 1) // tk
    if branch == 1:
        lo = 0
        selt = selt_ref[...]
    else:
        lo = jnp.maximum(qi * tq - WINDOW, 0) // tk

    def step(kj, carry):
        k0 = pl.multiple_of(kj * tk, tk)
        k = k_ref[pl.ds(k0, tk), :].astype(BF16)
        vt = vt_sc[kj].astype(BF16)
        key = k0 + lax.broadcasted_iota(jnp.int32, (tk, tq), 0)
        if branch == 1:
            kb = (k0 + lax.broadcasted_iota(jnp.int32, (tk, N_CMP), 0)) // SLC_BLK
            expand = jnp.where(kb == lax.broadcasted_iota(jnp.int32, (tk, N_CMP), 1), 1.0, 0.0).astype(BF16)
            valid = (_dot(expand, selt) > 0.5) & (key <= pos)
        else:
            valid = (key <= pos) & (key > pos - WINDOW)
        valid4 = jnp.concatenate([valid] * C_REP, axis=1)
        s = jnp.where(valid4, _dot_nt(k, q) * (C_HD ** -0.5), NEG)
        m_old = m_sc[...]
        m_new = jnp.maximum(m_old, jnp.max(s, axis=0, keepdims=True))
        alpha = jnp.exp(m_old - m_new)
        p = jnp.where(valid4, jnp.exp(s - m_new), 0.0)
        l_sc[...] = alpha * l_sc[...] + jnp.sum(p, axis=0, keepdims=True)
        acc_sc[...] = alpha * acc_sc[...] + _dot(vt, p.astype(BF16))
        m_sc[...] = m_new
        return carry

    lax.fori_loop(lo, hi, step, 0)
    out = (acc_sc[...] / jnp.maximum(l_sc[...], 1e-30)).T
    gates = gate_ref[...]
    for r in range(C_REP):
        gcol = _gate_col(gates, Z2_CG + branch * C_HEADS + g * C_REP + r)
        o_ref[:, r * C_HD:(r + 1) * C_HD] = (out[r * tq:(r + 1) * tq] * _sigmoid(gcol)).astype(o_ref.dtype)


def flash_prompt(qn, kv, selt, z2, branch):
    tq = 256
    tk = 256
    nq = SEQ // tq
    kcol = 2 * C_KV_HEADS if branch == 1 else 0
    kern = functools.partial(_flash_kernel, branch=branch, tk=tk)
    return pl.pallas_call(
        kern,
        grid=(BATCH, C_KV_HEADS, nq),
        in_specs=[
            pl.BlockSpec((tq, C_REP * C_HD), lambda b, g, i: (b * nq + i, g)),
            pl.BlockSpec((SEQ, C_HD), lambda b, g, i: (b, kcol + g)),
            pl.BlockSpec((SEQ, C_HD), lambda b, g, i: (b, kcol + C_KV_HEADS + g)),
            pl.BlockSpec((None, None, N_CMP, tq), lambda b, g, i: (b, g, 0, i)),
            pl.BlockSpec((tq, Z2_W), lambda b, g, i: (b * nq + i, 0)),
        ],
        out_specs=pl.BlockSpec((tq, C_REP * C_HD), lambda b, g, i: (b * nq + i, g)),
        out_shape=jax.ShapeDtypeStruct((T_P, C_WIDTH), BF16),
        scratch_shapes=[
            pltpu.VMEM((SEQ // tk, C_HD, tk), F32),
            pltpu.VMEM((1, C_REP * tq), F32),
            pltpu.VMEM((1, C_REP * tq), F32),
            pltpu.VMEM((C_HD, C_REP * tq), F32),
        ],
        compiler_params=_cp(("arbitrary", "arbitrary", "arbitrary")),
        name="flash_slc" if branch == 1 else "flash_win",
    )(qn, kv, kv, selt, z2)


def _merge_kernel(oa_ref, ob_ref, oc0_ref, oc1_ref, oc2_ref, ga_ref, gb_ref, gc_ref, wa_ref, wb_ref, wc_ref, o_ref, oc_sc):
    @pl.when(pl.program_id(1) == 0)
    def _():
        oc = oc0_ref[...].astype(F32) + oc1_ref[...].astype(F32) + oc2_ref[...].astype(F32)
        oc_sc[...] = oc.astype(BF16)

    m = _sigmoid(ga_ref[...].astype(F32)) * _dot(oa_ref[...], wa_ref[...])
    m = m + _sigmoid(gb_ref[...].astype(F32)) * _dot(ob_ref[...], wb_ref[...])
    m = m + _sigmoid(gc_ref[...].astype(F32)) * _dot(oc_sc[...], wc_ref[...])
    o_ref[...] = m.astype(o_ref.dtype)


def merge(o_a, o_b, oc0, oc1, oc2, z1, wa, wb, wc, l, tm):
    t = o_a.shape[0]
    tn = 512
    nj = D_MODEL // tn
    gate = lambda k: pl.BlockSpec((tm, tn), lambda i, j: (i, (Z_MG + k * D_MODEL) // tn + j))
    wspec = lambda kdim: pl.BlockSpec((None, kdim, tn), lambda i, j: (l, 0, j))
    return pl.pallas_call(
        _merge_kernel,
        grid=(t // tm, nj),
        in_specs=[
            pl.BlockSpec((tm, A_WIDTH), lambda i, j: (i, 0)),
            pl.BlockSpec((tm, B_WIDTH), lambda i, j: (i, 0)),
            pl.BlockSpec((tm, C_WIDTH), lambda i, j: (i, 0)),
            pl.BlockSpec((tm, C_WIDTH), lambda i, j: (i, 0)),
            pl.BlockSpec((tm, C_WIDTH), lambda i, j: (i, 0)),
            gate(0), gate(1), gate(2),
            wspec(A_WIDTH), wspec(B_WIDTH), wspec(C_WIDTH),
        ],
        out_specs=pl.BlockSpec((tm, tn), lambda i, j: (i, j)),
        out_shape=jax.ShapeDtypeStruct((t, D_MODEL), BF16),
        scratch_shapes=[pltpu.VMEM((tm, C_WIDTH), BF16)],
        compiler_params=_cp(("arbitrary", "arbitrary")),
        name="merge",
    )(o_a, o_b, oc0, oc1, oc2, z1, z1, z1, wa, wb, wc)


N_CELLS = 50
N_CELL_ROWS = 56


def _cell_tables():
    sa = np.zeros((N_CELL_ROWS, P_TOPK), np.float32)
    sb = np.zeros((N_CELL_ROWS, P_TOPK), np.float32)
    r = 0
    for a in range(P_TOPK):
        for b in range(P_TOPK // (a + 1)):
            sa[r, a] = 1.0
            sb[r, b] = 1.0
            r += 1
    assert r == N_CELLS
    return sa, sb


def _top16_cols(s, row_iota, exact):
    rank = jnp.full(s.shape, float(P_TOPK), F32)
    work = s
    top = jnp.zeros((P_TOPK, s.shape[1]), F32)
    r_iota = lax.broadcasted_iota(jnp.int32, top.shape, 0)
    for r in range(P_TOPK):
        m = jnp.max(work, axis=0, keepdims=True)
        if exact:
            idx = jnp.min(jnp.where(work == m, row_iota, 1.0e9), axis=0, keepdims=True)
            hit = row_iota == idx
        else:
            hit = work == m
        rank = jnp.where(hit, float(r), rank)
        work = jnp.where(hit, -jnp.inf, work)
        top = jnp.where(r_iota == r, m, top)
    n_out = jnp.sum(jnp.where(rank < float(P_TOPK), 1.0, 0.0), axis=0, keepdims=True)
    return rank, top, n_out


def _peer_topk_kernel(qn_ref, sk_ref, sa_ref, sb_ref, sat_ref, r2_ref, cnt_ref, e1_ref, e2_ref):
    tb = qn_ref.shape[0]
    lanes = 128
    sk1 = sk_ref[0].astype(BF16)
    sk2 = sk_ref[1].astype(BF16)
    s1_all = _dot_nt(sk1, qn_ref[:, 0:P_HALF])
    s2_all = _dot_nt(sk2, qn_ref[:, P_HALF:2 * P_HALF])
    n_iota = lax.broadcasted_iota(jnp.int32, (P_NKEYS, lanes), 0).astype(F32)
    c_iota = lax.broadcasted_iota(jnp.int32, (N_CELL_ROWS, lanes), 0).astype(F32)
    sa = sa_ref[...]
    sb = sb_ref[...]
    sat = sat_ref[...]

    def slab(t, exact):
        cols = slice(t * lanes, (t + 1) * lanes)
        s1 = s1_all[:, cols]
        s2 = s2_all[:, cols]
        rank1, top1, n1 = _top16_cols(s1, n_iota, exact)
        rank2, top2, n2 = _top16_cols(s2, n_iota, exact)
        cand = _dot_exact_lhs(sa, top1) + _dot_exact_lhs(sb, top2)
        rankc, topc, nc = _top16_cols(jnp.where(c_iota < N_CELLS, cand, -jnp.inf), c_iota, exact)
        picked = jnp.where(rankc < float(P_TOPK), 1.0, 0.0)
        zsum = jnp.sum(jnp.exp(topc - topc[0:1, :]), axis=0, keepdims=True)
        cnt = _dot(sat, picked.astype(BF16))
        cntd = jnp.zeros(rank1.shape, F32)
        for a in range(P_TOPK):
            cntd = jnp.where(rank1 == float(a), cnt[a:a + 1, :], cntd)
        e1 = jnp.where(rank1 < float(P_TOPK), jnp.exp(s1 - top1[0:1, :]), 0.0) / zsum
        e2 = jnp.exp(s2 - top2[0:1, :])
        r2_ref[:, cols] = rank2.astype(r2_ref.dtype)
        cnt_ref[:, cols] = cntd
        e1_ref[:, cols] = e1
        e2_ref[:, cols] = e2.astype(e2_ref.dtype)
        return jnp.abs(n1 - P_TOPK) + jnp.abs(n2 - P_TOPK) + jnp.abs(nc - P_TOPK)

    for t in range(tb // lanes):
        tied = jnp.max(slab(t, False)) > 0.0

        @pl.when(tied)
        def _():
            slab(t, True)


def peer_topk(qn, subkeys, l, tb):
    t = qn.shape[0]
    sa, sb = _cell_tables()
    out = lambda dt: jax.ShapeDtypeStruct((P_HEADS, P_NKEYS, t), dt)
    ospec = pl.BlockSpec((None, P_NKEYS, tb), lambda i, h: (h, 0, i))
    return pl.pallas_call(
        _peer_topk_kernel,
        grid=(t // tb, P_HEADS),
        in_specs=[
            pl.BlockSpec((tb, 2 * P_HALF), lambda i, h: (i, h)),
            pl.BlockSpec((None, None, 2, P_NKEYS, P_HALF), lambda i, h: (l, h, 0, 0, 0)),
            pl.BlockSpec((N_CELL_ROWS, P_TOPK), lambda i, h: (0, 0)),
            pl.BlockSpec((N_CELL_ROWS, P_TOPK), lambda i, h: (0, 0)),
            pl.BlockSpec((P_TOPK, N_CELL_ROWS), lambda i, h: (0, 0)),
        ],
        out_specs=[ospec, ospec, ospec, ospec],
        out_shape=[out(BF16), out(F32), out(F32), out(BF16)],
        compiler_params=_cp(("arbitrary", "arbitrary")),
        name="peer_topk",
    )(qn, subkeys, jnp.asarray(sa, BF16), jnp.asarray(sb, BF16), jnp.asarray(sa.T, BF16))


PEER_EC = 512
PEER_NGRP = P_NEXP // PEER_EC


def _peer_main_kernel(h_ref, ua_ref, ub_ref, va_ref, vb_ref, r2_ref, cnt_ref, e1_ref, e2_ref, y_ref, pa_sc, pb_sc):
    e = pl.program_id(1)
    last = pl.num_programs(1) - 1
    nchunk = PEER_EC // P_NKEYS

    @pl.when(e == 0)
    def _():
        y_ref[...] = jnp.zeros_like(y_ref)
        pb_sc[...] = jnp.zeros_like(pb_sc)

    def gated(scores, grp):
        hid = _gelu(scores)
        zero = jnp.zeros((), BF16)
        parts = []
        for cc in range(nchunk):
            c = grp * nchunk + cc
            w = None
            for h in range(P_HEADS):
                e1 = e1_ref[h, pl.ds(c, 1), :].astype(BF16)
                cn = cnt_ref[h, pl.ds(c, 1), :].astype(BF16)
                term = jnp.where(r2_ref[h] < cn, e2_ref[h], zero) * e1
                w = term if w is None else w + term
            parts.append(w.astype(F32) * hid[cc * P_NKEYS:(cc + 1) * P_NKEYS, :])
        return jnp.concatenate(parts, axis=0).T.astype(BF16)

    @pl.when(e < last)
    def _():
        sa = _dot_nt(ua_ref[...], h_ref[...])
        y_ref[...] += _dot(pb_sc[...], va_ref[...])
        sb = _dot_nt(ub_ref[...], h_ref[...])
        pa_sc[...] = gated(sa, 2 * e)
        y_ref[...] += _dot(pa_sc[...], vb_ref[...])
        pb_sc[...] = gated(sb, 2 * e + 1)

    @pl.when(e == last)
    def _():
        y_ref[...] += _dot(pb_sc[...], va_ref[...])


def peer_main(h2, u_bf, v_bf, r2, cnt, e1, e2, l, tb):
    t = h2.shape[0]
    nstep = PEER_NGRP // 2 + 1
    gspec = pl.BlockSpec((P_HEADS, P_NKEYS, tb), lambda i, e: (0, 0, i))
    grp = lambda f: pl.BlockSpec((None, PEER_EC, D_MODEL), lambda i, e: (l, jnp.clip(f(e), 0, PEER_NGRP - 1), 0))
    return pl.pallas_call(
        _peer_main_kernel,
        grid=(t // tb, nstep),
        in_specs=[
            pl.BlockSpec((tb, D_MODEL), lambda i, e: (i, 0)),
            grp(lambda e: 2 * e), grp(lambda e: 2 * e + 1),
            grp(lambda e: 2 * e - 1), grp(lambda e: 2 * e),
            gspec, gspec, gspec, gspec,
        ],
        out_specs=pl.BlockSpec((tb, D_MODEL), lambda i, e: (i, 0)),
        out_shape=jax.ShapeDtypeStruct((t, D_MODEL), F32),
        scratch_shapes=[pltpu.VMEM((tb, PEER_EC), BF16), pltpu.VMEM((tb, PEER_EC), BF16)],
        compiler_params=_cp(("arbitrary", "arbitrary")),
        name="peer_main",
    )(h2, u_bf, u_bf, v_bf, v_bf, r2, cnt, e1, e2)


N_PAGES = PAST_LEN // PAGE_SIZE
N_SEG = PAST_LEN // CMP_STRIDE
PAGES_PER_STEP = 8
SEG_PER_STEP = PAGES_PER_STEP * PAGE_SIZE // CMP_STRIDE
N_PAST_BLK = PAST_LEN // SLC_BLK
SLC_LANES = 384
S_ROWS = 16


def _row_of(x, b):
    r = lax.broadcasted_iota(jnp.int32, x.shape, 0)
    return jnp.sum(jnp.where(r == b, x, 0.0), axis=0, keepdims=True)


def _sample_pool_kernel(pt_ref, *refs):
    pages = refs[:PAGES_PER_STEP]
    wf_ref, ws_ref, o_ref, carry_sc = refs[PAGES_PER_STEP:]
    pc = pl.program_id(2)

    @pl.when(pc == 0)
    def _():
        carry_sc[...] = jnp.zeros_like(carry_sc)

    wf = wf_ref[...]
    ws = ws_ref[...]
    fs, ss = [], []
    for r in pages:
        xs = r[...].reshape(PAGE_SIZE // CMP_STRIDE, CMP_STRIDE, 2, C_KV_HEADS, C_HD)
        fs.append(jnp.sum(xs * wf[None], axis=1))
        ss.append(jnp.sum(xs * ws[None], axis=1))
    first = jnp.concatenate(fs, axis=0)
    second = jnp.concatenate(ss, axis=0)
    o_ref[...] = jnp.concatenate([carry_sc[...], first[:SEG_PER_STEP - 1]], axis=0) + second
    carry_sc[...] = first[SEG_PER_STEP - 1:]


def sample_pool(cache, pt_flat, wf_all, ws_all):
    def page(k):
        return pl.BlockSpec((None, None, PAGE_SIZE, 2, C_KV_HEADS, C_HD),
                            lambda l, b, pc, pt: (l, pt[b * N_PAGES + pc * PAGES_PER_STEP + k], 0, 0, 0, 0))

    wspec = pl.BlockSpec((None, CMP_STRIDE, 2, C_KV_HEADS, C_HD), lambda l, b, pc, pt: (l, 0, 0, 0, 0))
    grid_spec = pltpu.PrefetchScalarGridSpec(
        num_scalar_prefetch=1,
        grid=(DEPTH, DEC_BATCH, N_PAGES // PAGES_PER_STEP),
        in_specs=[page(k) for k in range(PAGES_PER_STEP)] + [wspec, wspec],
        out_specs=pl.BlockSpec((None, None, SEG_PER_STEP, 2, C_KV_HEADS, C_HD), lambda l, b, pc, pt: (l, b, pc, 0, 0, 0)),
        scratch_shapes=[pltpu.VMEM((1, 2, C_KV_HEADS, C_HD), F32)],
    )
    return pl.pallas_call(
        _sample_pool_kernel,
        grid_spec=grid_spec,
        out_shape=jax.ShapeDtypeStruct((DEPTH, DEC_BATCH, N_SEG, 2, C_KV_HEADS, C_HD), F32),
        compiler_params=_cp(("arbitrary", "arbitrary", "arbitrary")),
        name="sample_pool",
    )(pt_flat, *([cache] * PAGES_PER_STEP), wf_all, ws_all)


def _sample_cmp_mlp_kernel(x_ref, w1_ref, b1_ref, w2_ref, b2_ref, o_ref):
    for kv in range(2):
        w1 = w1_ref[kv].astype(BF16)
        w2 = w2_ref[kv].astype(BF16)
        for g in range(C_KV_HEADS):
            hid = _gelu(_dot(x_ref[:, kv, g, :].astype(BF16), w1) + b1_ref[kv])
            o_ref[kv * C_KV_HEADS + g] = _dot(hid.astype(BF16), w2) + b2_ref[kv]


def sample_cmp_mlp(pooled, w1, b1, w2, b2):
    wspec = pl.BlockSpec((None, 2, C_HD, C_HD), lambda l, b: (l, 0, 0, 0))
    bspec = pl.BlockSpec((None, 2, 1, C_HD), lambda l, b: (l, 0, 0, 0))
    return pl.pallas_call(
        _sample_cmp_mlp_kernel,
        grid=(DEPTH, DEC_BATCH),
        in_specs=[pl.BlockSpec((None, None, N_SEG, 2, C_KV_HEADS, C_HD), lambda l, b: (l, b, 0, 0, 0, 0)),
                  wspec, bspec, wspec, bspec],
        out_specs=pl.BlockSpec((None, None, 2 * C_KV_HEADS, N_SEG, C_HD), lambda l, b: (l, b, 0, 0, 0)),
        out_shape=jax.ShapeDtypeStruct((DEPTH, DEC_BATCH, 2 * C_KV_HEADS, N_SEG, C_HD), F32),
        compiler_params=_cp(("arbitrary", "arbitrary")),
        name="sample_cmp_mlp",
    )(pooled, w1, b1, w2, b2)


def _sample_cmp_kernel(q_ref, ckv_ref, gate_ref, m5_ref, o_ref, idx_ref, ok_ref):
    b = pl.program_id(0)
    qrow = _row_of(q_ref[...].astype(F32), b)
    grow = _row_of(gate_ref[...], b)
    m_i = lax.broadcasted_iota(jnp.int32, (8, N_SEG), 1)
    valid = (m_i >= 1) & ((m_i - 1) * CMP_STRIDE + (CMP_LEN - 1) <= PAST_LEN)
    lane = lax.broadcasted_iota(jnp.int32, (1, Z2_W), 1)
    j = lax.broadcasted_iota(jnp.int32, (8, SLC_LANES), 1)
    jf = j.astype(F32)
    cur = PAST_LEN // SLC_BLK
    forced = (j == 0) | (j == cur) | (j == cur - 1)
    idx_row = jnp.zeros((1, Z2_W), F32)
    ok_row = jnp.zeros((1, Z2_W), F32)
    for g in range(C_KV_HEADS):
        ck = ckv_ref[g].astype(BF16)
        cv = ckv_ref[C_KV_HEADS + g].astype(BF16)
        imp = jnp.zeros((8, N_SEG), F32)
        for r in range(C_REP):
            h = g * C_REP + r
            q8 = jnp.broadcast_to(qrow[:, h * C_HD:(h + 1) * C_HD], (8, C_HD)).astype(BF16)
            p = _softmax_rows(_dot_nt(q8, ck) * (C_HD ** -0.5), valid)
            imp = imp + p
            o = _dot(p.astype(BF16), cv)
            gate = jnp.sum(jnp.where(lane == Z2_CG + h, grow, 0.0), axis=1, keepdims=True)
            o_ref[:, h * C_HD:(h + 1) * C_HD] = o[0:1] * _sigmoid(gate)
        score = _dot_exact_rhs(imp, m5_ref[...])
        score = jnp.where(j * SLC_BLK <= PAST_LEN, score + jnp.where(forced, FORCE_BONUS, 0.0), -3.0e38)
        for r in range(SLC_TOPK):
            m = jnp.max(score, axis=1, keepdims=True)
            idx = jnp.min(jnp.where(score == m, jf, 1.0e9), axis=1, keepdims=True)
            slot = lane == g * SLC_TOPK + r
            idx_row = jnp.where(slot, idx[0:1], idx_row)
            ok_row = jnp.where(slot, jnp.where(m[0:1] > 0.5 * NEG, 1.0, 0.0), ok_row)
            score = jnp.where(jf == idx, -3.4e38, score)
    idx_ref[...] = idx_row.astype(jnp.int32)
    ok_ref[...] = ok_row.astype(jnp.int32)


def sample_cmp(qn, ckv_all, z2, m5s, l):
    small = lambda n, dt: jax.ShapeDtypeStruct((DEC_BATCH, 1, n), dt)
    ospec = lambda n: pl.BlockSpec((None, 1, n), lambda b: (b, 0, 0))
    return pl.pallas_call(
        _sample_cmp_kernel,
        grid=(DEC_BATCH,),
        in_specs=[
            pl.BlockSpec((S_ROWS, C_WIDTH), lambda b: (0, 0)),
            pl.BlockSpec((None, None, 2 * C_KV_HEADS, N_SEG, C_HD), lambda b: (l, b, 0, 0, 0)),
            pl.BlockSpec((S_ROWS, Z2_W), lambda b: (0, 0)),
            pl.BlockSpec((N_SEG, SLC_LANES), lambda b: (0, 0)),
        ],
        out_specs=[ospec(C_WIDTH), ospec(Z2_W), ospec(Z2_W)],
        out_shape=[small(C_WIDTH, F32), small(Z2_W, jnp.int32), small(Z2_W, jnp.int32)],
        compiler_params=_cp(("arbitrary",)),
        name="sample_cmp",
    )(qn, ckv_all, z2, m5s)


def _sample_attend_kernel(phys_ref, rb_ref, idx_ref, ok_ref, *refs):
    nblk = C_KV_HEADS * SLC_TOPK
    blks = refs[:nblk]
    q_ref, kva_ref, kvb_ref, win_ref, gate_ref, oslc_ref, owin_ref, kcat_sc, vcat_sc = refs[nblk:]
    b = pl.program_id(0)
    scale = C_HD ** -0.5
    qall = _row_of(q_ref[...].astype(F32), b)
    kva_row = _row_of(kva_ref[...], b)
    kvb_row = _row_of(kvb_ref[...], b)
    grow = _row_of(gate_ref[...], b)
    lane_g = lax.broadcasted_iota(jnp.int32, grow.shape, 1)
    r8 = lax.broadcasted_iota(jnp.int32, (8, C_HD), 0)
    r1 = lax.broadcasted_iota(jnp.int32, (8, 1), 0)
    r64 = lax.broadcasted_iota(jnp.int32, (SLC_BLK, C_HD), 0)
    lane = lax.broadcasted_iota(jnp.int32, (8, SLC_TOPK * SLC_BLK), 1)
    kp = PAST_LEN - WINDOW + lax.broadcasted_iota(jnp.int32, (8, WINDOW), 1)
    validw = (kp <= PAST_LEN) & (kp > PAST_LEN - WINDOW)
    seg = lambda row, i: row[:, i * C_HD:(i + 1) * C_HD]
    for g in range(C_KV_HEADS):
        base = (b * C_KV_HEADS + g) * SLC_TOPK
        q4 = jnp.zeros((8, C_HD), F32)
        for r in range(C_REP):
            q4 = jnp.where(r8 == r, jnp.broadcast_to(seg(qall, g * C_REP + r), (8, C_HD)), q4)
        q4 = q4.astype(BF16)
        newk = jnp.where(r64 == 0, jnp.broadcast_to(seg(kva_row, 2 * C_KV_HEADS + g), (SLC_BLK, C_HD)), 0.0)
        newv = jnp.where(r64 == 0, jnp.broadcast_to(seg(kva_row, 3 * C_KV_HEADS + g), (SLC_BLK, C_HD)), 0.0)
        rowpos = lane % SLC_BLK
        okv = jnp.zeros(lane.shape, jnp.int32)
        for k in range(SLC_TOPK):
            idx_k = idx_ref[base + k]
            is_new = jnp.full((SLC_BLK, C_HD), idx_k, jnp.int32) >= N_PAST_BLK
            blk = blks[g * SLC_TOPK + k]
            kcat_sc[k * SLC_BLK:(k + 1) * SLC_BLK, :] = jnp.where(is_new, newk, blk[:, 0, g, :])
            vcat_sc[k * SLC_BLK:(k + 1) * SLC_BLK, :] = jnp.where(is_new, newv, blk[:, 1, g, :])
            in_k = lane // SLC_BLK == k
            rowpos = jnp.where(in_k, rowpos + idx_k * SLC_BLK, rowpos)
            okv = jnp.where(in_k, ok_ref[base + k], okv)
        s = _dot_nt(q4, kcat_sc[...].astype(BF16)) * scale
        p = _softmax_rows(s, (okv > 0) & (rowpos <= PAST_LEN))
        o_slc = _dot(p.astype(BF16), vcat_sc[...].astype(BF16))
        sw = jnp.where(validw, _dot_nt(q4, win_ref[:, 0, g, :].astype(BF16)) * scale, NEG)
        knew = seg(kvb_row, g).astype(BF16).astype(F32)
        vnew = seg(kvb_row, C_KV_HEADS + g)
        sn = jnp.sum(q4.astype(F32) * knew, axis=1, keepdims=True) * scale
        m = jnp.maximum(jnp.max(sw, axis=1, keepdims=True), sn)
        ew = jnp.where(validw, jnp.exp(sw - m), 0.0)
        en = jnp.exp(sn - m)
        den = jnp.maximum(jnp.sum(ew, axis=1, keepdims=True) + en, 1e-30)
        o_win = (_dot(ew.astype(BF16), win_ref[:, 1, g, :].astype(BF16)) + en * vnew) / den
        for branch, o, o_ref in ((1, o_slc, oslc_ref), (2, o_win, owin_ref)):
            gcol = jnp.zeros((8, 1), F32)
            for r in range(C_REP):
                col = Z2_CG + branch * C_HEADS + g * C_REP + r
                gcol = jnp.where(r1 == r, jnp.sum(jnp.where(lane_g == col, grow, 0.0), axis=1, keepdims=True), gcol)
            o_ref[g] = o * _sigmoid(gcol)


def sample_attend(cache, win_buf, phys, rb, idx, ok, qn, kva, kvb, z2, l):
    nblk = C_KV_HEADS * SLC_TOPK

    def cblk(j):
        return pl.BlockSpec((None, None, SLC_BLK, 2, C_KV_HEADS, C_HD),
                            lambda b, ph, rbr, ix, okr: (l, ph[b * nblk + j], rbr[b * nblk + j], 1, 0, 0))

    rows = lambda w: pl.BlockSpec((S_ROWS, w), lambda b, *_: (0, 0))
    ospec = pl.BlockSpec((None, C_KV_HEADS, 8, C_HD), lambda b, *_: (b, 0, 0, 0))
    grid_spec = pltpu.PrefetchScalarGridSpec(
        num_scalar_prefetch=4,
        grid=(DEC_BATCH,),
        in_specs=[cblk(j) for j in range(nblk)] + [
            rows(C_WIDTH), rows(KVA_W), rows(KVB_W),
            pl.BlockSpec((None, None, WINDOW, 2, C_KV_HEADS, C_HD), lambda b, *_: (l, b, 0, 0, 0, 0)),
            rows(Z2_W),
        ],
        out_specs=[ospec, ospec],
        scratch_shapes=[pltpu.VMEM((SLC_TOPK * SLC_BLK, C_HD), F32), pltpu.VMEM((SLC_TOPK * SLC_BLK, C_HD), F32)],
    )
    out = jax.ShapeDtypeStruct((DEC_BATCH, C_KV_HEADS, 8, C_HD), F32)
    return pl.pallas_call(
        _sample_attend_kernel,
        grid_spec=grid_spec,
        out_shape=[out, out],
        compiler_params=_cp(("arbitrary",)),
        name="sample_attend",
    )(phys, rb, idx, ok, *([cache] * nblk), qn, kva, kvb, win_buf, z2)


def _sample_ab_kernel(au_ref, av_ref, bq_ref, bk_ref, bv_ref, br_ref, z2_ref, lng_ref, lnb_ref, w00_ref, b0_ref,
                      gw_ref, gb_ref, on_ref, s0_ref, oa_ref, ob_ref, av_out_ref, s_ref):
    u = _gelu(au_ref[...].astype(F32))
    v = _gelu(av_ref[...].astype(F32))
    mu = jnp.mean(v, axis=-1, keepdims=True)
    vc = v - mu
    v = vc * lax.rsqrt(jnp.mean(vc * vc, axis=-1, keepdims=True) + EPS) * lng_ref[...] + lnb_ref[...]
    av_out_ref[...] = v
    oa_ref[...] = u * (v * w00_ref[...] + b0_ref[...])
    la = _log_sigmoid(_dot(z2_ref[...].astype(BF16), gw_ref[...]) + gb_ref[...]) * (1.0 / B_GATE_TAU)
    q = bq_ref[...].astype(F32) * (B_DK ** -0.5)
    k = bk_ref[...].astype(F32)
    vv = bv_ref[...].astype(F32)
    gate = br_ref[...].astype(F32)
    gate = gate * _sigmoid(gate)
    qd = q * jnp.exp(la)
    kd = k * jnp.exp(-la)
    on = on_ref[...]
    ob_ref[...] = jnp.zeros_like(ob_ref)
    for b in range(DEC_BATCH):
        for h in range(B_HEADS):
            dk = slice(h * B_DK, (h + 1) * B_DK)
            dv = slice(h * B_DV, (h + 1) * B_DV)
            qd_r = qd[b:b + 1, dk]
            v_r = vv[b:b + 1, dv]
            att = jnp.sum(qd_r * kd[b:b + 1, dk], axis=1, keepdims=True)
            s0 = s0_ref[b, h]
            o = att * v_r + _dot(jnp.broadcast_to(qd_r, (8, B_DK)).astype(BF16), s0.astype(BF16))[0:1]
            s_ref[b, h] = s0 * _col_from_row(jnp.exp(la[b:b + 1, dk])) + _col_from_row(k[b:b + 1, dk]) * v_r
            ob_ref[b:b + 1, dv] = _rms(o, on) * gate[b:b + 1, dv]


def sample_ab(z1, z2, lng, lnb, w00, b0, gw2p, gb, on_g, state_gla, l):
    blk = lambda w, off: pl.BlockSpec((S_ROWS, w), lambda i: (0, off // w))
    row = lambda n: jax.ShapeDtypeStruct((S_ROWS, n), F32)
    rspec = lambda n: pl.BlockSpec((S_ROWS, n), lambda i: (0, 0))
    sspec = pl.BlockSpec((None, DEC_BATCH, B_HEADS, B_DK, B_DV), lambda i: (l, 0, 0, 0, 0))
    return pl.pallas_call(
        _sample_ab_kernel,
        grid=(1,),
        in_specs=[
            blk(A_WIDTH, Z_AU), blk(A_WIDTH, Z_AV), blk(256, Z_BQ), blk(256, Z_BK), blk(B_WIDTH, Z_BV),
            blk(B_WIDTH, Z_BR), blk(Z2_W, 0),
            _lspec(l, 1, A_WIDTH), _lspec(l, 1, A_WIDTH), _lspec(l, 1, A_WIDTH), _lspec(l, 1, A_WIDTH),
            _lspec(l, Z2_W, 256), _lspec(l, 1, 256), _lspec(l, 1, B_DV),
            sspec,
        ],
        out_specs=[rspec(A_WIDTH), rspec(B_WIDTH), rspec(A_WIDTH),
                   pl.BlockSpec((DEC_BATCH, B_HEADS, B_DK, B_DV), lambda i: (0, 0, 0, 0))],
        out_shape=[row(A_WIDTH), row(B_WIDTH), row(A_WIDTH),
                   jax.ShapeDtypeStruct((DEC_BATCH, B_HEADS, B_DK, B_DV), F32)],
        compiler_params=_cp(("arbitrary",)),
        name="sample_ab",
    )(z1, z1, z1, z1, z1, z1, z2, lng, lnb, w00, b0, gw2p, gb, on_g, state_gla)


def _m5s_table():
    m5 = np.zeros((N_SEG, SLC_LANES), np.float32)
    for j in range(N_PAST_BLK + 1):
        for m in range(max(4 * j, 1), min(4 * j + 4, N_SEG - 1) + 1):
            m5[m, j] = 1.0
    return m5


def _rope_tables(pos):
    half = C_ROT // 2
    inv = jnp.float32(ROPE_THETA) ** (-jnp.arange(half, dtype=F32) / half)
    ang = pos.astype(F32)[:, None] * inv[None, :]
    cos = jnp.cos(ang)
    sin = jnp.sin(ang)
    n = pos.shape[0]
    ones = jnp.ones((n, C_HD - C_ROT), F32)
    zeros = jnp.zeros((n, C_HD - half), F32)
    c = jnp.concatenate([cos, cos, ones], axis=1)
    s1 = jnp.concatenate([-sin, zeros], axis=1)
    s2 = jnp.concatenate([jnp.zeros((n, half), F32), sin, jnp.zeros((n, C_HD - C_ROT), F32)], axis=1)
    return c, s1, s2


def _m5_table():
    m5 = np.zeros((N_CMP, N_CMP), np.float32)
    for j in range(SEQ // SLC_BLK):
        for n in range(4 * j - 1, 4 * j + 4):
            if 0 <= n < N_CMP - 1:
                m5[n, j] = 1.0
    return m5


def _permute_w_in(w_in):
    o = np.cumsum([0, 512, 512, 256, 256, 512, 16, 512, 1024, 1536, 24, 6144])
    seg = lambda i: w_in[..., o[i]:o[i + 1]]
    w1 = jnp.concatenate([seg(7), seg(0), seg(1), seg(4), seg(6), seg(8), seg(2), seg(3), seg(10)], axis=-1)
    pad = jnp.zeros(w_in.shape[:-1] + (Z2_W - 40,), w_in.dtype)
    w2 = jnp.concatenate([seg(5), seg(9), pad], axis=-1)
    return w1.astype(BF16), w2.astype(BF16)


def kernel(x_prompt, x_sample, cache_nsa_kv, state_win_kv, state_gla, page_table, c_prompt, c_sample, ada_w, ada_b, norm1_g, norm2_g, w_in, a_ln_g, a_ln_b, a_ws, a_bs, b_gw2, b_gb, b_on_g, c_qn_g, c_kn_g, cmp_pool, cmp_w1, cmp_b1, cmp_w2, cmp_b2, w_br_a, w_br_b, w_br_c, w_out, p_wq, p_qn_g, p_subkeys, p_u, p_v):
    tm_p, tm_s = 512, T_S
    w1_all, w2_all = _permute_w_in(w_in)
    wa_all = w_br_a.astype(BF16)
    wb_all = w_br_b.astype(BF16)
    wc_all = w_br_c.astype(BF16)
    wo_all = w_out.astype(BF16)
    wq_all = p_wq.astype(BF16)
    u_all = p_u.astype(BF16)
    v_all = p_v.astype(BF16)
    row3 = lambda a: a.reshape(DEPTH, 1, a.shape[-1])
    g1_all, g2_all = row3(norm1_g), row3(norm2_g)
    lng_all, lnb_all = row3(a_ln_g), row3(a_ln_b)
    bst_all = jnp.swapaxes(a_bs, 1, 2)
    gw2p_all = jnp.pad(b_gw2, ((0, 0), (0, Z2_W - B_GATE_RANK), (0, 0))).astype(BF16)
    gb_all, on_all = row3(b_gb), row3(b_on_g)
    gq_all, pqg_all = row3(c_qn_g), row3(p_qn_g)
    cb1_all = cmp_b1.reshape(DEPTH, 2, 1, C_HD)
    cb2_all = cmp_b2.reshape(DEPTH, 2, 1, C_HD)
    c40 = jnp.concatenate([jnp.repeat(c_prompt, N_SEQ_ROWS, axis=0), c_sample], axis=0)
    mod = adaln_table(c40, ada_w, ada_b)
    rope_p = _rope_tables(jnp.arange(SEQ, dtype=jnp.int32))
    rope_s = _rope_tables(jnp.full((T_S,), PAST_LEN, jnp.int32))
    m5 = jnp.asarray(_m5_table().T, BF16)
    xp = x_prompt.reshape(T_P, D_MODEL)
    xs = jnp.pad(x_sample.reshape(DEC_BATCH, D_MODEL), ((0, T_S - DEC_BATCH), (0, 0)))
    pool_w = lambda p: jnp.broadcast_to(jnp.swapaxes(p, 1, 2)[:, :, :, None, None], (DEPTH, CMP_STRIDE, 2, C_KV_HEADS, C_HD))
    wf_all = pool_w(cmp_pool[:, :, :CMP_STRIDE])
    ws_all = pool_w(cmp_pool[:, :, CMP_STRIDE:])
    w00_all = jnp.repeat(a_ws[:, :, 0, 0], A_GW, axis=1)[:, None, :]
    b0_all = jnp.repeat(a_bs[:, :, 0], A_GW, axis=1)[:, None, :]
    m5s = jnp.asarray(_m5s_table(), BF16)
    ckv_s = sample_cmp_mlp(sample_pool(cache_nsa_kv, page_table.reshape(-1), wf_all, ws_all), cmp_w1, cb1_all, cmp_w2, cb2_all)

    outs = {k: [] for k in ('kv_p', 'kv_s', 'win_p', 'win_s', 'gla_p', 'gla_s', 'cv_s')}
    yp = ys = None
    for l in range(DEPTH):
        streams = []
        for is_s, x, y, tm in ((False, xp, yp, tm_p), (True, xs, ys, tm_s)):
            if l == 0:
                h = norm_mod(x, g1_all, mod, l, 0, 1, tm, is_s)
            else:
                x, h = resid_norm_mod(x, y, g1_all, mod, l - 1, l, 5, 0, 1, tm, is_s)
            z1 = matmul(h, w1_all, l, tm, 1024, BF16, "w_in")
            z2 = matmul(h, w2_all, l, tm, Z2_W, F32, "w_in_gates")
            rope = rope_s if is_s else rope_p
            qn, kva, kvb = qk_prep(z1, gq_all, c_kn_g, l, rope[0], rope[1], rope[2], tm, is_s)
            streams.append((x, z1, z2, qn, kva, kvb))
        x, z1, z2, qn, kva, kvb = streams[0]
        o_a = chunk_mlp_prompt(z1, lng_all, lnb_all, a_ws, bst_all, l)
        o_b, s_p = gla_prompt(z1, z2, gw2p_all, gb_all, on_all, l)
        ckv = compress_prompt(kva, cmp_pool, cmp_w1, cb1_all, cmp_w2, cb2_all, l)
        o_cmp, selt = cmp_select_prompt(qn, ckv, z2, m5)
        o_slc = flash_prompt(qn, kva, selt, z2, 1)
        o_win = flash_prompt(qn, kvb, selt, z2, 2)
        mixed_p = (o_a, o_b, o_cmp, o_slc, o_win)
        outs['kv_p'].append(kva.reshape(BATCH, SEQ, 4, C_KV_HEADS, C_HD))
        outs['win_p'].append(kvb.reshape(BATCH, SEQ, 2, C_KV_HEADS, C_HD)[:, SEQ - WINDOW:])
        outs['gla_p'].append(s_p)
        xs_, z1s, z2s, qns, kvas, kvbs = streams[1]
        oa_s, ob_s, av_s, s_s = sample_ab(z1s, z2s, lng_all, lnb_all, w00_all, b0_all, gw2p_all, gb_all, on_all, state_gla, l)
        ocmp_s, idx, ok = sample_cmp(qns, ckv_s, z2s, m5s, l)
        nsel = C_KV_HEADS * SLC_TOPK
        idx_f = idx[:, 0, :nsel].reshape(-1)
        ok_f = ok[:, 0, :nsel].reshape(-1)
        jc = jnp.minimum(idx_f, N_PAST_BLK - 1)
        bpp = PAGE_SIZE // SLC_BLK
        phys = page_table[jnp.repeat(jnp.arange(DEC_BATCH), nsel), jc // bpp]
        oslc_s, owin_s = sample_attend(cache_nsa_kv, state_win_kv, phys, jc % bpp, idx_f, ok_f, qns, kvas, kvbs, z2s, l)
        pad_s = lambda a: jnp.pad(a, ((0, T_S - a.shape[0]), (0, 0))).astype(BF16)
        heads = lambda o: o[:, :, :C_REP].reshape(DEC_BATCH, C_WIDTH)
        mixed_s = (pad_s(oa_s), pad_s(ob_s), pad_s(ocmp_s.reshape(DEC_BATCH, C_WIDTH)), pad_s(heads(oslc_s)), pad_s(heads(owin_s)))
        outs['kv_s'].append(kvas[:DEC_BATCH].reshape(DEC_BATCH, 1, 4, C_KV_HEADS, C_HD))
        outs['win_s'].append(kvbs[:DEC_BATCH].reshape(DEC_BATCH, 1, 2, C_KV_HEADS, C_HD))
        outs['gla_s'].append(s_s)
        outs['cv_s'].append(av_s[:DEC_BATCH, None, :])
        new = []
        for is_s, st, mixed, tm in ((False, streams[0], mixed_p, tm_p), (True, streams[1], mixed_s, tm_s)):
            x, z1 = st[0], st[1]
            mg = merge(*mixed, z1, wa_all, wb_all, wc_all, l, tm)
            att = matmul(mg, wo_all, l, tm, 1024, F32, "w_out")
            x1, h2 = resid_norm_mod(x, att, g2_all, mod, l, l, 2, 3, 4, tm, is_s)
            pq = peer_query(h2, wq_all, pqg_all, l, tm)
            r2, cnt, e1, e2 = peer_topk(pq, p_subkeys, l, tm)
            y = peer_main(h2, u_all, v_all, r2, cnt, e1, e2, l, tm)
            new.append((x1, y))
        (xp, yp), (xs, ys) = new
    xp = resid(xp, yp, mod, DEPTH - 1, 5, tm_p, False)
    xs = resid(xs, ys, mod, DEPTH - 1, 5, tm_s, True)
    return (
        xp.reshape(BATCH, SEQ, D_MODEL),
        xs[:DEC_BATCH].reshape(DEC_BATCH, 1, D_MODEL),
        jnp.stack(outs['kv_p']),
        jnp.stack(outs['kv_s']),
        jnp.stack(outs['win_p']),
        jnp.stack(outs['win_s']),
        jnp.stack(outs['gla_p']),
        jnp.stack(outs['gla_s']),
        jnp.stack(outs['cv_s']),
    )
```

```python
import functools

import numpy as np
import jax
import jax.numpy as jnp
from jax import lax
from jax.experimental import pallas as pl
from jax.experimental.pallas import tpu as pltpu

F32 = jnp.float32
BF16 = jnp.bfloat16

D_MODEL = 2048
BATCH = 4
SEQ = 2048
DEPTH = 4
DEC_BATCH = 8
PAST_LEN = 16384
PAGE_SIZE = 128
EPS = 1e-6
NEG = -1.0e30
A_WIDTH = 512
A_GROUPS = 4
A_GW = 128
A_CHUNK = 128
B_HEADS = 4
B_WIDTH = 512
B_DV = 128
B_DK = 64
B_GATE_RANK = 16
B_GATE_TAU = 16.0
B_CHUNK = 64
C_HEADS = 8
C_KV_HEADS = 2
C_HD = 128
C_WIDTH = 1024
C_REP = 4
C_ROT = 32
ROPE_THETA = 500000.0
CMP_LEN = 32
CMP_STRIDE = 16
SLC_BLK = 64
SLC_TOPK = 16
FORCE_BONUS = 1.0e4
WINDOW = 512
P_HEADS = 8
P_NKEYS = 128
P_NEXP = P_NKEYS * P_NKEYS
P_HALF = 128
P_TOPK = 16

T_P = BATCH * SEQ
T_S = 128
N_SEQ_ROWS = 8
N_CMP = SEQ // CMP_STRIDE

Z_CQ = 0
Z_AU = 1024
Z_AV = 1536
Z_BV = 2048
Z_BR = 2560
Z_CKV = 3072
Z_BQ = 4608
Z_BK = 4864
Z_MG = 5120
Z1_W = 11264
Z2_W = 128
Z2_CG = 16
KVA_W = 1024
KVB_W = 512

VMEM_LIMIT = 56 * 1024 * 1024


def _cp(sem, vmem=VMEM_LIMIT):
    return pltpu.CompilerParams(dimension_semantics=sem, vmem_limit_bytes=vmem)


def _lspec(l, *dims):
    nd = len(dims)
    return pl.BlockSpec((None,) + tuple(dims), lambda *_: (l,) + (0,) * nd)


def _gelu(x):
    return 0.5 * x * (1.0 + jnp.tanh(0.7978845608028654 * (x + 0.044715 * x * x * x)))


def _sigmoid(x):
    return 0.5 * jnp.tanh(0.5 * x) + 0.5


def _log_sigmoid(x):
    return jnp.minimum(x, 0.0) - jnp.log(1.0 + jnp.exp(-jnp.abs(x)))


def _dot(a, b):
    return jnp.dot(a, b, preferred_element_type=F32)


def _dot_nt(a, b):
    return lax.dot_general(a, b, (((1,), (1,)), ((), ())), preferred_element_type=F32)


def _split3(v):
    hi = v.astype(BF16)
    r1 = v - hi.astype(F32)
    mid = r1.astype(BF16)
    r2 = r1 - mid.astype(F32)
    lo = r2.astype(BF16)
    return hi, mid, lo


def _dot_exact_lhs(sel_bf16, v):
    hi, mid, lo = _split3(v)
    return (_dot(sel_bf16, hi) + _dot(sel_bf16, mid)) + _dot(sel_bf16, lo)


def _dot_exact_rhs(v, sel_bf16):
    hi, mid, lo = _split3(v)
    return (_dot(hi, sel_bf16) + _dot(mid, sel_bf16)) + _dot(lo, sel_bf16)


def _rms(x, g):
    return x * lax.rsqrt(jnp.mean(x * x, axis=-1, keepdims=True) + EPS) * g


def _mod_rows(y, scale8, shift8=None):
    tm, n = y.shape
    y3 = y.reshape(tm // N_SEQ_ROWS, N_SEQ_ROWS, n) * scale8[None]
    if shift8 is not None:
        y3 = y3 + shift8[None]
    return y3.reshape(tm, n)


def _gate_col(gates, col):
    lane = lax.broadcasted_iota(jnp.int32, gates.shape, 1)
    return jnp.sum(jnp.where(lane == col, gates, 0.0), axis=1, keepdims=True)


def _adaln_kernel(c_ref, w_ref, b_ref, o_ref):
    c = c_ref[...]
    s = (c * _sigmoid(c)).astype(BF16)
    o_ref[...] = _dot(s, w_ref[...].astype(BF16)) + b_ref[...]


def adaln_table(c40, ada_w, ada_b):
    tn = 1024
    n = 6 * D_MODEL
    return pl.pallas_call(
        _adaln_kernel,
        grid=(DEPTH, n // tn),
        in_specs=[
            pl.BlockSpec((5 * N_SEQ_ROWS, D_MODEL), lambda l, j: (0, 0)),
            pl.BlockSpec((None, D_MODEL, tn), lambda l, j: (l, 0, j)),
            pl.BlockSpec((None, 1, tn), lambda l, j: (l, 0, j)),
        ],
        out_specs=pl.BlockSpec((None, 5 * N_SEQ_ROWS, tn), lambda l, j: (l, 0, j)),
        out_shape=jax.ShapeDtypeStruct((DEPTH, 5 * N_SEQ_ROWS, n), F32),
        compiler_params=_cp(("arbitrary", "arbitrary")),
        name="adaln_table",
    )(c40, ada_w, ada_b.reshape(DEPTH, 1, n))


def _seq_of(tm, is_sample):
    if is_sample:
        return lambda i: 4
    per = SEQ // tm
    return lambda i: i // per


def _mod_spec(l, sq, k):
    return pl.BlockSpec((None, N_SEQ_ROWS, D_MODEL), lambda i: (l, sq(i), k))


def _norm_mod_kernel(x_ref, g_ref, sh_ref, sc_ref, h_ref):
    y = _rms(x_ref[...], g_ref[...])
    h_ref[...] = _mod_rows(y, 1.0 + sc_ref[...], sh_ref[...]).astype(h_ref.dtype)


def norm_mod(x, g, mod, l, k_shift, k_scale, tm, is_sample):
    t = x.shape[0]
    sq = _seq_of(tm, is_sample)
    return pl.pallas_call(
        _norm_mod_kernel,
        grid=(t // tm,),
        in_specs=[
            pl.BlockSpec((tm, D_MODEL), lambda i: (i, 0)),
            _lspec(l, 1, D_MODEL),
            _mod_spec(l, sq, k_shift),
            _mod_spec(l, sq, k_scale),
        ],
        out_specs=pl.BlockSpec((tm, D_MODEL), lambda i: (i, 0)),
        out_shape=jax.ShapeDtypeStruct((t, D_MODEL), BF16),
        compiler_params=_cp(("arbitrary",)),
        name="norm_mod",
    )(x, g, mod, mod)


def _resid_norm_mod_kernel(x_ref, y_ref, gate_ref, g_ref, sh_ref, sc_ref, xo_ref, h_ref):
    xn = x_ref[...] + _mod_rows(y_ref[...], gate_ref[...])
    xo_ref[...] = xn
    h_ref[...] = _mod_rows(_rms(xn, g_ref[...]), 1.0 + sc_ref[...], sh_ref[...]).astype(h_ref.dtype)


def resid_norm_mod(x, y, g, mod, l_gate, l, k_gate, k_shift, k_scale, tm, is_sample):
    t = x.shape[0]
    sq = _seq_of(tm, is_sample)
    row = pl.BlockSpec((tm, D_MODEL), lambda i: (i, 0))
    return pl.pallas_call(
        _resid_norm_mod_kernel,
        grid=(t // tm,),
        in_specs=[
            row,
            row,
            _mod_spec(l_gate, sq, k_gate),
            _lspec(l, 1, D_MODEL),
            _mod_spec(l, sq, k_shift),
            _mod_spec(l, sq, k_scale),
        ],
        out_specs=[row, row],
        out_shape=[jax.ShapeDtypeStruct((t, D_MODEL), F32), jax.ShapeDtypeStruct((t, D_MODEL), BF16)],
        compiler_params=_cp(("arbitrary",)),
        name="resid_norm_mod",
    )(x, y, mod, g, mod, mod)


def _resid_kernel(x_ref, y_ref, gate_ref, xo_ref):
    xo_ref[...] = x_ref[...] + _mod_rows(y_ref[...], gate_ref[...])


def resid(x, y, mod, l, k_gate, tm, is_sample):
    t = x.shape[0]
    sq = _seq_of(tm, is_sample)
    row = pl.BlockSpec((tm, D_MODEL), lambda i: (i, 0))
    return pl.pallas_call(
        _resid_kernel,
        grid=(t // tm,),
        in_specs=[row, row, _mod_spec(l, sq, k_gate)],
        out_specs=row,
        out_shape=jax.ShapeDtypeStruct((t, D_MODEL), F32),
        compiler_params=_cp(("arbitrary",)),
        name="resid",
    )(x, y, mod)


def _mm_kernel(x_ref, w_ref, o_ref):
    o_ref[...] = _dot(x_ref[...], w_ref[...]).astype(o_ref.dtype)


def matmul(x, w, l, tm, tn, out_dtype, name):
    m, k = x.shape
    n = w.shape[2]
    return pl.pallas_call(
        _mm_kernel,
        grid=(n // tn, m // tm),
        in_specs=[
            pl.BlockSpec((tm, k), lambda j, i: (i, 0)),
            pl.BlockSpec((None, k, tn), lambda j, i: (l, 0, j)),
        ],
        out_specs=pl.BlockSpec((tm, tn), lambda j, i: (i, j)),
        out_shape=jax.ShapeDtypeStruct((m, n), out_dtype),
        compiler_params=_cp(("arbitrary", "arbitrary")),
        name=name,
    )(x, w)


def _wq_kernel(x_ref, w_ref, g_ref, o_ref):
    acc = _dot(x_ref[...], w_ref[...])
    g = g_ref[...]
    for c in range(acc.shape[1] // P_HALF):
        a = acc[:, c * P_HALF:(c + 1) * P_HALF]
        o_ref[:, c * P_HALF:(c + 1) * P_HALF] = _rms(a, g).astype(o_ref.dtype)


def peer_query(h2, wq, qn_g, l, tm):
    m = h2.shape[0]
    tn = 512
    return pl.pallas_call(
        _wq_kernel,
        grid=(D_MODEL // tn, m // tm),
        in_specs=[
            pl.BlockSpec((tm, D_MODEL), lambda j, i: (i, 0)),
            pl.BlockSpec((None, D_MODEL, tn), lambda j, i: (l, 0, j)),
            _lspec(l, 1, P_HALF),
        ],
        out_specs=pl.BlockSpec((tm, tn), lambda j, i: (i, j)),
        out_shape=jax.ShapeDtypeStruct((m, D_MODEL), BF16),
        compiler_params=_cp(("arbitrary", "arbitrary")),
        name="peer_query",
    )(h2, wq, qn_g)


def _qk_prep_kernel(zq_ref, zkv_ref, gq_ref, gk_ref, c_ref, s1_ref, s2_ref, qn_ref, kva_ref, kvb_ref):
    cos = c_ref[...]
    s1 = s1_ref[...]
    s2 = s2_ref[...]

    def norm_rope(x, g):
        xn = _rms(x, g)
        return xn * cos + pltpu.roll(xn, C_HD - C_ROT // 2, 1) * s1 + pltpu.roll(xn, C_ROT // 2, 1) * s2

    gq = gq_ref[...]
    for h in range(C_HEADS):
        x = zq_ref[:, h * C_HD:(h + 1) * C_HD].astype(F32)
        qn_ref[:, h * C_HD:(h + 1) * C_HD] = norm_rope(x, gq).astype(qn_ref.dtype)
    for s in range(6):
        for g in range(C_KV_HEADS):
            c0 = (s * C_KV_HEADS + g) * C_HD
            x = zkv_ref[:, c0:c0 + C_HD].astype(F32)
            if s % 2 == 0:
                x = norm_rope(x, gk_ref[s // 2:s // 2 + 1, :])
            if c0 < KVA_W:
                kva_ref[:, c0:c0 + C_HD] = x
            else:
                kvb_ref[:, c0 - KVA_W:c0 - KVA_W + C_HD] = x


def qk_prep(z1, gq, gk, l, rope_c, rope_s1, rope_s2, tm, is_sample):
    t = z1.shape[0]
    per = 1 if is_sample else SEQ // tm
    tab = pl.BlockSpec((tm, C_HD), lambda i: (i % per, 0))
    return pl.pallas_call(
        _qk_prep_kernel,
        grid=(t // tm,),
        in_specs=[
            pl.BlockSpec((tm, C_WIDTH), lambda i: (i, Z_CQ // C_WIDTH)),
            pl.BlockSpec((tm, 1536), lambda i: (i, Z_CKV // 1536)),
            _lspec(l, 1, C_HD),
            _lspec(l, 3, C_HD),
            tab, tab, tab,
        ],
        out_specs=[
            pl.BlockSpec((tm, C_WIDTH), lambda i: (i, 0)),
            pl.BlockSpec((tm, KVA_W), lambda i: (i, 0)),
            pl.BlockSpec((tm, KVB_W), lambda i: (i, 0)),
        ],
        out_shape=[
            jax.ShapeDtypeStruct((t, C_WIDTH), BF16),
            jax.ShapeDtypeStruct((t, KVA_W), F32),
            jax.ShapeDtypeStruct((t, KVB_W), F32),
        ],
        compiler_params=_cp(("arbitrary",)),
        name="qk_prep",
    )(z1, z1, gq, gk, rope_c, rope_s1, rope_s2)


def _chunk_mlp_kernel(au_ref, av_ref, lng_ref, lnb_ref, ws_ref, bst_ref, o_ref):
    u = _gelu(au_ref[...].astype(F32))
    v = _gelu(av_ref[...].astype(F32))
    mu = jnp.mean(v, axis=-1, keepdims=True)
    vc = v - mu
    var = jnp.mean(vc * vc, axis=-1, keepdims=True)
    v = vc * lax.rsqrt(var + EPS) * lng_ref[...] + lnb_ref[...]
    r = lax.broadcasted_iota(jnp.int32, (A_CHUNK, A_CHUNK), 0)
    c = lax.broadcasted_iota(jnp.int32, (A_CHUNK, A_CHUNK), 1)
    tril = r >= c
    tm = u.shape[0]
    for g in range(A_GROUPS):
        w = jnp.where(tril, ws_ref[g], 0.0).astype(BF16)
        bias = bst_ref[:, g:g + 1]
        for ch in range(tm // A_CHUNK):
            rows = slice(ch * A_CHUNK, (ch + 1) * A_CHUNK)
            cols = slice(g * A_GW, (g + 1) * A_GW)
            mixed = _dot(w, v[rows, cols].astype(BF16)) + bias
            o_ref[rows, cols] = (u[rows, cols] * mixed).astype(o_ref.dtype)


def chunk_mlp_prompt(z1, ln_g, ln_b, ws, bs_t, l):
    tm = 512
    return pl.pallas_call(
        _chunk_mlp_kernel,
        grid=(T_P // tm,),
        in_specs=[
            pl.BlockSpec((tm, A_WIDTH), lambda i: (i, Z_AU // A_WIDTH)),
            pl.BlockSpec((tm, A_WIDTH), lambda i: (i, Z_AV // A_WIDTH)),
            _lspec(l, 1, A_WIDTH),
            _lspec(l, 1, A_WIDTH),
            _lspec(l, A_GROUPS, A_CHUNK, A_CHUNK),
            _lspec(l, A_CHUNK, A_GROUPS),
        ],
        out_specs=pl.BlockSpec((tm, A_WIDTH), lambda i: (i, 0)),
        out_shape=jax.ShapeDtypeStruct((T_P, A_WIDTH), BF16),
        compiler_params=_cp(("arbitrary",)),
        name="chunk_mlp",
    )(z1, z1, ln_g, ln_b, ws, bs_t)


def _col_from_row(row):
    n = row.shape[1]
    r = lax.broadcasted_iota(jnp.int32, (n, n), 0)
    c = lax.broadcasted_iota(jnp.int32, (n, n), 1)
    return jnp.sum(jnp.where(r == c, jnp.broadcast_to(row, (n, n)), 0.0), axis=1, keepdims=True)


def _gla_kernel(q_ref, k_ref, v_ref, r_ref, bg_ref, gw_ref, gb_ref, on_ref, o_ref, s_out_ref, s_sc):
    n = pl.program_id(0)

    @pl.when(n == 0)
    def _():
        s_sc[...] = jnp.zeros_like(s_sc)

    c = B_CHUNK
    ri = lax.broadcasted_iota(jnp.int32, (c, c), 0)
    ci = lax.broadcasted_iota(jnp.int32, (c, c), 1)
    tril = ri >= ci
    eye = jnp.where(ri == ci, 1.0, 0.0).astype(BF16)
    ones_tril = jnp.where(tril, 1.0, 0.0).astype(BF16)
    on = on_ref[...]
    for sq in range(BATCH):
        la = _log_sigmoid(_dot(bg_ref[sq].astype(BF16), gw_ref[...]) + gb_ref[...]) * (1.0 / B_GATE_TAU)
        bc = _dot_exact_lhs(ones_tril, la)
        q = q_ref[sq].astype(F32) * (B_DK ** -0.5)
        k = k_ref[sq].astype(F32)
        v = v_ref[sq].astype(BF16)
        gate = r_ref[sq].astype(F32)
        gate = gate * _sigmoid(gate)
        for h in range(B_HEADS):
            dk = slice(h * B_DK, (h + 1) * B_DK)
            dv = slice(h * B_DV, (h + 1) * B_DV)
            b = bc[:, dk]
            qd = (q[:, dk] * jnp.exp(b)).astype(BF16)
            kd = (k[:, dk] * jnp.exp(-b)).astype(BF16)
            att = jnp.where(tril, _dot_nt(qd, kd), 0.0)
            s_old = s_sc[sq, h]
            o = _dot(att.astype(BF16), v[:, dv]) + _dot(qd, s_old.astype(BF16))
            blast = b[c - 1:c, :]
            kdec = (k[:, dk] * jnp.exp(blast - b)).astype(BF16)
            kdec_t = _dot_nt(eye, kdec).astype(BF16)
            s_sc[sq, h] = s_old * _col_from_row(jnp.exp(blast)) + _dot(kdec_t, v[:, dv])
            o_ref[sq, :, dv] = (_rms(o, on) * gate[:, dv]).astype(o_ref.dtype)

    @pl.when(n == pl.num_programs(0) - 1)
    def _():
        s_out_ref[...] = s_sc[...]


def gla_prompt(z1, z2, gw2p, gb, on_g, l):
    c = B_CHUNK
    z1b = z1.reshape(BATCH, SEQ, Z1_W)
    z2b = z2.reshape(BATCH, SEQ, Z2_W)
    blk = lambda w, off: pl.BlockSpec((BATCH, c, w), lambda n: (0, n, off // w))
    state = jax.ShapeDtypeStruct((BATCH, B_HEADS, B_DK, B_DV), F32)
    o_b, s_p = pl.pallas_call(
        _gla_kernel,
        grid=(SEQ // c,),
        in_specs=[
            blk(256, Z_BQ), blk(256, Z_BK), blk(B_WIDTH, Z_BV), blk(B_WIDTH, Z_BR), blk(Z2_W, 0),
            _lspec(l, Z2_W, 256),
            _lspec(l, 1, 256),
            _lspec(l, 1, B_DV),
        ],
        out_specs=[
            pl.BlockSpec((BATCH, c, B_WIDTH), lambda n: (0, n, 0)),
            pl.BlockSpec((BATCH, B_HEADS, B_DK, B_DV), lambda n: (0, 0, 0, 0)),
        ],
        out_shape=[jax.ShapeDtypeStruct((BATCH, SEQ, B_WIDTH), BF16), state],
        scratch_shapes=[pltpu.VMEM((BATCH, B_HEADS, B_DK, B_DV), F32)],
        compiler_params=_cp(("arbitrary",)),
        name="gla",
    )(z1b, z1b, z1b, z1b, z2b, gw2p, gb, on_g)
    return o_b.reshape(T_P, B_WIDTH), s_p


def _compress_kernel(pool_ref, rows_ref, w1_ref, b1_ref, w2_ref, b2_ref, o_ref, *, l):
    kv = pl.program_id(2)
    r = lax.broadcasted_iota(jnp.int32, (N_CMP, SEQ), 0)
    c = lax.broadcasted_iota(jnp.int32, (N_CMP, SEQ), 1)
    d = c - CMP_STRIDE * r
    p = jnp.zeros((N_CMP, SEQ), F32)
    for i in range(CMP_LEN):
        p = jnp.where(d == i, pool_ref[l, kv, i], p)
    rows = rows_ref[...]
    ph, pm, plo = _split3(p)
    rh, rm, rl = _split3(rows)
    pooled = (_dot(ph, rh) + (_dot(ph, rm) + _dot(pm, rh))) + ((_dot(pm, rm) + _dot(ph, rl)) + _dot(plo, rh))
    hid = _gelu(_dot(pooled.astype(BF16), w1_ref[...].astype(BF16)) + b1_ref[...])
    o_ref[...] = _dot(hid.astype(BF16), w2_ref[...].astype(BF16)) + b2_ref[...]


def compress_prompt(kva, pool, w1, b1, w2, b2, l):
    wspec = pl.BlockSpec((None, None, C_HD, C_HD), lambda b, g, kv: (l, kv, 0, 0))
    bspec = pl.BlockSpec((None, None, 1, C_HD), lambda b, g, kv: (l, kv, 0, 0))
    return pl.pallas_call(
        functools.partial(_compress_kernel, l=l),
        grid=(BATCH, C_KV_HEADS, 2),
        in_specs=[
            pl.BlockSpec(memory_space=pltpu.SMEM),
            pl.BlockSpec((SEQ, C_HD), lambda b, g, kv: (b, kv * C_KV_HEADS + g)),
            wspec, bspec, wspec, bspec,
        ],
        out_specs=pl.BlockSpec((None, None, None, N_CMP, C_HD), lambda b, g, kv: (b, g, kv, 0, 0)),
        out_shape=jax.ShapeDtypeStruct((BATCH, C_KV_HEADS, 2, N_CMP, C_HD), F32),
        compiler_params=_cp(("arbitrary", "arbitrary", "arbitrary")),
        name="compress",
    )(pool, kva, w1, b1, w2, b2)


def _softmax_rows(s, valid):
    s = jnp.where(valid, s, NEG)
    m = jnp.max(s, axis=-1, keepdims=True)
    e = jnp.where(valid, jnp.exp(s - m), 0.0)
    return e / jnp.maximum(jnp.sum(e, axis=-1, keepdims=True), 1e-30)


def _cmp_select_kernel(q_ref, ckv_ref, gate_ref, m5_ref, o_ref, selt_ref):
    g = pl.program_id(1)
    qi = pl.program_id(2)
    tq = q_ref.shape[0]
    ck = ckv_ref[0].astype(BF16)
    cv = ckv_ref[1].astype(BF16)
    pos = qi * tq + lax.broadcasted_iota(jnp.int32, (tq, N_CMP), 0)
    j = lax.broadcasted_iota(jnp.int32, (tq, N_CMP), 1)
    valid = j * CMP_STRIDE + (CMP_LEN - 1) <= pos
    gates = gate_ref[...]
    imp = jnp.zeros((tq, N_CMP), F32)
    for r in range(C_REP):
        q = q_ref[:, r * C_HD:(r + 1) * C_HD]
        p = _softmax_rows(_dot_nt(q, ck) * (C_HD ** -0.5), valid)
        imp = imp + p
        o = _dot(p.astype(BF16), cv)
        gcol = _gate_col(gates, Z2_CG + g * C_REP + r)
        o_ref[:, r * C_HD:(r + 1) * C_HD] = (o * _sigmoid(gcol)).astype(o_ref.dtype)
    nslc = SEQ // SLC_BLK
    hi, mid, lo = _split3(imp)
    m5t = m5_ref[...]
    score = ((_dot_nt(m5t, hi) + _dot_nt(m5t, mid)) + _dot_nt(m5t, lo))[0:nslc, :]
    jt = lax.broadcasted_iota(jnp.int32, (nslc, tq), 0)
    post = qi * tq + lax.broadcasted_iota(jnp.int32, (nslc, tq), 1)
    cur = post // SLC_BLK
    forced = (jt == 0) | (jt == cur) | (jt == cur - 1)
    score = jnp.where(jt * SLC_BLK <= post, score + jnp.where(forced, FORCE_BONUS, 0.0), NEG)
    rank = jnp.zeros((nslc, tq), F32)
    for jj in range(nslc):
        row = score[jj:jj + 1, :]
        ahead = (row > score) | ((row == score) & (jt > jj))
        rank = rank + jnp.where(ahead, 1.0, 0.0)
    sel = (rank < SLC_TOPK) & (score > 0.5 * NEG)
    selt_ref[0:nslc, :] = jnp.where(sel, 1.0, 0.0).astype(selt_ref.dtype)
    selt_ref[nslc:, :] = jnp.zeros((N_CMP - nslc, tq), selt_ref.dtype)


def cmp_select_prompt(qn, ckv, z2, m5):
    tq = 512
    nq = SEQ // tq
    return pl.pallas_call(
        _cmp_select_kernel,
        grid=(BATCH, C_KV_HEADS, nq),
        in_specs=[
            pl.BlockSpec((tq, C_REP * C_HD), lambda b, g, i: (b * nq + i, g)),
            pl.BlockSpec((None, None, 2, N_CMP, C_HD), lambda b, g, i: (b, g, 0, 0, 0)),
            pl.BlockSpec((tq, Z2_W), lambda b, g, i: (b * nq + i, 0)),
            pl.BlockSpec((N_CMP, N_CMP), lambda b, g, i: (0, 0)),
        ],
        out_specs=[
            pl.BlockSpec((tq, C_REP * C_HD), lambda b, g, i: (b * nq + i, g)),
            pl.BlockSpec((None, None, N_CMP, tq), lambda b, g, i: (b, g, 0, i)),
        ],
        out_shape=[
            jax.ShapeDtypeStruct((T_P, C_WIDTH), BF16),
            jax.ShapeDtypeStruct((BATCH, C_KV_HEADS, N_CMP, SEQ), BF16),
        ],
        compiler_params=_cp(("arbitrary", "arbitrary", "arbitrary")),
        name="cmp_select",
    )(qn, ckv, z2, m5)


def _flash_kernel(q_ref, k_ref, v_ref, selt_ref, gate_ref, o_ref, vt_sc, m_sc, l_sc, acc_sc, *, branch, tk):
    g = pl.program_id(1)
    qi = pl.program_id(2)
    tq = q_ref.shape[0]

    @pl.when(qi == 0)
    def _():
        for jt in range(SEQ // tk):
            vt_sc[jt] = v_ref[jt * tk:(jt + 1) * tk, :].T

    q = jnp.concatenate([q_ref[:, r * C_HD:(r + 1) * C_HD] for r in range(C_REP)], axis=0)
    m_sc[...] = jnp.full(m_sc.shape, NEG, F32)
    l_sc[...] = jnp.zeros(l_sc.shape, F32)
    acc_sc[...] = jnp.zeros(acc_sc.shape, F32)
    pos = qi * tq + lax.broadcasted_iota(jnp.int32, (tk, tq), 1)
    hi = ((qi + 1) * tq + tk - 1) // tk
    if branch == 1:
        lo = 0
        selt = selt_ref[...]
    else:
        lo = jnp.maximum(qi * tq - WINDOW, 0) // tk

    def step(kj, carry):
        k0 = pl.multiple_of(kj * tk, tk)
        k = k_ref[pl.ds(k0, tk), :].astype(BF16)
        vt = vt_sc[kj].astype(BF16)
        key = k0 + lax.broadcasted_iota(jnp.int32, (tk, tq), 0)
        if branch == 1:
            kb = (k0 + lax.broadcasted_iota(jnp.int32, (tk, N_CMP), 0)) // SLC_BLK
            expand = jnp.where(kb == lax.broadcasted_iota(jnp.int32, (tk, N_CMP), 1), 1.0, 0.0).astype(BF16)
            valid = (_dot(expand, selt) > 0.5) & (key <= pos)
        else:
            valid = (key <= pos) & (key > pos - WINDOW)
        valid4 = jnp.concatenate([valid] * C_REP, axis=1)
        s = jnp.where(valid4, _dot_nt(k, q) * (C_HD ** -0.5), NEG)
        m_old = m_sc[...]
        m_new = jnp.maximum(m_old, jnp.max(s, axis=0, keepdims=True))
        alpha = jnp.exp(m_old - m_new)
        p = jnp.where(valid4, jnp.exp(s - m_new), 0.0)
        l_sc[...] = alpha * l_sc[...] + jnp.sum(p, axis=0, keepdims=True)
        acc_sc[...] = alpha * acc_sc[...] + _dot(vt, p.astype(BF16))
        m_sc[...] = m_new
        return carry

    lax.fori_loop(lo, hi, step, 0)
    out = (acc_sc[...] / jnp.maximum(l_sc[...], 1e-30)).T
    gates = gate_ref[...]
    for r in range(C_REP):
        gcol = _gate_col(gates, Z2_CG + branch * C_HEADS + g * C_REP + r)
        o_ref[:, r * C_HD:(r + 1) * C_HD] = (out[r * tq:(r + 1) * tq] * _sigmoid(gcol)).astype(o_ref.dtype)


def flash_prompt(qn, kv, selt, z2, branch):
    tq = 256
    tk = 256
    nq = SEQ // tq
    kcol = 2 * C_KV_HEADS if branch == 1 else 0
    kern = functools.partial(_flash_kernel, branch=branch, tk=tk)
    return pl.pallas_call(
        kern,
        grid=(BATCH, C_KV_HEADS, nq),
        in_specs=[
            pl.BlockSpec((tq, C_REP * C_HD), lambda b, g, i: (b * nq + i, g)),
            pl.BlockSpec((SEQ, C_HD), lambda b, g, i: (b, kcol + g)),
            pl.BlockSpec((SEQ, C_HD), lambda b, g, i: (b, kcol + C_KV_HEADS + g)),
            pl.BlockSpec((None, None, N_CMP, tq), lambda b, g, i: (b, g, 0, i)),
            pl.BlockSpec((tq, Z2_W), lambda b, g, i: (b * nq + i, 0)),
        ],
        out_specs=pl.BlockSpec((tq, C_REP * C_HD), lambda b, g, i: (b * nq + i, g)),
        out_shape=jax.ShapeDtypeStruct((T_P, C_WIDTH), BF16),
        scratch_shapes=[
            pltpu.VMEM((SEQ // tk, C_HD, tk), F32),
            pltpu.VMEM((1, C_REP * tq), F32),
            pltpu.VMEM((1, C_REP * tq), F32),
            pltpu.VMEM((C_HD, C_REP * tq), F32),
        ],
        compiler_params=_cp(("arbitrary", "arbitrary", "arbitrary")),
        name="flash_slc" if branch == 1 else "flash_win",
    )(qn, kv, kv, selt, z2)


def _merge_kernel(oa_ref, ob_ref, oc0_ref, oc1_ref, oc2_ref, ga_ref, gb_ref, gc_ref, wa_ref, wb_ref, wc_ref, o_ref, oc_sc):
    @pl.when(pl.program_id(1) == 0)
    def _():
        oc = oc0_ref[...].astype(F32) + oc1_ref[...].astype(F32) + oc2_ref[...].astype(F32)
        oc_sc[...] = oc.astype(BF16)

    m = _sigmoid(ga_ref[...].astype(F32)) * _dot(oa_ref[...], wa_ref[...])
    m = m + _sigmoid(gb_ref[...].astype(F32)) * _dot(ob_ref[...], wb_ref[...])
    m = m + _sigmoid(gc_ref[...].astype(F32)) * _dot(oc_sc[...], wc_ref[...])
    o_ref[...] = m.astype(o_ref.dtype)


def merge(o_a, o_b, oc0, oc1, oc2, z1, wa, wb, wc, l, tm):
    t = o_a.shape[0]
    tn = 512
    nj = D_MODEL // tn
    gate = lambda k: pl.BlockSpec((tm, tn), lambda i, j: (i, (Z_MG + k * D_MODEL) // tn + j))
    wspec = lambda kdim: pl.BlockSpec((None, kdim, tn), lambda i, j: (l, 0, j))
    return pl.pallas_call(
        _merge_kernel,
        grid=(t // tm, nj),
        in_specs=[
            pl.BlockSpec((tm, A_WIDTH), lambda i, j: (i, 0)),
            pl.BlockSpec((tm, B_WIDTH), lambda i, j: (i, 0)),
            pl.BlockSpec((tm, C_WIDTH), lambda i, j: (i, 0)),
            pl.BlockSpec((tm, C_WIDTH), lambda i, j: (i, 0)),
            pl.BlockSpec((tm, C_WIDTH), lambda i, j: (i, 0)),
            gate(0), gate(1), gate(2),
            wspec(A_WIDTH), wspec(B_WIDTH), wspec(C_WIDTH),
        ],
        out_specs=pl.BlockSpec((tm, tn), lambda i, j: (i, j)),
        out_shape=jax.ShapeDtypeStruct((t, D_MODEL), BF16),
        scratch_shapes=[pltpu.VMEM((tm, C_WIDTH), BF16)],
        compiler_params=_cp(("arbitrary", "arbitrary")),
        name="merge",
    )(o_a, o_b, oc0, oc1, oc2, z1, z1, z1, wa, wb, wc)


N_CELLS = 50
N_CELL_ROWS = 56


def _cell_tables():
    sa = np.zeros((N_CELL_ROWS, P_TOPK), np.float32)
    sb = np.zeros((N_CELL_ROWS, P_TOPK), np.float32)
    r = 0
    for a in range(P_TOPK):
        for b in range(P_TOPK // (a + 1)):
            sa[r, a] = 1.0
            sb[r, b] = 1.0
            r += 1
    assert r == N_CELLS
    return sa, sb


def _top16_cols(s, row_iota, exact, want_rank=True):
    track = exact or want_rank
    rank = jnp.full(s.shape, float(P_TOPK), F32) if track else None
    work = s
    top = jnp.zeros((P_TOPK, s.shape[1]), F32)
    r_iota = lax.broadcasted_iota(jnp.int32, top.shape, 0)
    for r in range(P_TOPK):
        m = jnp.max(work, axis=0, keepdims=True)
        if exact:
            idx = jnp.min(jnp.where(work == m, row_iota, 1.0e9), axis=0, keepdims=True)
            hit = row_iota == idx
        else:
            hit = work == m
        if track:
            rank = jnp.where(hit, float(r), rank)
        work = jnp.where(hit, -jnp.inf, work)
        top = jnp.where(r_iota == r, m, top)
    taken = (rank < float(P_TOPK)) if track else (s >= top[P_TOPK - 1:P_TOPK, :])
    n_out = jnp.sum(jnp.where(taken, 1.0, 0.0), axis=0, keepdims=True)
    return rank, top, n_out


def _peer_topk_kernel(qn_ref, sk_ref, sa_ref, sb_ref, sat_ref, r2_ref, cnt_ref, e1_ref, e2_ref):
    tb = qn_ref.shape[0]
    lanes = 128
    sk1 = sk_ref[0].astype(BF16)
    sk2 = sk_ref[1].astype(BF16)
    s1_all = _dot_nt(sk1, qn_ref[:, 0:P_HALF])
    s2_all = _dot_nt(sk2, qn_ref[:, P_HALF:2 * P_HALF])
    n_iota = lax.broadcasted_iota(jnp.int32, (P_NKEYS, lanes), 0).astype(F32)
    c_iota = lax.broadcasted_iota(jnp.int32, (N_CELL_ROWS, lanes), 0).astype(F32)
    sa = sa_ref[...]
    sb = sb_ref[...]
    sat = sat_ref[...]

    def slab(t, exact):
        cols = slice(t * lanes, (t + 1) * lanes)
        s1 = s1_all[:, cols]
        s2 = s2_all[:, cols]
        rank1, top1, n1 = _top16_cols(s1, n_iota, exact, want_rank=False)
        rank2, top2, n2 = _top16_cols(s2, n_iota, exact)
        cand = _dot_exact_lhs(sa, top1) + _dot_exact_lhs(sb, top2)
        cand = jnp.where(c_iota < N_CELLS, cand, -jnp.inf)
        rankc, topc, nc = _top16_cols(cand, c_iota, exact, want_rank=False)
        if exact:
            in1 = rank1 < float(P_TOPK)
            picked = rankc < float(P_TOPK)
        else:
            in1 = s1 >= top1[P_TOPK - 1:P_TOPK, :]
            picked = cand >= topc[P_TOPK - 1:P_TOPK, :]
        zsum = jnp.sum(jnp.exp(topc - topc[0:1, :]), axis=0, keepdims=True)
        cnt = _dot(sat, jnp.where(picked, 1.0, 0.0).astype(BF16))
        cntd = jnp.zeros(s1.shape, F32)
        for a in range(P_TOPK):
            is_a = (rank1 == float(a)) if exact else (s1 == top1[a:a + 1, :])
            cntd = jnp.where(is_a, cnt[a:a + 1, :], cntd)
        e1 = jnp.where(in1, jnp.exp(s1 - top1[0:1, :]), 0.0) / zsum
        e2 = jnp.exp(s2 - top2[0:1, :])
        r2_ref[:, cols] = rank2.astype(r2_ref.dtype)
        cnt_ref[:, cols] = cntd
        e1_ref[:, cols] = e1
        e2_ref[:, cols] = e2.astype(e2_ref.dtype)
        return jnp.abs(n1 - P_TOPK) + jnp.abs(n2 - P_TOPK) + jnp.abs(nc - P_TOPK)

    tied = [jnp.max(slab(t, False)) > 0.0 for t in range(tb // lanes)]
    for t in range(tb // lanes):
        @pl.when(tied[t])
        def _():
            slab(t, True)


def peer_topk(qn, subkeys, l, tb):
    t = qn.shape[0]
    sa, sb = _cell_tables()
    out = lambda dt: jax.ShapeDtypeStruct((P_HEADS, P_NKEYS, t), dt)
    ospec = pl.BlockSpec((None, P_NKEYS, tb), lambda i, h: (h, 0, i))
    return pl.pallas_call(
        _peer_topk_kernel,
        grid=(t // tb, P_HEADS),
        in_specs=[
            pl.BlockSpec((tb, 2 * P_HALF), lambda i, h: (i, h)),
            pl.BlockSpec((None, None, 2, P_NKEYS, P_HALF), lambda i, h: (l, h, 0, 0, 0)),
            pl.BlockSpec((N_CELL_ROWS, P_TOPK), lambda i, h: (0, 0)),
            pl.BlockSpec((N_CELL_ROWS, P_TOPK), lambda i, h: (0, 0)),
            pl.BlockSpec((P_TOPK, N_CELL_ROWS), lambda i, h: (0, 0)),
        ],
        out_specs=[ospec, ospec, ospec, ospec],
        out_shape=[out(BF16), out(F32), out(F32), out(BF16)],
        compiler_params=_cp(("arbitrary", "arbitrary")),
        name="peer_topk",
    )(qn, subkeys, jnp.asarray(sa, BF16), jnp.asarray(sb, BF16), jnp.asarray(sa.T, BF16))


PEER_EC = 512
PEER_NGRP = P_NEXP // PEER_EC


def _peer_main_kernel(h_ref, ua_ref, ub_ref, va_ref, vb_ref, r2_ref, cnt_ref, e1_ref, e2_ref, y_ref, pa_sc, pb_sc):
    e = pl.program_id(1)
    last = pl.num_programs(1) - 1
    nchunk = PEER_EC // P_NKEYS

    @pl.when(e == 0)
    def _():
        y_ref[...] = jnp.zeros_like(y_ref)
        pb_sc[...] = jnp.zeros_like(pb_sc)

    def gated(scores, grp):
        hid = _gelu(scores)
        zero = jnp.zeros((), BF16)
        parts = []
        for cc in range(nchunk):
            c = grp * nchunk + cc
            w = None
            for h in range(P_HEADS):
                e1 = e1_ref[h, pl.ds(c, 1), :].astype(BF16)
                cn = cnt_ref[h, pl.ds(c, 1), :].astype(BF16)
                term = jnp.where(r2_ref[h] < cn, e2_ref[h], zero) * e1
                w = term if w is None else w + term
            parts.append(w.astype(F32) * hid[cc * P_NKEYS:(cc + 1) * P_NKEYS, :])
        return jnp.concatenate(parts, axis=0).T.astype(BF16)

    @pl.when(e < last)
    def _():
        sa = _dot_nt(ua_ref[...], h_ref[...])
        y_ref[...] += _dot(pb_sc[...], va_ref[...])
        sb = _dot_nt(ub_ref[...], h_ref[...])
        pa_sc[...] = gated(sa, 2 * e)
        y_ref[...] += _dot(pa_sc[...], vb_ref[...])
        pb_sc[...] = gated(sb, 2 * e + 1)

    @pl.when(e == last)
    def _():
        y_ref[...] += _dot(pb_sc[...], va_ref[...])


def peer_main(h2, u_bf, v_bf, r2, cnt, e1, e2, l, tb):
    t = h2.shape[0]
    nstep = PEER_NGRP // 2 + 1
    gspec = pl.BlockSpec((P_HEADS, P_NKEYS, tb), lambda i, e: (0, 0, i))
    grp = lambda f: pl.BlockSpec((None, PEER_EC, D_MODEL), lambda i, e: (l, jnp.clip(f(e), 0, PEER_NGRP - 1), 0))
    return pl.pallas_call(
        _peer_main_kernel,
        grid=(t // tb, nstep),
        in_specs=[
            pl.BlockSpec((tb, D_MODEL), lambda i, e: (i, 0)),
            grp(lambda e: 2 * e), grp(lambda e: 2 * e + 1),
            grp(lambda e: 2 * e - 1), grp(lambda e: 2 * e),
            gspec, gspec, gspec, gspec,
        ],
        out_specs=pl.BlockSpec((tb, D_MODEL), lambda i, e: (i, 0)),
        out_shape=jax.ShapeDtypeStruct((t, D_MODEL), F32),
        scratch_shapes=[pltpu.VMEM((tb, PEER_EC), BF16), pltpu.VMEM((tb, PEER_EC), BF16)],
        compiler_params=_cp(("arbitrary", "arbitrary")),
        name="peer_main",
    )(h2, u_bf, u_bf, v_bf, v_bf, r2, cnt, e1, e2)


N_PAGES = PAST_LEN // PAGE_SIZE
N_SEG = PAST_LEN // CMP_STRIDE
PAGES_PER_STEP = 8
SEG_PER_STEP = PAGES_PER_STEP * PAGE_SIZE // CMP_STRIDE
N_PAST_BLK = PAST_LEN // SLC_BLK
SLC_LANES = 384
S_ROWS = 16


def _row_of(x, b):
    r = lax.broadcasted_iota(jnp.int32, x.shape, 0)
    return jnp.sum(jnp.where(r == b, x, 0.0), axis=0, keepdims=True)


def _sample_pool_kernel(pt_ref, *refs):
    pages = refs[:PAGES_PER_STEP]
    wf_ref, ws_ref, o_ref, carry_sc = refs[PAGES_PER_STEP:]
    pc = pl.program_id(2)

    @pl.when(pc == 0)
    def _():
        carry_sc[...] = jnp.zeros_like(carry_sc)

    wf = wf_ref[...]
    ws = ws_ref[...]
    fs, ss = [], []
    for r in pages:
        xs = r[...].reshape(PAGE_SIZE // CMP_STRIDE, CMP_STRIDE, 2, C_KV_HEADS, C_HD)
        fs.append(jnp.sum(xs * wf[None], axis=1))
        ss.append(jnp.sum(xs * ws[None], axis=1))
    first = jnp.concatenate(fs, axis=0)
    second = jnp.concatenate(ss, axis=0)
    o_ref[...] = jnp.concatenate([carry_sc[...], first[:SEG_PER_STEP - 1]], axis=0) + second
    carry_sc[...] = first[SEG_PER_STEP - 1:]


def sample_pool(cache, pt_flat, wf_all, ws_all):
    def page(k):
        return pl.BlockSpec((None, None, PAGE_SIZE, 2, C_KV_HEADS, C_HD),
                            lambda l, b, pc, pt: (l, pt[b * N_PAGES + pc * PAGES_PER_STEP + k], 0, 0, 0, 0))

    wspec = pl.BlockSpec((None, CMP_STRIDE, 2, C_KV_HEADS, C_HD), lambda l, b, pc, pt: (l, 0, 0, 0, 0))
    grid_spec = pltpu.PrefetchScalarGridSpec(
        num_scalar_prefetch=1,
        grid=(DEPTH, DEC_BATCH, N_PAGES // PAGES_PER_STEP),
        in_specs=[page(k) for k in range(PAGES_PER_STEP)] + [wspec, wspec],
        out_specs=pl.BlockSpec((None, None, SEG_PER_STEP, 2, C_KV_HEADS, C_HD), lambda l, b, pc, pt: (l, b, pc, 0, 0, 0)),
        scratch_shapes=[pltpu.VMEM((1, 2, C_KV_HEADS, C_HD), F32)],
    )
    return pl.pallas_call(
        _sample_pool_kernel,
        grid_spec=grid_spec,
        out_shape=jax.ShapeDtypeStruct((DEPTH, DEC_BATCH, N_SEG, 2, C_KV_HEADS, C_HD), F32),
        compiler_params=_cp(("arbitrary", "arbitrary", "arbitrary")),
        name="sample_pool",
    )(pt_flat, *([cache] * PAGES_PER_STEP), wf_all, ws_all)


def _sample_cmp_mlp_kernel(x_ref, w1_ref, b1_ref, w2_ref, b2_ref, o_ref):
    for kv in range(2):
        w1 = w1_ref[kv].astype(BF16)
        w2 = w2_ref[kv].astype(BF16)
        for g in range(C_KV_HEADS):
            hid = _gelu(_dot(x_ref[:, kv, g, :].astype(BF16), w1) + b1_ref[kv])
            o_ref[kv * C_KV_HEADS + g] = _dot(hid.astype(BF16), w2) + b2_ref[kv]


def sample_cmp_mlp(pooled, w1, b1, w2, b2):
    wspec = pl.BlockSpec((None, 2, C_HD, C_HD), lambda l, b: (l, 0, 0, 0))
    bspec = pl.BlockSpec((None, 2, 1, C_HD), lambda l, b: (l, 0, 0, 0))
    return pl.pallas_call(
        _sample_cmp_mlp_kernel,
        grid=(DEPTH, DEC_BATCH),
        in_specs=[pl.BlockSpec((None, None, N_SEG, 2, C_KV_HEADS, C_HD), lambda l, b: (l, b, 0, 0, 0, 0)),
                  wspec, bspec, wspec, bspec],
        out_specs=pl.BlockSpec((None, None, 2 * C_KV_HEADS, N_SEG, C_HD), lambda l, b: (l, b, 0, 0, 0)),
        out_shape=jax.ShapeDtypeStruct((DEPTH, DEC_BATCH, 2 * C_KV_HEADS, N_SEG, C_HD), F32),
        compiler_params=_cp(("arbitrary", "arbitrary")),
        name="sample_cmp_mlp",
    )(pooled, w1, b1, w2, b2)


def _sample_cmp_kernel(q_ref, ckv_ref, gate_ref, m5_ref, o_ref, idx_ref, ok_ref):
    b = pl.program_id(0)
    qrow = _row_of(q_ref[...].astype(F32), b)
    grow = _row_of(gate_ref[...], b)
    m_i = lax.broadcasted_iota(jnp.int32, (8, N_SEG), 1)
    valid = (m_i >= 1) & ((m_i - 1) * CMP_STRIDE + (CMP_LEN - 1) <= PAST_LEN)
    lane = lax.broadcasted_iota(jnp.int32, (1, Z2_W), 1)
    j = lax.broadcasted_iota(jnp.int32, (8, SLC_LANES), 1)
    jf = j.astype(F32)
    cur = PAST_LEN // SLC_BLK
    forced = (j == 0) | (j == cur) | (j == cur - 1)
    idx_row = jnp.zeros((1, Z2_W), F32)
    ok_row = jnp.zeros((1, Z2_W), F32)
    for g in range(C_KV_HEADS):
        ck = ckv_ref[g].astype(BF16)
        cv = ckv_ref[C_KV_HEADS + g].astype(BF16)
        imp = jnp.zeros((8, N_SEG), F32)
        for r in range(C_REP):
            h = g * C_REP + r
            q8 = jnp.broadcast_to(qrow[:, h * C_HD:(h + 1) * C_HD], (8, C_HD)).astype(BF16)
            p = _softmax_rows(_dot_nt(q8, ck) * (C_HD ** -0.5), valid)
            imp = imp + p
            o = _dot(p.astype(BF16), cv)
            gate = jnp.sum(jnp.where(lane == Z2_CG + h, grow, 0.0), axis=1, keepdims=True)
            o_ref[:, h * C_HD:(h + 1) * C_HD] = o[0:1] * _sigmoid(gate)
        score = _dot_exact_rhs(imp, m5_ref[...])
        score = jnp.where(j * SLC_BLK <= PAST_LEN, score + jnp.where(forced, FORCE_BONUS, 0.0), -3.0e38)
        for r in range(SLC_TOPK):
            m = jnp.max(score, axis=1, keepdims=True)
            idx = jnp.min(jnp.where(score == m, jf, 1.0e9), axis=1, keepdims=True)
            slot = lane == g * SLC_TOPK + r
            idx_row = jnp.where(slot, idx[0:1], idx_row)
            ok_row = jnp.where(slot, jnp.where(m[0:1] > 0.5 * NEG, 1.0, 0.0), ok_row)
            score = jnp.where(jf == idx, -3.4e38, score)
    idx_ref[...] = idx_row.astype(jnp.int32)
    ok_ref[...] = ok_row.astype(jnp.int32)


def sample_cmp(qn, ckv_all, z2, m5s, l):
    small = lambda n, dt: jax.ShapeDtypeStruct((DEC_BATCH, 1, n), dt)
    ospec = lambda n: pl.BlockSpec((None, 1, n), lambda b: (b, 0, 0))
    return pl.pallas_call(
        _sample_cmp_kernel,
        grid=(DEC_BATCH,),
        in_specs=[
            pl.BlockSpec((S_ROWS, C_WIDTH), lambda b: (0, 0)),
            pl.BlockSpec((None, None, 2 * C_KV_HEADS, N_SEG, C_HD), lambda b: (l, b, 0, 0, 0)),
            pl.BlockSpec((S_ROWS, Z2_W), lambda b: (0, 0)),
            pl.BlockSpec((N_SEG, SLC_LANES), lambda b: (0, 0)),
        ],
        out_specs=[ospec(C_WIDTH), ospec(Z2_W), ospec(Z2_W)],
        out_shape=[small(C_WIDTH, F32), small(Z2_W, jnp.int32), small(Z2_W, jnp.int32)],
        compiler_params=_cp(("arbitrary",)),
        name="sample_cmp",
    )(qn, ckv_all, z2, m5s)


def _sample_attend_kernel(phys_ref, rb_ref, idx_ref, ok_ref, *refs):
    nblk = C_KV_HEADS * SLC_TOPK
    blks = refs[:nblk]
    q_ref, kva_ref, kvb_ref, win_ref, gate_ref, oslc_ref, owin_ref, kcat_sc, vcat_sc = refs[nblk:]
    b = pl.program_id(0)
    scale = C_HD ** -0.5
    qall = _row_of(q_ref[...].astype(F32), b)
    kva_row = _row_of(kva_ref[...], b)
    kvb_row = _row_of(kvb_ref[...], b)
    grow = _row_of(gate_ref[...], b)
    lane_g = lax.broadcasted_iota(jnp.int32, grow.shape, 1)
    r8 = lax.broadcasted_iota(jnp.int32, (8, C_HD), 0)
    r1 = lax.broadcasted_iota(jnp.int32, (8, 1), 0)
    r64 = lax.broadcasted_iota(jnp.int32, (SLC_BLK, C_HD), 0)
    lane = lax.broadcasted_iota(jnp.int32, (8, SLC_TOPK * SLC_BLK), 1)
    kp = PAST_LEN - WINDOW + lax.broadcasted_iota(jnp.int32, (8, WINDOW), 1)
    validw = (kp <= PAST_LEN) & (kp > PAST_LEN - WINDOW)
    seg = lambda row, i: row[:, i * C_HD:(i + 1) * C_HD]
    for g in range(C_KV_HEADS):
        base = (b * C_KV_HEADS + g) * SLC_TOPK
        q4 = jnp.zeros((8, C_HD), F32)
        for r in range(C_REP):
            q4 = jnp.where(r8 == r, jnp.broadcast_to(seg(qall, g * C_REP + r), (8, C_HD)), q4)
        q4 = q4.astype(BF16)
        newk = jnp.where(r64 == 0, jnp.broadcast_to(seg(kva_row, 2 * C_KV_HEADS + g), (SLC_BLK, C_HD)), 0.0)
        newv = jnp.where(r64 == 0, jnp.broadcast_to(seg(kva_row, 3 * C_KV_HEADS + g), (SLC_BLK, C_HD)), 0.0)
        rowpos = lane % SLC_BLK
        okv = jnp.zeros(lane.shape, jnp.int32)
        for k in range(SLC_TOPK):
            idx_k = idx_ref[base + k]
            is_new = jnp.full((SLC_BLK, C_HD), idx_k, jnp.int32) >= N_PAST_BLK
            blk = blks[g * SLC_TOPK + k]
            kcat_sc[k * SLC_BLK:(k + 1) * SLC_BLK, :] = jnp.where(is_new, newk, blk[:, 0, g, :])
            vcat_sc[k * SLC_BLK:(k + 1) * SLC_BLK, :] = jnp.where(is_new, newv, blk[:, 1, g, :])
            in_k = lane // SLC_BLK == k
            rowpos = jnp.where(in_k, rowpos + idx_k * SLC_BLK, rowpos)
            okv = jnp.where(in_k, ok_ref[base + k], okv)
        s = _dot_nt(q4, kcat_sc[...].astype(BF16)) * scale
        p = _softmax_rows(s, (okv > 0) & (rowpos <= PAST_LEN))
        o_slc = _dot(p.astype(BF16), vcat_sc[...].astype(BF16))
        sw = jnp.where(validw, _dot_nt(q4, win_ref[:, 0, g, :].astype(BF16)) * scale, NEG)
        knew = seg(kvb_row, g).astype(BF16).astype(F32)
        vnew = seg(kvb_row, C_KV_HEADS + g)
        sn = jnp.sum(q4.astype(F32) * knew, axis=1, keepdims=True) * scale
        m = jnp.maximum(jnp.max(sw, axis=1, keepdims=True), sn)
        ew = jnp.where(validw, jnp.exp(sw - m), 0.0)
        en = jnp.exp(sn - m)
        den = jnp.maximum(jnp.sum(ew, axis=1, keepdims=True) + en, 1e-30)
        o_win = (_dot(ew.astype(BF16), win_ref[:, 1, g, :].astype(BF16)) + en * vnew) / den
        for branch, o, o_ref in ((1, o_slc, oslc_ref), (2, o_win, owin_ref)):
            gcol = jnp.zeros((8, 1), F32)
            for r in range(C_REP):
                col = Z2_CG + branch * C_HEADS + g * C_REP + r
                gcol = jnp.where(r1 == r, jnp.sum(jnp.where(lane_g == col, grow, 0.0), axis=1, keepdims=True), gcol)
            o_ref[g] = o * _sigmoid(gcol)


def sample_attend(cache, win_buf, phys, rb, idx, ok, qn, kva, kvb, z2, l):
    nblk = C_KV_HEADS * SLC_TOPK

    def cblk(j):
        return pl.BlockSpec((None, None, SLC_BLK, 2, C_KV_HEADS, C_HD),
                            lambda b, ph, rbr, ix, okr: (l, ph[b * nblk + j], rbr[b * nblk + j], 1, 0, 0))

    rows = lambda w: pl.BlockSpec((S_ROWS, w), lambda b, *_: (0, 0))
    ospec = pl.BlockSpec((None, C_KV_HEADS, 8, C_HD), lambda b, *_: (b, 0, 0, 0))
    grid_spec = pltpu.PrefetchScalarGridSpec(
        num_scalar_prefetch=4,
        grid=(DEC_BATCH,),
        in_specs=[cblk(j) for j in range(nblk)] + [
            rows(C_WIDTH), rows(KVA_W), rows(KVB_W),
            pl.BlockSpec((None, None, WINDOW, 2, C_KV_HEADS, C_HD), lambda b, *_: (l, b, 0, 0, 0, 0)),
            rows(Z2_W),
        ],
        out_specs=[ospec, ospec],
        scratch_shapes=[pltpu.VMEM((SLC_TOPK * SLC_BLK, C_HD), F32), pltpu.VMEM((SLC_TOPK * SLC_BLK, C_HD), F32)],
    )
    out = jax.ShapeDtypeStruct((DEC_BATCH, C_KV_HEADS, 8, C_HD), F32)
    return pl.pallas_call(
        _sample_attend_kernel,
        grid_spec=grid_spec,
        out_shape=[out, out],
        compiler_params=_cp(("arbitrary",)),
        name="sample_attend",
    )(phys, rb, idx, ok, *([cache] * nblk), qn, kva, kvb, win_buf, z2)


def _sample_ab_kernel(au_ref, av_ref, bq_ref, bk_ref, bv_ref, br_ref, z2_ref, lng_ref, lnb_ref, w00_ref, b0_ref,
                      gw_ref, gb_ref, on_ref, s0_ref, oa_ref, ob_ref, av_out_ref, s_ref):
    u = _gelu(au_ref[...].astype(F32))
    v = _gelu(av_ref[...].astype(F32))
    mu = jnp.mean(v, axis=-1, keepdims=True)
    vc = v - mu
    v = vc * lax.rsqrt(jnp.mean(vc * vc, axis=-1, keepdims=True) + EPS) * lng_ref[...] + lnb_ref[...]
    av_out_ref[...] = v
    oa_ref[...] = u * (v * w00_ref[...] + b0_ref[...])
    la = _log_sigmoid(_dot(z2_ref[...].astype(BF16), gw_ref[...]) + gb_ref[...]) * (1.0 / B_GATE_TAU)
    q = bq_ref[...].astype(F32) * (B_DK ** -0.5)
    k = bk_ref[...].astype(F32)
    vv = bv_ref[...].astype(F32)
    gate = br_ref[...].astype(F32)
    gate = gate * _sigmoid(gate)
    qd = q * jnp.exp(la)
    kd = k * jnp.exp(-la)
    on = on_ref[...]
    ob_ref[...] = jnp.zeros_like(ob_ref)
    for b in range(DEC_BATCH):
        for h in range(B_HEADS):
            dk = slice(h * B_DK, (h + 1) * B_DK)
            dv = slice(h * B_DV, (h + 1) * B_DV)
            qd_r = qd[b:b + 1, dk]
            v_r = vv[b:b + 1, dv]
            att = jnp.sum(qd_r * kd[b:b + 1, dk], axis=1, keepdims=True)
            s0 = s0_ref[b, h]
            o = att * v_r + _dot(jnp.broadcast_to(qd_r, (8, B_DK)).astype(BF16), s0.astype(BF16))[0:1]
            s_ref[b, h] = s0 * _col_from_row(jnp.exp(la[b:b + 1, dk])) + _col_from_row(k[b:b + 1, dk]) * v_r
            ob_ref[b:b + 1, dv] = _rms(o, on) * gate[b:b + 1, dv]


def sample_ab(z1, z2, lng, lnb, w00, b0, gw2p, gb, on_g, state_gla, l):
    blk = lambda w, off: pl.BlockSpec((S_ROWS, w), lambda i: (0, off // w))
    row = lambda n: jax.ShapeDtypeStruct((S_ROWS, n), F32)
    rspec = lambda n: pl.BlockSpec((S_ROWS, n), lambda i: (0, 0))
    sspec = pl.BlockSpec((None, DEC_BATCH, B_HEADS, B_DK, B_DV), lambda i: (l, 0, 0, 0, 0))
    return pl.pallas_call(
        _sample_ab_kernel,
        grid=(1,),
        in_specs=[
            blk(A_WIDTH, Z_AU), blk(A_WIDTH, Z_AV), blk(256, Z_BQ), blk(256, Z_BK), blk(B_WIDTH, Z_BV),
            blk(B_WIDTH, Z_BR), blk(Z2_W, 0),
            _lspec(l, 1, A_WIDTH), _lspec(l, 1, A_WIDTH), _lspec(l, 1, A_WIDTH), _lspec(l, 1, A_WIDTH),
            _lspec(l, Z2_W, 256), _lspec(l, 1, 256), _lspec(l, 1, B_DV),
            sspec,
        ],
        out_specs=[rspec(A_WIDTH), rspec(B_WIDTH), rspec(A_WIDTH),
                   pl.BlockSpec((DEC_BATCH, B_HEADS, B_DK, B_DV), lambda i: (0, 0, 0, 0))],
        out_shape=[row(A_WIDTH), row(B_WIDTH), row(A_WIDTH),
                   jax.ShapeDtypeStruct((DEC_BATCH, B_HEADS, B_DK, B_DV), F32)],
        compiler_params=_cp(("arbitrary",)),
        name="sample_ab",
    )(z1, z1, z1, z1, z1, z1, z2, lng, lnb, w00, b0, gw2p, gb, on_g, state_gla)


def _m5s_table():
    m5 = np.zeros((N_SEG, SLC_LANES), np.float32)
    for j in range(N_PAST_BLK + 1):
        for m in range(max(4 * j, 1), min(4 * j + 4, N_SEG - 1) + 1):
            m5[m, j] = 1.0
    return m5


def _rope_tables(pos):
    half = C_ROT // 2
    inv = jnp.float32(ROPE_THETA) ** (-jnp.arange(half, dtype=F32) / half)
    ang = pos.astype(F32)[:, None] * inv[None, :]
    cos = jnp.cos(ang)
    sin = jnp.sin(ang)
    n = pos.shape[0]
    ones = jnp.ones((n, C_HD - C_ROT), F32)
    zeros = jnp.zeros((n, C_HD - half), F32)
    c = jnp.concatenate([cos, cos, ones], axis=1)
    s1 = jnp.concatenate([-sin, zeros], axis=1)
    s2 = jnp.concatenate([jnp.zeros((n, half), F32), sin, jnp.zeros((n, C_HD - C_ROT), F32)], axis=1)
    return c, s1, s2


def _m5_table():
    m5 = np.zeros((N_CMP, N_CMP), np.float32)
    for j in range(SEQ // SLC_BLK):
        for n in range(4 * j - 1, 4 * j + 4):
            if 0 <= n < N_CMP - 1:
                m5[n, j] = 1.0
    return m5


def _permute_w_in(w_in):
    o = np.cumsum([0, 512, 512, 256, 256, 512, 16, 512, 1024, 1536, 24, 6144])
    seg = lambda i: w_in[..., o[i]:o[i + 1]]
    w1 = jnp.concatenate([seg(7), seg(0), seg(1), seg(4), seg(6), seg(8), seg(2), seg(3), seg(10)], axis=-1)
    pad = jnp.zeros(w_in.shape[:-1] + (Z2_W - 40,), w_in.dtype)
    w2 = jnp.concatenate([seg(5), seg(9), pad], axis=-1)
    return w1.astype(BF16), w2.astype(BF16)


def kernel(x_prompt, x_sample, cache_nsa_kv, state_win_kv, state_gla, page_table, c_prompt, c_sample, ada_w, ada_b, norm1_g, norm2_g, w_in, a_ln_g, a_ln_b, a_ws, a_bs, b_gw2, b_gb, b_on_g, c_qn_g, c_kn_g, cmp_pool, cmp_w1, cmp_b1, cmp_w2, cmp_b2, w_br_a, w_br_b, w_br_c, w_out, p_wq, p_qn_g, p_subkeys, p_u, p_v):
    tm_p, tm_s = 512, T_S
    w1_all, w2_all = _permute_w_in(w_in)
    wa_all = w_br_a.astype(BF16)
    wb_all = w_br_b.astype(BF16)
    wc_all = w_br_c.astype(BF16)
    wo_all = w_out.astype(BF16)
    wq_all = p_wq.astype(BF16)
    u_all = p_u.astype(BF16)
    v_all = p_v.astype(BF16)
    row3 = lambda a: a.reshape(DEPTH, 1, a.shape[-1])
    g1_all, g2_all = row3(norm1_g), row3(norm2_g)
    lng_all, lnb_all = row3(a_ln_g), row3(a_ln_b)
    bst_all = jnp.swapaxes(a_bs, 1, 2)
    gw2p_all = jnp.pad(b_gw2, ((0, 0), (0, Z2_W - B_GATE_RANK), (0, 0))).astype(BF16)
    gb_all, on_all = row3(b_gb), row3(b_on_g)
    gq_all, pqg_all = row3(c_qn_g), row3(p_qn_g)
    cb1_all = cmp_b1.reshape(DEPTH, 2, 1, C_HD)
    cb2_all = cmp_b2.reshape(DEPTH, 2, 1, C_HD)
    c40 = jnp.concatenate([jnp.repeat(c_prompt, N_SEQ_ROWS, axis=0), c_sample], axis=0)
    mod = adaln_table(c40, ada_w, ada_b)
    rope_p = _rope_tables(jnp.arange(SEQ, dtype=jnp.int32))
    rope_s = _rope_tables(jnp.full((T_S,), PAST_LEN, jnp.int32))
    m5 = jnp.asarray(_m5_table().T, BF16)
    xp = x_prompt.reshape(T_P, D_MODEL)
    xs = jnp.pad(x_sample.reshape(DEC_BATCH, D_MODEL), ((0, T_S - DEC_BATCH), (0, 0)))
    pool_w = lambda p: jnp.broadcast_to(jnp.swapaxes(p, 1, 2)[:, :, :, None, None], (DEPTH, CMP_STRIDE, 2, C_KV_HEADS, C_HD))
    wf_all = pool_w(cmp_pool[:, :, :CMP_STRIDE])
    ws_all = pool_w(cmp_pool[:, :, CMP_STRIDE:])
    w00_all = jnp.repeat(a_ws[:, :, 0, 0], A_GW, axis=1)[:, None, :]
    b0_all = jnp.repeat(a_bs[:, :, 0], A_GW, axis=1)[:, None, :]
    m5s = jnp.asarray(_m5s_table(), BF16)
    ckv_s = sample_cmp_mlp(sample_pool(cache_nsa_kv, page_table.reshape(-1), wf_all, ws_all), cmp_w1, cb1_all, cmp_w2, cb2_all)

    outs = {k: [] for k in ('kv_p', 'kv_s', 'win_p', 'win_s', 'gla_p', 'gla_s', 'cv_s')}
    yp = ys = None
    for l in range(DEPTH):
        streams = []
        for is_s, x, y, tm in ((False, xp, yp, tm_p), (True, xs, ys, tm_s)):
            if l == 0:
                h = norm_mod(x, g1_all, mod, l, 0, 1, tm, is_s)
            else:
                x, h = resid_norm_mod(x, y, g1_all, mod, l - 1, l, 5, 0, 1, tm, is_s)
            z1 = matmul(h, w1_all, l, tm, 1024, BF16, "w_in")
            z2 = matmul(h, w2_all, l, tm, Z2_W, F32, "w_in_gates")
            rope = rope_s if is_s else rope_p
            qn, kva, kvb = qk_prep(z1, gq_all, c_kn_g, l, rope[0], rope[1], rope[2], tm, is_s)
            streams.append((x, z1, z2, qn, kva, kvb))
        x, z1, z2, qn, kva, kvb = streams[0]
        o_a = chunk_mlp_prompt(z1, lng_all, lnb_all, a_ws, bst_all, l)
        o_b, s_p = gla_prompt(z1, z2, gw2p_all, gb_all, on_all, l)
        ckv = compress_prompt(kva, cmp_pool, cmp_w1, cb1_all, cmp_w2, cb2_all, l)
        o_cmp, selt = cmp_select_prompt(qn, ckv, z2, m5)
        o_slc = flash_prompt(qn, kva, selt, z2, 1)
        o_win = flash_prompt(qn, kvb, selt, z2, 2)
        mixed_p = (o_a, o_b, o_cmp, o_slc, o_win)
        outs['kv_p'].append(kva.reshape(BATCH, SEQ, 4, C_KV_HEADS, C_HD))
        outs['win_p'].append(kvb.reshape(BATCH, SEQ, 2, C_KV_HEADS, C_HD)[:, SEQ - WINDOW:])
        outs['gla_p'].append(s_p)
        xs_, z1s, z2s, qns, kvas, kvbs = streams[1]
        oa_s, ob_s, av_s, s_s = sample_ab(z1s, z2s, lng_all, lnb_all, w00_all, b0_all, gw2p_all, gb_all, on_all, state_gla, l)
        ocmp_s, idx, ok = sample_cmp(qns, ckv_s, z2s, m5s, l)
        nsel = C_KV_HEADS * SLC_TOPK
        idx_f = idx[:, 0, :nsel].reshape(-1)
        ok_f = ok[:, 0, :nsel].reshape(-1)
        jc = jnp.minimum(idx_f, N_PAST_BLK - 1)
        bpp = PAGE_SIZE // SLC_BLK
        phys = page_table[jnp.repeat(jnp.arange(DEC_BATCH), nsel), jc // bpp]
        oslc_s, owin_s = sample_attend(cache_nsa_kv, state_win_kv, phys, jc % bpp, idx_f, ok_f, qns, kvas, kvbs, z2s, l)
        pad_s = lambda a: jnp.pad(a, ((0, T_S - a.shape[0]), (0, 0))).astype(BF16)
        heads = lambda o: o[:, :, :C_REP].reshape(DEC_BATCH, C_WIDTH)
        mixed_s = (pad_s(oa_s), pad_s(ob_s), pad_s(ocmp_s.reshape(DEC_BATCH, C_WIDTH)), pad_s(heads(oslc_s)), pad_s(heads(owin_s)))
        outs['kv_s'].append(kvas[:DEC_BATCH].reshape(DEC_BATCH, 1, 4, C_KV_HEADS, C_HD))
        outs['win_s'].append(kvbs[:DEC_BATCH].reshape(DEC_BATCH, 1, 2, C_KV_HEADS, C_HD))
        outs['gla_s'].append(s_s)
        outs['cv_s'].append(av_s[:DEC_BATCH, None, :])
        new = []
        for is_s, st, mixed, tm in ((False, streams[0], mixed_p, tm_p), (True, streams[1], mixed_s, tm_s)):
            x, z1 = st[0], st[1]
            mg = merge(*mixed, z1, wa_all, wb_all, wc_all, l, tm)
            att = matmul(mg, wo_all, l, tm, 1024, F32, "w_out")
            x1, h2 = resid_norm_mod(x, att, g2_all, mod, l, l, 2, 3, 4, tm, is_s)
            pq = peer_query(h2, wq_all, pqg_all, l, tm)
            r2, cnt, e1, e2 = peer_topk(pq, p_subkeys, l, tm)
            y = peer_main(h2, u_all, v_all, r2, cnt, e1, e2, l, tm)
            new.append((x1, y))
        (xp, yp), (xs, ys) = new
    xp = resid(xp, yp, mod, DEPTH - 1, 5, tm_p, False)
    xs = resid(xs, ys, mod, DEPTH - 1, 5, tm_s, True)
    return (
        xp.reshape(BATCH, SEQ, D_MODEL),
        xs[:DEC_BATCH].reshape(DEC_BATCH, 1, D_MODEL),
        jnp.stack(outs['kv_p']),
        jnp.stack(outs['kv_s']),
        jnp.stack(outs['win_p']),
        jnp.stack(outs['win_s']),
        jnp.stack(outs['gla_p']),
        jnp.stack(outs['gla_s']),
        jnp.stack(outs['cv_s']),
    )
```

```python
import functools

import numpy as np
import jax
import jax.numpy as jnp
from jax import lax
from jax.experimental import pallas as pl
from jax.experimental.pallas import tpu as pltpu

F32 = jnp.float32
BF16 = jnp.bfloat16

D_MODEL = 2048
BATCH = 4
SEQ = 2048
DEPTH = 4
DEC_BATCH = 8
PAST_LEN = 16384
PAGE_SIZE = 128
EPS = 1e-6
NEG = -1.0e30
A_WIDTH = 512
A_GROUPS = 4
A_GW = 128
A_CHUNK = 128
B_HEADS = 4
B_WIDTH = 512
B_DV = 128
B_DK = 64
B_GATE_RANK = 16
B_GATE_TAU = 16.0
B_CHUNK = 64
C_HEADS = 8
C_KV_HEADS = 2
C_HD = 128
C_WIDTH = 1024
C_REP = 4
C_ROT = 32
ROPE_THETA = 500000.0
CMP_LEN = 32
CMP_STRIDE = 16
SLC_BLK = 64
SLC_TOPK = 16
FORCE_BONUS = 1.0e4
WINDOW = 512
P_HEADS = 8
P_NKEYS = 128
P_NEXP = P_NKEYS * P_NKEYS
P_HALF = 128
P_TOPK = 16

T_P = BATCH * SEQ
T_S = 128
N_SEQ_ROWS = 8
N_CMP = SEQ // CMP_STRIDE

Z_CQ = 0
Z_AU = 1024
Z_AV = 1536
Z_BV = 2048
Z_BR = 2560
Z_CKV = 3072
Z_BQ = 4608
Z_BK = 4864
Z_MG = 5120
Z1_W = 11264
Z2_W = 128
Z2_CG = 16
KVA_W = 1024
KVB_W = 512

VMEM_LIMIT = 56 * 1024 * 1024


def _cp(sem, vmem=VMEM_LIMIT):
    return pltpu.CompilerParams(dimension_semantics=sem, vmem_limit_bytes=vmem)


def _lspec(l, *dims):
    nd = len(dims)
    return pl.BlockSpec((None,) + tuple(dims), lambda *_: (l,) + (0,) * nd)


def _gelu(x):
    return 0.5 * x * (1.0 + jnp.tanh(0.7978845608028654 * (x + 0.044715 * x * x * x)))


def _sigmoid(x):
    return 0.5 * jnp.tanh(0.5 * x) + 0.5


def _log_sigmoid(x):
    return jnp.minimum(x, 0.0) - jnp.log(1.0 + jnp.exp(-jnp.abs(x)))


def _dot(a, b):
    return jnp.dot(a, b, preferred_element_type=F32)


def _dot_nt(a, b):
    return lax.dot_general(a, b, (((1,), (1,)), ((), ())), preferred_element_type=F32)


def _split3(v):
    hi = v.astype(BF16)
    r1 = v - hi.astype(F32)
    mid = r1.astype(BF16)
    r2 = r1 - mid.astype(F32)
    lo = r2.astype(BF16)
    return hi, mid, lo


def _dot_exact_lhs(sel_bf16, v):
    hi, mid, lo = _split3(v)
    return (_dot(sel_bf16, hi) + _dot(sel_bf16, mid)) + _dot(sel_bf16, lo)


def _dot_exact_rhs(v, sel_bf16):
    hi, mid, lo = _split3(v)
    return (_dot(hi, sel_bf16) + _dot(mid, sel_bf16)) + _dot(lo, sel_bf16)


def _rms(x, g):
    return x * lax.rsqrt(jnp.mean(x * x, axis=-1, keepdims=True) + EPS) * g


def _mod_rows(y, scale8, shift8=None):
    tm, n = y.shape
    y3 = y.reshape(tm // N_SEQ_ROWS, N_SEQ_ROWS, n) * scale8[None]
    if shift8 is not None:
        y3 = y3 + shift8[None]
    return y3.reshape(tm, n)


def _gate_col(gates, col):
    lane = lax.broadcasted_iota(jnp.int32, gates.shape, 1)
    return jnp.sum(jnp.where(lane == col, gates, 0.0), axis=1, keepdims=True)


def _adaln_kernel(c_ref, w_ref, b_ref, o_ref):
    c = c_ref[...]
    s = (c * _sigmoid(c)).astype(BF16)
    o_ref[...] = _dot(s, w_ref[...].astype(BF16)) + b_ref[...]


def adaln_table(c40, ada_w, ada_b):
    tn = 1024
    n = 6 * D_MODEL
    return pl.pallas_call(
        _adaln_kernel,
        grid=(DEPTH, n // tn),
        in_specs=[
            pl.BlockSpec((5 * N_SEQ_ROWS, D_MODEL), lambda l, j: (0, 0)),
            pl.BlockSpec((None, D_MODEL, tn), lambda l, j: (l, 0, j)),
            pl.BlockSpec((None, 1, tn), lambda l, j: (l, 0, j)),
        ],
        out_specs=pl.BlockSpec((None, 5 * N_SEQ_ROWS, tn), lambda l, j: (l, 0, j)),
        out_shape=jax.ShapeDtypeStruct((DEPTH, 5 * N_SEQ_ROWS, n), F32),
        compiler_params=_cp(("arbitrary", "arbitrary")),
        name="adaln_table",
    )(c40, ada_w, ada_b.reshape(DEPTH, 1, n))


def _seq_of(tm, is_sample):
    if is_sample:
        return lambda i: 4
    per = SEQ // tm
    return lambda i: i // per


def _mod_spec(l, sq, k):
    return pl.BlockSpec((None, N_SEQ_ROWS, D_MODEL), lambda i: (l, sq(i), k))


def _norm_mod_kernel(x_ref, g_ref, sh_ref, sc_ref, h_ref):
    y = _rms(x_ref[...], g_ref[...])
    h_ref[...] = _mod_rows(y, 1.0 + sc_ref[...], sh_ref[...]).astype(h_ref.dtype)


def norm_mod(x, g, mod, l, k_shift, k_scale, tm, is_sample):
    t = x.shape[0]
    sq = _seq_of(tm, is_sample)
    return pl.pallas_call(
        _norm_mod_kernel,
        grid=(t // tm,),
        in_specs=[
            pl.BlockSpec((tm, D_MODEL), lambda i: (i, 0)),
            _lspec(l, 1, D_MODEL),
            _mod_spec(l, sq, k_shift),
            _mod_spec(l, sq, k_scale),
        ],
        out_specs=pl.BlockSpec((tm, D_MODEL), lambda i: (i, 0)),
        out_shape=jax.ShapeDtypeStruct((t, D_MODEL), BF16),
        compiler_params=_cp(("arbitrary",)),
        name="norm_mod",
    )(x, g, mod, mod)


def _resid_norm_mod_kernel(x_ref, y_ref, gate_ref, g_ref, sh_ref, sc_ref, xo_ref, h_ref):
    xn = x_ref[...] + _mod_rows(y_ref[...], gate_ref[...])
    xo_ref[...] = xn
    h_ref[...] = _mod_rows(_rms(xn, g_ref[...]), 1.0 + sc_ref[...], sh_ref[...]).astype(h_ref.dtype)


def resid_norm_mod(x, y, g, mod, l_gate, l, k_gate, k_shift, k_scale, tm, is_sample):
    t = x.shape[0]
    sq = _seq_of(tm, is_sample)
    row = pl.BlockSpec((tm, D_MODEL), lambda i: (i, 0))
    return pl.pallas_call(
        _resid_norm_mod_kernel,
        grid=(t // tm,),
        in_specs=[
            row,
            row,
            _mod_spec(l_gate, sq, k_gate),
            _lspec(l, 1, D_MODEL),
            _mod_spec(l, sq, k_shift),
            _mod_spec(l, sq, k_scale),
        ],
        out_specs=[row, row],
        out_shape=[jax.ShapeDtypeStruct((t, D_MODEL), F32), jax.ShapeDtypeStruct((t, D_MODEL), BF16)],
        compiler_params=_cp(("arbitrary",)),
        name="resid_norm_mod",
    )(x, y, mod, g, mod, mod)


def _resid_kernel(x_ref, y_ref, gate_ref, xo_ref):
    xo_ref[...] = x_ref[...] + _mod_rows(y_ref[...], gate_ref[...])


def resid(x, y, mod, l, k_gate, tm, is_sample):
    t = x.shape[0]
    sq = _seq_of(tm, is_sample)
    row = pl.BlockSpec((tm, D_MODEL), lambda i: (i, 0))
    return pl.pallas_call(
        _resid_kernel,
        grid=(t // tm,),
        in_specs=[row, row, _mod_spec(l, sq, k_gate)],
        out_specs=row,
        out_shape=jax.ShapeDtypeStruct((t, D_MODEL), F32),
        compiler_params=_cp(("arbitrary",)),
        name="resid",
    )(x, y, mod)


def _mm_kernel(x_ref, w_ref, o_ref):
    o_ref[...] = _dot(x_ref[...], w_ref[...]).astype(o_ref.dtype)


def matmul(x, w, l, tm, tn, out_dtype, name):
    m, k = x.shape
    n = w.shape[2]
    return pl.pallas_call(
        _mm_kernel,
        grid=(n // tn, m // tm),
        in_specs=[
            pl.BlockSpec((tm, k), lambda j, i: (i, 0)),
            pl.BlockSpec((None, k, tn), lambda j, i: (l, 0, j)),
        ],
        out_specs=pl.BlockSpec((tm, tn), lambda j, i: (i, j)),
        out_shape=jax.ShapeDtypeStruct((m, n), out_dtype),
        compiler_params=_cp(("arbitrary", "arbitrary")),
        name=name,
    )(x, w)


def _wq_kernel(x_ref, w_ref, g_ref, o_ref):
    acc = _dot(x_ref[...], w_ref[...])
    g = g_ref[...]
    for c in range(acc.shape[1] // P_HALF):
        a = acc[:, c * P_HALF:(c + 1) * P_HALF]
        o_ref[:, c * P_HALF:(c + 1) * P_HALF] = _rms(a, g).astype(o_ref.dtype)


def peer_query(h2, wq, qn_g, l, tm):
    m = h2.shape[0]
    tn = 512
    return pl.pallas_call(
        _wq_kernel,
        grid=(D_MODEL // tn, m // tm),
        in_specs=[
            pl.BlockSpec((tm, D_MODEL), lambda j, i: (i, 0)),
            pl.BlockSpec((None, D_MODEL, tn), lambda j, i: (l, 0, j)),
            _lspec(l, 1, P_HALF),
        ],
        out_specs=pl.BlockSpec((tm, tn), lambda j, i: (i, j)),
        out_shape=jax.ShapeDtypeStruct((m, D_MODEL), BF16),
        compiler_params=_cp(("arbitrary", "arbitrary")),
        name="peer_query",
    )(h2, wq, qn_g)


def _qk_prep_kernel(zq_ref, zkv_ref, gq_ref, gk_ref, c_ref, s1_ref, s2_ref, qn_ref, kva_ref, kvb_ref):
    cos = c_ref[...]
    s1 = s1_ref[...]
    s2 = s2_ref[...]

    def norm_rope(x, g):
        xn = _rms(x, g)
        return xn * cos + pltpu.roll(xn, C_HD - C_ROT // 2, 1) * s1 + pltpu.roll(xn, C_ROT // 2, 1) * s2

    gq = gq_ref[...]
    for h in range(C_HEADS):
        x = zq_ref[:, h * C_HD:(h + 1) * C_HD].astype(F32)
        qn_ref[:, h * C_HD:(h + 1) * C_HD] = norm_rope(x, gq).astype(qn_ref.dtype)
    for s in range(6):
        for g in range(C_KV_HEADS):
            c0 = (s * C_KV_HEADS + g) * C_HD
            x = zkv_ref[:, c0:c0 + C_HD].astype(F32)
            if s % 2 == 0:
                x = norm_rope(x, gk_ref[s // 2:s // 2 + 1, :])
            if c0 < KVA_W:
                kva_ref[:, c0:c0 + C_HD] = x
            else:
                kvb_ref[:, c0 - KVA_W:c0 - KVA_W + C_HD] = x


def qk_prep(z1, gq, gk, l, rope_c, rope_s1, rope_s2, tm, is_sample):
    t = z1.shape[0]
    per = 1 if is_sample else SEQ // tm
    tab = pl.BlockSpec((tm, C_HD), lambda i: (i % per, 0))
    return pl.pallas_call(
        _qk_prep_kernel,
        grid=(t // tm,),
        in_specs=[
            pl.BlockSpec((tm, C_WIDTH), lambda i: (i, Z_CQ // C_WIDTH)),
            pl.BlockSpec((tm, 1536), lambda i: (i, Z_CKV // 1536)),
            _lspec(l, 1, C_HD),
            _lspec(l, 3, C_HD),
            tab, tab, tab,
        ],
        out_specs=[
            pl.BlockSpec((tm, C_WIDTH), lambda i: (i, 0)),
            pl.BlockSpec((tm, KVA_W), lambda i: (i, 0)),
            pl.BlockSpec((tm, KVB_W), lambda i: (i, 0)),
        ],
        out_shape=[
            jax.ShapeDtypeStruct((t, C_WIDTH), BF16),
            jax.ShapeDtypeStruct((t, KVA_W), F32),
            jax.ShapeDtypeStruct((t, KVB_W), F32),
        ],
        compiler_params=_cp(("arbitrary",)),
        name="qk_prep",
    )(z1, z1, gq, gk, rope_c, rope_s1, rope_s2)


def _chunk_mlp_kernel(au_ref, av_ref, lng_ref, lnb_ref, ws_ref, bst_ref, o_ref):
    u = _gelu(au_ref[...].astype(F32))
    v = _gelu(av_ref[...].astype(F32))
    mu = jnp.mean(v, axis=-1, keepdims=True)
    vc = v - mu
    var = jnp.mean(vc * vc, axis=-1, keepdims=True)
    v = vc * lax.rsqrt(var + EPS) * lng_ref[...] + lnb_ref[...]
    r = lax.broadcasted_iota(jnp.int32, (A_CHUNK, A_CHUNK), 0)
    c = lax.broadcasted_iota(jnp.int32, (A_CHUNK, A_CHUNK), 1)
    tril = r >= c
    tm = u.shape[0]
    for g in range(A_GROUPS):
        w = jnp.where(tril, ws_ref[g], 0.0).astype(BF16)
        bias = bst_ref[:, g:g + 1]
        for ch in range(tm // A_CHUNK):
            rows = slice(ch * A_CHUNK, (ch + 1) * A_CHUNK)
            cols = slice(g * A_GW, (g + 1) * A_GW)
            mixed = _dot(w, v[rows, cols].astype(BF16)) + bias
            o_ref[rows, cols] = (u[rows, cols] * mixed).astype(o_ref.dtype)


def chunk_mlp_prompt(z1, ln_g, ln_b, ws, bs_t, l):
    tm = 512
    return pl.pallas_call(
        _chunk_mlp_kernel,
        grid=(T_P // tm,),
        in_specs=[
            pl.BlockSpec((tm, A_WIDTH), lambda i: (i, Z_AU // A_WIDTH)),
            pl.BlockSpec((tm, A_WIDTH), lambda i: (i, Z_AV // A_WIDTH)),
            _lspec(l, 1, A_WIDTH),
            _lspec(l, 1, A_WIDTH),
            _lspec(l, A_GROUPS, A_CHUNK, A_CHUNK),
            _lspec(l, A_CHUNK, A_GROUPS),
        ],
        out_specs=pl.BlockSpec((tm, A_WIDTH), lambda i: (i, 0)),
        out_shape=jax.ShapeDtypeStruct((T_P, A_WIDTH), BF16),
        compiler_params=_cp(("arbitrary",)),
        name="chunk_mlp",
    )(z1, z1, ln_g, ln_b, ws, bs_t)


def _col_from_row(row):
    n = row.shape[1]
    r = lax.broadcasted_iota(jnp.int32, (n, n), 0)
    c = lax.broadcasted_iota(jnp.int32, (n, n), 1)
    return jnp.sum(jnp.where(r == c, jnp.broadcast_to(row, (n, n)), 0.0), axis=1, keepdims=True)


def _gla_kernel(q_ref, k_ref, v_ref, r_ref, bg_ref, gw_ref, gb_ref, on_ref, o_ref, s_out_ref, s_sc):
    n = pl.program_id(0)

    @pl.when(n == 0)
    def _():
        s_sc[...] = jnp.zeros_like(s_sc)

    c = B_CHUNK
    ri = lax.broadcasted_iota(jnp.int32, (c, c), 0)
    ci = lax.broadcasted_iota(jnp.int32, (c, c), 1)
    tril = ri >= ci
    eye = jnp.where(ri == ci, 1.0, 0.0).astype(BF16)
    ones_tril = jnp.where(tril, 1.0, 0.0).astype(BF16)
    on = on_ref[...]
    for sq in range(BATCH):
        la = _log_sigmoid(_dot(bg_ref[sq].astype(BF16), gw_ref[...]) + gb_ref[...]) * (1.0 / B_GATE_TAU)
        bc = _dot_exact_lhs(ones_tril, la)
        q = q_ref[sq].astype(F32) * (B_DK ** -0.5)
        k = k_ref[sq].astype(F32)
        v = v_ref[sq].astype(BF16)
        gate = r_ref[sq].astype(F32)
        gate = gate * _sigmoid(gate)
        for h in range(B_HEADS):
            dk = slice(h * B_DK, (h + 1) * B_DK)
            dv = slice(h * B_DV, (h + 1) * B_DV)
            b = bc[:, dk]
            qd = (q[:, dk] * jnp.exp(b)).astype(BF16)
            kd = (k[:, dk] * jnp.exp(-b)).astype(BF16)
            att = jnp.where(tril, _dot_nt(qd, kd), 0.0)
            s_old = s_sc[sq, h]
            o = _dot(att.astype(BF16), v[:, dv]) + _dot(qd, s_old.astype(BF16))
            blast = b[c - 1:c, :]
            kdec = (k[:, dk] * jnp.exp(blast - b)).astype(BF16)
            kdec_t = _dot_nt(eye, kdec).astype(BF16)
            s_sc[sq, h] = s_old * _col_from_row(jnp.exp(blast)) + _dot(kdec_t, v[:, dv])
            o_ref[sq, :, dv] = (_rms(o, on) * gate[:, dv]).astype(o_ref.dtype)

    @pl.when(n == pl.num_programs(0) - 1)
    def _():
        s_out_ref[...] = s_sc[...]


def gla_prompt(z1, z2, gw2p, gb, on_g, l):
    c = B_CHUNK
    z1b = z1.reshape(BATCH, SEQ, Z1_W)
    z2b = z2.reshape(BATCH, SEQ, Z2_W)
    blk = lambda w, off: pl.BlockSpec((BATCH, c, w), lambda n: (0, n, off // w))
    state = jax.ShapeDtypeStruct((BATCH, B_HEADS, B_DK, B_DV), F32)
    o_b, s_p = pl.pallas_call(
        _gla_kernel,
        grid=(SEQ // c,),
        in_specs=[
            blk(256, Z_BQ), blk(256, Z_BK), blk(B_WIDTH, Z_BV), blk(B_WIDTH, Z_BR), blk(Z2_W, 0),
            _lspec(l, Z2_W, 256),
            _lspec(l, 1, 256),
            _lspec(l, 1, B_DV),
        ],
        out_specs=[
            pl.BlockSpec((BATCH, c, B_WIDTH), lambda n: (0, n, 0)),
            pl.BlockSpec((BATCH, B_HEADS, B_DK, B_DV), lambda n: (0, 0, 0, 0)),
        ],
        out_shape=[jax.ShapeDtypeStruct((BATCH, SEQ, B_WIDTH), BF16), state],
        scratch_shapes=[pltpu.VMEM((BATCH, B_HEADS, B_DK, B_DV), F32)],
        compiler_params=_cp(("arbitrary",)),
        name="gla",
    )(z1b, z1b, z1b, z1b, z2b, gw2p, gb, on_g)
    return o_b.reshape(T_P, B_WIDTH), s_p


def _compress_kernel(pool_ref, rows_ref, w1_ref, b1_ref, w2_ref, b2_ref, o_ref, *, l):
    kv = pl.program_id(2)
    r = lax.broadcasted_iota(jnp.int32, (N_CMP, SEQ), 0)
    c = lax.broadcasted_iota(jnp.int32, (N_CMP, SEQ), 1)
    d = c - CMP_STRIDE * r
    p = jnp.zeros((N_CMP, SEQ), F32)
    for i in range(CMP_LEN):
        p = jnp.where(d == i, pool_ref[l, kv, i], p)
    rows = rows_ref[...]
    ph, pm, plo = _split3(p)
    rh, rm, rl = _split3(rows)
    pooled = (_dot(ph, rh) + (_dot(ph, rm) + _dot(pm, rh))) + ((_dot(pm, rm) + _dot(ph, rl)) + _dot(plo, rh))
    hid = _gelu(_dot(pooled.astype(BF16), w1_ref[...].astype(BF16)) + b1_ref[...])
    o_ref[...] = _dot(hid.astype(BF16), w2_ref[...].astype(BF16)) + b2_ref[...]


def compress_prompt(kva, pool, w1, b1, w2, b2, l):
    wspec = pl.BlockSpec((None, None, C_HD, C_HD), lambda b, g, kv: (l, kv, 0, 0))
    bspec = pl.BlockSpec((None, None, 1, C_HD), lambda b, g, kv: (l, kv, 0, 0))
    return pl.pallas_call(
        functools.partial(_compress_kernel, l=l),
        grid=(BATCH, C_KV_HEADS, 2),
        in_specs=[
            pl.BlockSpec(memory_space=pltpu.SMEM),
            pl.BlockSpec((SEQ, C_HD), lambda b, g, kv: (b, kv * C_KV_HEADS + g)),
            wspec, bspec, wspec, bspec,
        ],
        out_specs=pl.BlockSpec((None, None, None, N_CMP, C_HD), lambda b, g, kv: (b, g, kv, 0, 0)),
        out_shape=jax.ShapeDtypeStruct((BATCH, C_KV_HEADS, 2, N_CMP, C_HD), F32),
        compiler_params=_cp(("arbitrary", "arbitrary", "arbitrary")),
        name="compress",
    )(pool, kva, w1, b1, w2, b2)


def _softmax_rows(s, valid):
    s = jnp.where(valid, s, NEG)
    m = jnp.max(s, axis=-1, keepdims=True)
    e = jnp.where(valid, jnp.exp(s - m), 0.0)
    return e / jnp.maximum(jnp.sum(e, axis=-1, keepdims=True), 1e-30)


def _cmp_select_kernel(q_ref, ckv_ref, gate_ref, m5_ref, o_ref, selt_ref):
    g = pl.program_id(1)
    qi = pl.program_id(2)
    tq = q_ref.shape[0]
    ck = ckv_ref[0].astype(BF16)
    cv = ckv_ref[1].astype(BF16)
    pos = qi * tq + lax.broadcasted_iota(jnp.int32, (tq, N_CMP), 0)
    j = lax.broadcasted_iota(jnp.int32, (tq, N_CMP), 1)
    valid = j * CMP_STRIDE + (CMP_LEN - 1) <= pos
    gates = gate_ref[...]
    imp = jnp.zeros((tq, N_CMP), F32)
    for r in range(C_REP):
        q = q_ref[:, r * C_HD:(r + 1) * C_HD]
        p = _softmax_rows(_dot_nt(q, ck) * (C_HD ** -0.5), valid)
        imp = imp + p
        o = _dot(p.astype(BF16), cv)
        gcol = _gate_col(gates, Z2_CG + g * C_REP + r)
        o_ref[:, r * C_HD:(r + 1) * C_HD] = (o * _sigmoid(gcol)).astype(o_ref.dtype)
    nslc = SEQ // SLC_BLK
    hi, mid, lo = _split3(imp)
    m5t = m5_ref[...]
    score = ((_dot_nt(m5t, hi) + _dot_nt(m5t, mid)) + _dot_nt(m5t, lo))[0:nslc, :]
    jt = lax.broadcasted_iota(jnp.int32, (nslc, tq), 0)
    post = qi * tq + lax.broadcasted_iota(jnp.int32, (nslc, tq), 1)
    cur = post // SLC_BLK
    forced = (jt == 0) | (jt == cur) | (jt == cur - 1)
    score = jnp.where(jt * SLC_BLK <= post, score + jnp.where(forced, FORCE_BONUS, 0.0), NEG)
    rank = jnp.zeros((nslc, tq), F32)
    for jj in range(nslc):
        row = score[jj:jj + 1, :]
        ahead = (row > score) | ((row == score) & (jt > jj))
        rank = rank + jnp.where(ahead, 1.0, 0.0)
    sel = (rank < SLC_TOPK) & (score > 0.5 * NEG)
    selt_ref[0:nslc, :] = jnp.where(sel, 1.0, 0.0).astype(selt_ref.dtype)
    selt_ref[nslc:, :] = jnp.zeros((N_CMP - nslc, tq), selt_ref.dtype)


def cmp_select_prompt(qn, ckv, z2, m5):
    tq = 512
    nq = SEQ // tq
    return pl.pallas_call(
        _cmp_select_kernel,
        grid=(BATCH, C_KV_HEADS, nq),
        in_specs=[
            pl.BlockSpec((tq, C_REP * C_HD), lambda b, g, i: (b * nq + i, g)),
            pl.BlockSpec((None, None, 2, N_CMP, C_HD), lambda b, g, i: (b, g, 0, 0, 0)),
            pl.BlockSpec((tq, Z2_W), lambda b, g, i: (b * nq + i, 0)),
            pl.BlockSpec((N_CMP, N_CMP), lambda b, g, i: (0, 0)),
        ],
        out_specs=[
            pl.BlockSpec((tq, C_REP * C_HD), lambda b, g, i: (b * nq + i, g)),
            pl.BlockSpec((None, None, N_CMP, tq), lambda b, g, i: (b, g, 0, i)),
        ],
        out_shape=[
            jax.ShapeDtypeStruct((T_P, C_WIDTH), BF16),
            jax.ShapeDtypeStruct((BATCH, C_KV_HEADS, N_CMP, SEQ), BF16),
        ],
        compiler_params=_cp(("arbitrary", "arbitrary", "arbitrary")),
        name="cmp_select",
    )(qn, ckv, z2, m5)


def _flash_kernel(q_ref, k_ref, v_ref, selt_ref, gate_ref, o_ref, vt_sc, m_sc, l_sc, acc_sc, *, branch, tk):
    g = pl.program_id(1)
    qi = pl.program_id(2)
    tq = q_ref.shape[0]

    @pl.when(qi == 0)
    def _():
        for jt in range(SEQ // tk):
            vt_sc[jt] = v_ref[jt * tk:(jt + 1) * tk, :].T

    q = jnp.concatenate([q_ref[:, r * C_HD:(r + 1) * C_HD] for r in range(C_REP)], axis=0)
    m_sc[...] = jnp.full(m_sc.shape, NEG, F32)
    l_sc[...] = jnp.zeros(l_sc.shape, F32)
    acc_sc[...] = jnp.zeros(acc_sc.shape, F32)
    pos = qi * tq + lax.broadcasted_iota(jnp.int32, (tk, tq), 1)
    hi = ((qi + 1) * tq + tk - 1) // tk
    if branch == 1:
        lo = 0
        selt = selt_ref[...]
    else:
        lo = jnp.maximum(qi * tq - WINDOW, 0) // tk

    def step(kj, carry):
        k0 = pl.multiple_of(kj * tk, tk)
        k = k_ref[pl.ds(k0, tk), :].astype(BF16)
        vt = vt_sc[kj].astype(BF16)
        key = k0 + lax.broadcasted_iota(jnp.int32, (tk, tq), 0)
        if branch == 1:
            kb = (k0 + lax.broadcasted_iota(jnp.int32, (tk, N_CMP), 0)) // SLC_BLK
            expand = jnp.where(kb == lax.broadcasted_iota(jnp.int32, (tk, N_CMP), 1), 1.0, 0.0).astype(BF16)
            valid = (_dot(expand, selt) > 0.5) & (key <= pos)
        else:
            valid = (key <= pos) & (key > pos - WINDOW)
        valid4 = jnp.concatenate([valid] * C_REP, axis=1)
        s = jnp.where(valid4, _dot_nt(k, q) * (C_HD ** -0.5), NEG)
        m_old = m_sc[...]
        m_new = jnp.maximum(m_old, jnp.max(s, axis=0, keepdims=True))
        alpha = jnp.exp(m_old - m_new)
        p = jnp.where(valid4, jnp.exp(s - m_new), 0.0)
        l_sc[...] = alpha * l_sc[...] + jnp.sum(p, axis=0, keepdims=True)
        acc_sc[...] = alpha * acc_sc[...] + _dot(vt, p.astype(BF16))
        m_sc[...] = m_new
        return carry

    lax.fori_loop(lo, hi, step, 0)
    out = (acc_sc[...] / jnp.maximum(l_sc[...], 1e-30)).T
    gates = gate_ref[...]
    for r in range(C_REP):
        gcol = _gate_col(gates, Z2_CG + branch * C_HEADS + g * C_REP + r)
        o_ref[:, r * C_HD:(r + 1) * C_HD] = (out[r * tq:(r + 1) * tq] * _sigmoid(gcol)).astype(o_ref.dtype)


def flash_prompt(qn, kv, selt, z2, branch):
    tq = 256
    tk = 256
    nq = SEQ // tq
    kcol = 2 * C_KV_HEADS if branch == 1 else 0
    kern = functools.partial(_flash_kernel, branch=branch, tk=tk)
    return pl.pallas_call(
        kern,
        grid=(BATCH, C_KV_HEADS, nq),
        in_specs=[
            pl.BlockSpec((tq, C_REP * C_HD), lambda b, g, i: (b * nq + i, g)),
            pl.BlockSpec((SEQ, C_HD), lambda b, g, i: (b, kcol + g)),
            pl.BlockSpec((SEQ, C_HD), lambda b, g, i: (b, kcol + C_KV_HEADS + g)),
            pl.BlockSpec((None, None, N_CMP, tq), lambda b, g, i: (b, g, 0, i)),
            pl.BlockSpec((tq, Z2_W), lambda b, g, i: (b * nq + i, 0)),
        ],
        out_specs=pl.BlockSpec((tq, C_REP * C_HD), lambda b, g, i: (b * nq + i, g)),
        out_shape=jax.ShapeDtypeStruct((T_P, C_WIDTH), BF16),
        scratch_shapes=[
            pltpu.VMEM((SEQ // tk, C_HD, tk), F32),
            pltpu.VMEM((1, C_REP * tq), F32),
            pltpu.VMEM((1, C_REP * tq), F32),
            pltpu.VMEM((C_HD, C_REP * tq), F32),
        ],
        compiler_params=_cp(("arbitrary", "arbitrary", "arbitrary")),
        name="flash_slc" if branch == 1 else "flash_win",
    )(qn, kv, kv, selt, z2)


def _merge_kernel(oa_ref, ob_ref, oc0_ref, oc1_ref, oc2_ref, ga_ref, gb_ref, gc_ref, wa_ref, wb_ref, wc_ref, o_ref, oc_sc):
    @pl.when(pl.program_id(1) == 0)
    def _():
        oc = oc0_ref[...].astype(F32) + oc1_ref[...].astype(F32) + oc2_ref[...].astype(F32)
        oc_sc[...] = oc.astype(BF16)

    m = _sigmoid(ga_ref[...].astype(F32)) * _dot(oa_ref[...], wa_ref[...])
    m = m + _sigmoid(gb_ref[...].astype(F32)) * _dot(ob_ref[...], wb_ref[...])
    m = m + _sigmoid(gc_ref[...].astype(F32)) * _dot(oc_sc[...], wc_ref[...])
    o_ref[...] = m.astype(o_ref.dtype)


def merge(o_a, o_b, oc0, oc1, oc2, z1, wa, wb, wc, l, tm):
    t = o_a.shape[0]
    tn = 512
    nj = D_MODEL // tn
    gate = lambda k: pl.BlockSpec((tm, tn), lambda i, j: (i, (Z_MG + k * D_MODEL) // tn + j))
    wspec = lambda kdim: pl.BlockSpec((None, kdim, tn), lambda i, j: (l, 0, j))
    return pl.pallas_call(
        _merge_kernel,
        grid=(t // tm, nj),
        in_specs=[
            pl.BlockSpec((tm, A_WIDTH), lambda i, j: (i, 0)),
            pl.BlockSpec((tm, B_WIDTH), lambda i, j: (i, 0)),
            pl.BlockSpec((tm, C_WIDTH), lambda i, j: (i, 0)),
            pl.BlockSpec((tm, C_WIDTH), lambda i, j: (i, 0)),
            pl.BlockSpec((tm, C_WIDTH), lambda i, j: (i, 0)),
            gate(0), gate(1), gate(2),
            wspec(A_WIDTH), wspec(B_WIDTH), wspec(C_WIDTH),
        ],
        out_specs=pl.BlockSpec((tm, tn), lambda i, j: (i, j)),
        out_shape=jax.ShapeDtypeStruct((t, D_MODEL), BF16),
        scratch_shapes=[pltpu.VMEM((tm, C_WIDTH), BF16)],
        compiler_params=_cp(("arbitrary", "arbitrary")),
        name="merge",
    )(o_a, o_b, oc0, oc1, oc2, z1, z1, z1, wa, wb, wc)


N_CELLS = 50
N_CELL_ROWS = 56


def _cell_tables():
    sa = np.zeros((N_CELL_ROWS, P_TOPK), np.float32)
    sb = np.zeros((N_CELL_ROWS, P_TOPK), np.float32)
    r = 0
    for a in range(P_TOPK):
        for b in range(P_TOPK // (a + 1)):
            sa[r, a] = 1.0
            sb[r, b] = 1.0
            r += 1
    assert r == N_CELLS
    return sa, sb


def _top16_cols(s, row_iota, exact, want_rank=True):
    track = exact or want_rank
    rank = jnp.full(s.shape, float(P_TOPK), F32) if track else None
    work = s
    top = jnp.zeros((P_TOPK, s.shape[1]), F32)
    r_iota = lax.broadcasted_iota(jnp.int32, top.shape, 0)
    for r in range(P_TOPK):
        m = jnp.max(work, axis=0, keepdims=True)
        if exact:
            idx = jnp.min(jnp.where(work == m, row_iota, 1.0e9), axis=0, keepdims=True)
            hit = row_iota == idx
        else:
            hit = work == m
        if track:
            rank = jnp.where(hit, float(r), rank)
        work = jnp.where(hit, -jnp.inf, work)
        top = jnp.where(r_iota == r, m, top)
    taken = (rank < float(P_TOPK)) if track else (s >= top[P_TOPK - 1:P_TOPK, :])
    n_out = jnp.sum(jnp.where(taken, 1.0, 0.0), axis=0, keepdims=True)
    return rank, top, n_out


def _peer_topk_kernel(qn_ref, sk_ref, sa_ref, sb_ref, sat_ref, r2_ref, cnt_ref, e1_ref, e2_ref):
    tb = qn_ref.shape[0]
    lanes = 128
    sk1 = sk_ref[0].astype(BF16)
    sk2 = sk_ref[1].astype(BF16)
    s1_all = _dot_nt(sk1, qn_ref[:, 0:P_HALF])
    s2_all = _dot_nt(sk2, qn_ref[:, P_HALF:2 * P_HALF])
    n_iota = lax.broadcasted_iota(jnp.int32, (P_NKEYS, lanes), 0).astype(F32)
    c_iota = lax.broadcasted_iota(jnp.int32, (N_CELL_ROWS, lanes), 0).astype(F32)
    sa = sa_ref[...]
    sb = sb_ref[...]
    sat = sat_ref[...]

    def slab(t, exact):
        cols = slice(t * lanes, (t + 1) * lanes)
        s1 = s1_all[:, cols]
        s2 = s2_all[:, cols]
        rank1, top1, n1 = _top16_cols(s1, n_iota, exact, want_rank=False)
        rank2, top2, n2 = _top16_cols(s2, n_iota, exact)
        cand = _dot_exact_lhs(sa, top1) + _dot_exact_lhs(sb, top2)
        cand = jnp.where(c_iota < N_CELLS, cand, -jnp.inf)
        rankc, topc, nc = _top16_cols(cand, c_iota, exact, want_rank=False)
        if exact:
            in1 = rank1 < float(P_TOPK)
            picked = rankc < float(P_TOPK)
        else:
            in1 = s1 >= top1[P_TOPK - 1:P_TOPK, :]
            picked = cand >= topc[P_TOPK - 1:P_TOPK, :]
        zsum = jnp.sum(jnp.exp(topc - topc[0:1, :]), axis=0, keepdims=True)
        cnt = _dot(sat, jnp.where(picked, 1.0, 0.0).astype(BF16))
        cntd = jnp.zeros(s1.shape, F32)
        for a in range(P_TOPK):
            is_a = (rank1 == float(a)) if exact else (s1 == top1[a:a + 1, :])
            cntd = jnp.where(is_a, cnt[a:a + 1, :], cntd)
        e1 = jnp.where(in1, jnp.exp(s1 - top1[0:1, :]), 0.0) / zsum
        e2 = jnp.exp(s2 - top2[0:1, :])
        r2_ref[:, cols] = rank2.astype(r2_ref.dtype)
        cnt_ref[:, cols] = cntd
        e1_ref[:, cols] = e1
        e2_ref[:, cols] = e2.astype(e2_ref.dtype)
        return jnp.abs(n1 - P_TOPK) + jnp.abs(n2 - P_TOPK) + jnp.abs(nc - P_TOPK)

    tied = [jnp.max(slab(t, False)) > 0.0 for t in range(tb // lanes)]
    for t in range(tb // lanes):
        @pl.when(tied[t])
        def _():
            slab(t, True)


def peer_topk(qn, subkeys, l, tb):
    t = qn.shape[0]
    sa, sb = _cell_tables()
    out = lambda dt: jax.ShapeDtypeStruct((P_HEADS, P_NKEYS, t), dt)
    ospec = pl.BlockSpec((None, P_NKEYS, tb), lambda i, h: (h, 0, i))
    return pl.pallas_call(
        _peer_topk_kernel,
        grid=(t // tb, P_HEADS),
        in_specs=[
            pl.BlockSpec((tb, 2 * P_HALF), lambda i, h: (i, h)),
            pl.BlockSpec((None, None, 2, P_NKEYS, P_HALF), lambda i, h: (l, h, 0, 0, 0)),
            pl.BlockSpec((N_CELL_ROWS, P_TOPK), lambda i, h: (0, 0)),
            pl.BlockSpec((N_CELL_ROWS, P_TOPK), lambda i, h: (0, 0)),
            pl.BlockSpec((P_TOPK, N_CELL_ROWS), lambda i, h: (0, 0)),
        ],
        out_specs=[ospec, ospec, ospec, ospec],
        out_shape=[out(BF16), out(F32), out(F32), out(BF16)],
        compiler_params=_cp(("arbitrary", "arbitrary")),
        name="peer_topk",
    )(qn, subkeys, jnp.asarray(sa, BF16), jnp.asarray(sb, BF16), jnp.asarray(sa.T, BF16))


PEER_EC = 512
PEER_NGRP = P_NEXP // PEER_EC


def _peer_main_kernel(h_ref, ua_ref, ub_ref, va_ref, vb_ref, r2_ref, cnt_ref, e1_ref, e2_ref, y_ref, pa_sc, pb_sc):
    e = pl.program_id(1)
    last = pl.num_programs(1) - 1
    nchunk = PEER_EC // P_NKEYS

    @pl.when(e == 0)
    def _():
        y_ref[...] = jnp.zeros_like(y_ref)
        pb_sc[...] = jnp.zeros_like(pb_sc)

    def gated(scores, grp):
        hid = _gelu(scores.astype(BF16))
        zero = jnp.zeros((), BF16)
        parts = []
        for cc in range(nchunk):
            c = grp * nchunk + cc
            w = None
            for h in range(P_HEADS):
                e1 = e1_ref[h, pl.ds(c, 1), :].astype(BF16)
                cn = cnt_ref[h, pl.ds(c, 1), :].astype(BF16)
                term = jnp.where(r2_ref[h] < cn, e2_ref[h], zero) * e1
                w = term if w is None else w + term
            parts.append(w * hid[cc * P_NKEYS:(cc + 1) * P_NKEYS, :])
        return jnp.concatenate(parts, axis=0).T

    @pl.when(e < last)
    def _():
        sa = _dot_nt(ua_ref[...], h_ref[...])
        y_ref[...] += _dot(pb_sc[...], va_ref[...])
        sb = _dot_nt(ub_ref[...], h_ref[...])
        pa_sc[...] = gated(sa, 2 * e)
        y_ref[...] += _dot(pa_sc[...], vb_ref[...])
        pb_sc[...] = gated(sb, 2 * e + 1)

    @pl.when(e == last)
    def _():
        y_ref[...] += _dot(pb_sc[...], va_ref[...])


def peer_main(h2, u_bf, v_bf, r2, cnt, e1, e2, l, tb):
    t = h2.shape[0]
    nstep = PEER_NGRP // 2 + 1
    gspec = pl.BlockSpec((P_HEADS, P_NKEYS, tb), lambda i, e: (0, 0, i))
    grp = lambda f: pl.BlockSpec((None, PEER_EC, D_MODEL), lambda i, e: (l, jnp.clip(f(e), 0, PEER_NGRP - 1), 0))
    return pl.pallas_call(
        _peer_main_kernel,
        grid=(t // tb, nstep),
        in_specs=[
            pl.BlockSpec((tb, D_MODEL), lambda i, e: (i, 0)),
            grp(lambda e: 2 * e), grp(lambda e: 2 * e + 1),
            grp(lambda e: 2 * e - 1), grp(lambda e: 2 * e),
            gspec, gspec, gspec, gspec,
        ],
        out_specs=pl.BlockSpec((tb, D_MODEL), lambda i, e: (i, 0)),
        out_shape=jax.ShapeDtypeStruct((t, D_MODEL), F32),
        scratch_shapes=[pltpu.VMEM((tb, PEER_EC), BF16), pltpu.VMEM((tb, PEER_EC), BF16)],
        compiler_params=_cp(("arbitrary", "arbitrary")),
        name="peer_main",
    )(h2, u_bf, u_bf, v_bf, v_bf, r2, cnt, e1, e2)


N_PAGES = PAST_LEN // PAGE_SIZE
N_SEG = PAST_LEN // CMP_STRIDE
PAGES_PER_STEP = 8
SEG_PER_STEP = PAGES_PER_STEP * PAGE_SIZE // CMP_STRIDE
N_PAST_BLK = PAST_LEN // SLC_BLK
SLC_LANES = 384
S_ROWS = 16


def _row_of(x, b):
    r = lax.broadcasted_iota(jnp.int32, x.shape, 0)
    return jnp.sum(jnp.where(r == b, x, 0.0), axis=0, keepdims=True)


def _sample_pool_kernel(pt_ref, *refs):
    pages = refs[:PAGES_PER_STEP]
    wf_ref, ws_ref, o_ref, carry_sc = refs[PAGES_PER_STEP:]
    pc = pl.program_id(2)

    @pl.when(pc == 0)
    def _():
        carry_sc[...] = jnp.zeros_like(carry_sc)

    wf = wf_ref[...]
    ws = ws_ref[...]
    fs, ss = [], []
    for r in pages:
        xs = r[...].reshape(PAGE_SIZE // CMP_STRIDE, CMP_STRIDE, 2, C_KV_HEADS, C_HD)
        fs.append(jnp.sum(xs * wf[None], axis=1))
        ss.append(jnp.sum(xs * ws[None], axis=1))
    first = jnp.concatenate(fs, axis=0)
    second = jnp.concatenate(ss, axis=0)
    o_ref[...] = jnp.concatenate([carry_sc[...], first[:SEG_PER_STEP - 1]], axis=0) + second
    carry_sc[...] = first[SEG_PER_STEP - 1:]


def sample_pool(cache, pt_flat, wf_all, ws_all):
    def page(k):
        return pl.BlockSpec((None, None, PAGE_SIZE, 2, C_KV_HEADS, C_HD),
                            lambda l, b, pc, pt: (l, pt[b * N_PAGES + pc * PAGES_PER_STEP + k], 0, 0, 0, 0))

    wspec = pl.BlockSpec((None, CMP_STRIDE, 2, C_KV_HEADS, C_HD), lambda l, b, pc, pt: (l, 0, 0, 0, 0))
    grid_spec = pltpu.PrefetchScalarGridSpec(
        num_scalar_prefetch=1,
        grid=(DEPTH, DEC_BATCH, N_PAGES // PAGES_PER_STEP),
        in_specs=[page(k) for k in range(PAGES_PER_STEP)] + [wspec, wspec],
        out_specs=pl.BlockSpec((None, None, SEG_PER_STEP, 2, C_KV_HEADS, C_HD), lambda l, b, pc, pt: (l, b, pc, 0, 0, 0)),
        scratch_shapes=[pltpu.VMEM((1, 2, C_KV_HEADS, C_HD), F32)],
    )
    return pl.pallas_call(
        _sample_pool_kernel,
        grid_spec=grid_spec,
        out_shape=jax.ShapeDtypeStruct((DEPTH, DEC_BATCH, N_SEG, 2, C_KV_HEADS, C_HD), F32),
        compiler_params=_cp(("arbitrary", "arbitrary", "arbitrary")),
        name="sample_pool",
    )(pt_flat, *([cache] * PAGES_PER_STEP), wf_all, ws_all)


def _sample_cmp_mlp_kernel(x_ref, w1_ref, b1_ref, w2_ref, b2_ref, o_ref):
    for kv in range(2):
        w1 = w1_ref[kv].astype(BF16)
        w2 = w2_ref[kv].astype(BF16)
        for g in range(C_KV_HEADS):
            hid = _gelu(_dot(x_ref[:, kv, g, :].astype(BF16), w1) + b1_ref[kv])
            o_ref[kv * C_KV_HEADS + g] = _dot(hid.astype(BF16), w2) + b2_ref[kv]


def sample_cmp_mlp(pooled, w1, b1, w2, b2):
    wspec = pl.BlockSpec((None, 2, C_HD, C_HD), lambda l, b: (l, 0, 0, 0))
    bspec = pl.BlockSpec((None, 2, 1, C_HD), lambda l, b: (l, 0, 0, 0))
    return pl.pallas_call(
        _sample_cmp_mlp_kernel,
        grid=(DEPTH, DEC_BATCH),
        in_specs=[pl.BlockSpec((None, None, N_SEG, 2, C_KV_HEADS, C_HD), lambda l, b: (l, b, 0, 0, 0, 0)),
                  wspec, bspec, wspec, bspec],
        out_specs=pl.BlockSpec((None, None, 2 * C_KV_HEADS, N_SEG, C_HD), lambda l, b: (l, b, 0, 0, 0)),
        out_shape=jax.ShapeDtypeStruct((DEPTH, DEC_BATCH, 2 * C_KV_HEADS, N_SEG, C_HD), F32),
        compiler_params=_cp(("arbitrary", "arbitrary")),
        name="sample_cmp_mlp",
    )(pooled, w1, b1, w2, b2)


def _sample_cmp_kernel(q_ref, ckv_ref, gate_ref, m5_ref, o_ref, idx_ref, ok_ref):
    b = pl.program_id(0)
    qrow = _row_of(q_ref[...].astype(F32), b)
    grow = _row_of(gate_ref[...], b)
    m_i = lax.broadcasted_iota(jnp.int32, (8, N_SEG), 1)
    valid = (m_i >= 1) & ((m_i - 1) * CMP_STRIDE + (CMP_LEN - 1) <= PAST_LEN)
    lane = lax.broadcasted_iota(jnp.int32, (1, Z2_W), 1)
    j = lax.broadcasted_iota(jnp.int32, (8, SLC_LANES), 1)
    jf = j.astype(F32)
    cur = PAST_LEN // SLC_BLK
    forced = (j == 0) | (j == cur) | (j == cur - 1)
    idx_row = jnp.zeros((1, Z2_W), F32)
    ok_row = jnp.zeros((1, Z2_W), F32)
    for g in range(C_KV_HEADS):
        ck = ckv_ref[g].astype(BF16)
        cv = ckv_ref[C_KV_HEADS + g].astype(BF16)
        imp = jnp.zeros((8, N_SEG), F32)
        for r in range(C_REP):
            h = g * C_REP + r
            q8 = jnp.broadcast_to(qrow[:, h * C_HD:(h + 1) * C_HD], (8, C_HD)).astype(BF16)
            p = _softmax_rows(_dot_nt(q8, ck) * (C_HD ** -0.5), valid)
            imp = imp + p
            o = _dot(p.astype(BF16), cv)
            gate = jnp.sum(jnp.where(lane == Z2_CG + h, grow, 0.0), axis=1, keepdims=True)
            o_ref[:, h * C_HD:(h + 1) * C_HD] = o[0:1] * _sigmoid(gate)
        score = _dot_exact_rhs(imp, m5_ref[...])
        score = jnp.where(j * SLC_BLK <= PAST_LEN, score + jnp.where(forced, FORCE_BONUS, 0.0), -3.0e38)
        for r in range(SLC_TOPK):
            m = jnp.max(score, axis=1, keepdims=True)
            idx = jnp.min(jnp.where(score == m, jf, 1.0e9), axis=1, keepdims=True)
            slot = lane == g * SLC_TOPK + r
            idx_row = jnp.where(slot, idx[0:1], idx_row)
            ok_row = jnp.where(slot, jnp.where(m[0:1] > 0.5 * NEG, 1.0, 0.0), ok_row)
            score = jnp.where(jf == idx, -3.4e38, score)
    idx_ref[...] = idx_row.astype(jnp.int32)
    ok_ref[...] = ok_row.astype(jnp.int32)


def sample_cmp(qn, ckv_all, z2, m5s, l):
    small = lambda n, dt: jax.ShapeDtypeStruct((DEC_BATCH, 1, n), dt)
    ospec = lambda n: pl.BlockSpec((None, 1, n), lambda b: (b, 0, 0))
    return pl.pallas_call(
        _sample_cmp_kernel,
        grid=(DEC_BATCH,),
        in_specs=[
            pl.BlockSpec((S_ROWS, C_WIDTH), lambda b: (0, 0)),
            pl.BlockSpec((None, None, 2 * C_KV_HEADS, N_SEG, C_HD), lambda b: (l, b, 0, 0, 0)),
            pl.BlockSpec((S_ROWS, Z2_W), lambda b: (0, 0)),
            pl.BlockSpec((N_SEG, SLC_LANES), lambda b: (0, 0)),
        ],
        out_specs=[ospec(C_WIDTH), ospec(Z2_W), ospec(Z2_W)],
        out_shape=[small(C_WIDTH, F32), small(Z2_W, jnp.int32), small(Z2_W, jnp.int32)],
        compiler_params=_cp(("arbitrary",)),
        name="sample_cmp",
    )(qn, ckv_all, z2, m5s)


def _sample_attend_kernel(phys_ref, rb_ref, idx_ref, ok_ref, *refs):
    nblk = C_KV_HEADS * SLC_TOPK
    blks = refs[:nblk]
    q_ref, kva_ref, kvb_ref, win_ref, gate_ref, oslc_ref, owin_ref, kcat_sc, vcat_sc = refs[nblk:]
    b = pl.program_id(0)
    scale = C_HD ** -0.5
    qall = _row_of(q_ref[...].astype(F32), b)
    kva_row = _row_of(kva_ref[...], b)
    kvb_row = _row_of(kvb_ref[...], b)
    grow = _row_of(gate_ref[...], b)
    lane_g = lax.broadcasted_iota(jnp.int32, grow.shape, 1)
    r8 = lax.broadcasted_iota(jnp.int32, (8, C_HD), 0)
    r1 = lax.broadcasted_iota(jnp.int32, (8, 1), 0)
    r64 = lax.broadcasted_iota(jnp.int32, (SLC_BLK, C_HD), 0)
    lane = lax.broadcasted_iota(jnp.int32, (8, SLC_TOPK * SLC_BLK), 1)
    kp = PAST_LEN - WINDOW + lax.broadcasted_iota(jnp.int32, (8, WINDOW), 1)
    validw = (kp <= PAST_LEN) & (kp > PAST_LEN - WINDOW)
    seg = lambda row, i: row[:, i * C_HD:(i + 1) * C_HD]
    for g in range(C_KV_HEADS):
        base = (b * C_KV_HEADS + g) * SLC_TOPK
        q4 = jnp.zeros((8, C_HD), F32)
        for r in range(C_REP):
            q4 = jnp.where(r8 == r, jnp.broadcast_to(seg(qall, g * C_REP + r), (8, C_HD)), q4)
        q4 = q4.astype(BF16)
        newk = jnp.where(r64 == 0, jnp.broadcast_to(seg(kva_row, 2 * C_KV_HEADS + g), (SLC_BLK, C_HD)), 0.0)
        newv = jnp.where(r64 == 0, jnp.broadcast_to(seg(kva_row, 3 * C_KV_HEADS + g), (SLC_BLK, C_HD)), 0.0)
        rowpos = lane % SLC_BLK
        okv = jnp.zeros(lane.shape, jnp.int32)
        for k in range(SLC_TOPK):
            idx_k = idx_ref[base + k]
            is_new = jnp.full((SLC_BLK, C_HD), idx_k, jnp.int32) >= N_PAST_BLK
            blk = blks[g * SLC_TOPK + k]
            kcat_sc[k * SLC_BLK:(k + 1) * SLC_BLK, :] = jnp.where(is_new, newk, blk[:, 0, g, :])
            vcat_sc[k * SLC_BLK:(k + 1) * SLC_BLK, :] = jnp.where(is_new, newv, blk[:, 1, g, :])
            in_k = lane // SLC_BLK == k
            rowpos = jnp.where(in_k, rowpos + idx_k * SLC_BLK, rowpos)
            okv = jnp.where(in_k, ok_ref[base + k], okv)
        s = _dot_nt(q4, kcat_sc[...].astype(BF16)) * scale
        p = _softmax_rows(s, (okv > 0) & (rowpos <= PAST_LEN))
        o_slc = _dot(p.astype(BF16), vcat_sc[...].astype(BF16))
        sw = jnp.where(validw, _dot_nt(q4, win_ref[:, 0, g, :].astype(BF16)) * scale, NEG)
        knew = seg(kvb_row, g).astype(BF16).astype(F32)
        vnew = seg(kvb_row, C_KV_HEADS + g)
        sn = jnp.sum(q4.astype(F32) * knew, axis=1, keepdims=True) * scale
        m = jnp.maximum(jnp.max(sw, axis=1, keepdims=True), sn)
        ew = jnp.where(validw, jnp.exp(sw - m), 0.0)
        en = jnp.exp(sn - m)
        den = jnp.maximum(jnp.sum(ew, axis=1, keepdims=True) + en, 1e-30)
        o_win = (_dot(ew.astype(BF16), win_ref[:, 1, g, :].astype(BF16)) + en * vnew) / den
        for branch, o, o_ref in ((1, o_slc, oslc_ref), (2, o_win, owin_ref)):
            gcol = jnp.zeros((8, 1), F32)
            for r in range(C_REP):
                col = Z2_CG + branch * C_HEADS + g * C_REP + r
                gcol = jnp.where(r1 == r, jnp.sum(jnp.where(lane_g == col, grow, 0.0), axis=1, keepdims=True), gcol)
            o_ref[g] = o * _sigmoid(gcol)


def sample_attend(cache, win_buf, phys, rb, idx, ok, qn, kva, kvb, z2, l):
    nblk = C_KV_HEADS * SLC_TOPK

    def cblk(j):
        return pl.BlockSpec((None, None, SLC_BLK, 2, C_KV_HEADS, C_HD),
                            lambda b, ph, rbr, ix, okr: (l, ph[b * nblk + j], rbr[b * nblk + j], 1, 0, 0))

    rows = lambda w: pl.BlockSpec((S_ROWS, w), lambda b, *_: (0, 0))
    ospec = pl.BlockSpec((None, C_KV_HEADS, 8, C_HD), lambda b, *_: (b, 0, 0, 0))
    grid_spec = pltpu.PrefetchScalarGridSpec(
        num_scalar_prefetch=4,
        grid=(DEC_BATCH,),
        in_specs=[cblk(j) for j in range(nblk)] + [
            rows(C_WIDTH), rows(KVA_W), rows(KVB_W),
            pl.BlockSpec((None, None, WINDOW, 2, C_KV_HEADS, C_HD), lambda b, *_: (l, b, 0, 0, 0, 0)),
            rows(Z2_W),
        ],
        out_specs=[ospec, ospec],
        scratch_shapes=[pltpu.VMEM((SLC_TOPK * SLC_BLK, C_HD), F32), pltpu.VMEM((SLC_TOPK * SLC_BLK, C_HD), F32)],
    )
    out = jax.ShapeDtypeStruct((DEC_BATCH, C_KV_HEADS, 8, C_HD), F32)
    return pl.pallas_call(
        _sample_attend_kernel,
        grid_spec=grid_spec,
        out_shape=[out, out],
        compiler_params=_cp(("arbitrary",)),
        name="sample_attend",
    )(phys, rb, idx, ok, *([cache] * nblk), qn, kva, kvb, win_buf, z2)


def _sample_ab_kernel(au_ref, av_ref, bq_ref, bk_ref, bv_ref, br_ref, z2_ref, lng_ref, lnb_ref, w00_ref, b0_ref,
                      gw_ref, gb_ref, on_ref, s0_ref, oa_ref, ob_ref, av_out_ref, s_ref):
    u = _gelu(au_ref[...].astype(F32))
    v = _gelu(av_ref[...].astype(F32))
    mu = jnp.mean(v, axis=-1, keepdims=True)
    vc = v - mu
    v = vc * lax.rsqrt(jnp.mean(vc * vc, axis=-1, keepdims=True) + EPS) * lng_ref[...] + lnb_ref[...]
    av_out_ref[...] = v
    oa_ref[...] = u * (v * w00_ref[...] + b0_ref[...])
    la = _log_sigmoid(_dot(z2_ref[...].astype(BF16), gw_ref[...]) + gb_ref[...]) * (1.0 / B_GATE_TAU)
    q = bq_ref[...].astype(F32) * (B_DK ** -0.5)
    k = bk_ref[...].astype(F32)
    vv = bv_ref[...].astype(F32)
    gate = br_ref[...].astype(F32)
    gate = gate * _sigmoid(gate)
    qd = q * jnp.exp(la)
    kd = k * jnp.exp(-la)
    on = on_ref[...]
    ob_ref[...] = jnp.zeros_like(ob_ref)
    for b in range(DEC_BATCH):
        for h in range(B_HEADS):
            dk = slice(h * B_DK, (h + 1) * B_DK)
            dv = slice(h * B_DV, (h + 1) * B_DV)
            qd_r = qd[b:b + 1, dk]
            v_r = vv[b:b + 1, dv]
            att = jnp.sum(qd_r * kd[b:b + 1, dk], axis=1, keepdims=True)
            s0 = s0_ref[b, h]
            o = att * v_r + _dot(jnp.broadcast_to(qd_r, (8, B_DK)).astype(BF16), s0.astype(BF16))[0:1]
            s_ref[b, h] = s0 * _col_from_row(jnp.exp(la[b:b + 1, dk])) + _col_from_row(k[b:b + 1, dk]) * v_r
            ob_ref[b:b + 1, dv] = _rms(o, on) * gate[b:b + 1, dv]


def sample_ab(z1, z2, lng, lnb, w00, b0, gw2p, gb, on_g, state_gla, l):
    blk = lambda w, off: pl.BlockSpec((S_ROWS, w), lambda i: (0, off // w))
    row = lambda n: jax.ShapeDtypeStruct((S_ROWS, n), F32)
    rspec = lambda n: pl.BlockSpec((S_ROWS, n), lambda i: (0, 0))
    sspec = pl.BlockSpec((None, DEC_BATCH, B_HEADS, B_DK, B_DV), lambda i: (l, 0, 0, 0, 0))
    return pl.pallas_call(
        _sample_ab_kernel,
        grid=(1,),
        in_specs=[
            blk(A_WIDTH, Z_AU), blk(A_WIDTH, Z_AV), blk(256, Z_BQ), blk(256, Z_BK), blk(B_WIDTH, Z_BV),
            blk(B_WIDTH, Z_BR), blk(Z2_W, 0),
            _lspec(l, 1, A_WIDTH), _lspec(l, 1, A_WIDTH), _lspec(l, 1, A_WIDTH), _lspec(l, 1, A_WIDTH),
            _lspec(l, Z2_W, 256), _lspec(l, 1, 256), _lspec(l, 1, B_DV),
            sspec,
        ],
        out_specs=[rspec(A_WIDTH), rspec(B_WIDTH), rspec(A_WIDTH),
                   pl.BlockSpec((DEC_BATCH, B_HEADS, B_DK, B_DV), lambda i: (0, 0, 0, 0))],
        out_shape=[row(A_WIDTH), row(B_WIDTH), row(A_WIDTH),
                   jax.ShapeDtypeStruct((DEC_BATCH, B_HEADS, B_DK, B_DV), F32)],
        compiler_params=_cp(("arbitrary",)),
        name="sample_ab",
    )(z1, z1, z1, z1, z1, z1, z2, lng, lnb, w00, b0, gw2p, gb, on_g, state_gla)


def _m5s_table():
    m5 = np.zeros((N_SEG, SLC_LANES), np.float32)
    for j in range(N_PAST_BLK + 1):
        for m in range(max(4 * j, 1), min(4 * j + 4, N_SEG - 1) + 1):
            m5[m, j] = 1.0
    return m5


def _rope_tables(pos):
    half = C_ROT // 2
    inv = jnp.float32(ROPE_THETA) ** (-jnp.arange(half, dtype=F32) / half)
    ang = pos.astype(F32)[:, None] * inv[None, :]
    cos = jnp.cos(ang)
    sin = jnp.sin(ang)
    n = pos.shape[0]
    ones = jnp.ones((n, C_HD - C_ROT), F32)
    zeros = jnp.zeros((n, C_HD - half), F32)
    c = jnp.concatenate([cos, cos, ones], axis=1)
    s1 = jnp.concatenate([-sin, zeros], axis=1)
    s2 = jnp.concatenate([jnp.zeros((n, half), F32), sin, jnp.zeros((n, C_HD - C_ROT), F32)], axis=1)
    return c, s1, s2


def _m5_table():
    m5 = np.zeros((N_CMP, N_CMP), np.float32)
    for j in range(SEQ // SLC_BLK):
        for n in range(4 * j - 1, 4 * j + 4):
            if 0 <= n < N_CMP - 1:
                m5[n, j] = 1.0
    return m5


def _permute_w_in(w_in):
    o = np.cumsum([0, 512, 512, 256, 256, 512, 16, 512, 1024, 1536, 24, 6144])
    seg = lambda i: w_in[..., o[i]:o[i + 1]]
    w1 = jnp.concatenate([seg(7), seg(0), seg(1), seg(4), seg(6), seg(8), seg(2), seg(3), seg(10)], axis=-1)
    pad = jnp.zeros(w_in.shape[:-1] + (Z2_W - 40,), w_in.dtype)
    w2 = jnp.concatenate([seg(5), seg(9), pad], axis=-1)
    return w1.astype(BF16), w2.astype(BF16)


def kernel(x_prompt, x_sample, cache_nsa_kv, state_win_kv, state_gla, page_table, c_prompt, c_sample, ada_w, ada_b, norm1_g, norm2_g, w_in, a_ln_g, a_ln_b, a_ws, a_bs, b_gw2, b_gb, b_on_g, c_qn_g, c_kn_g, cmp_pool, cmp_w1, cmp_b1, cmp_w2, cmp_b2, w_br_a, w_br_b, w_br_c, w_out, p_wq, p_qn_g, p_subkeys, p_u, p_v):
    tm_p, tm_s = 512, T_S
    w1_all, w2_all = _permute_w_in(w_in)
    wa_all = w_br_a.astype(BF16)
    wb_all = w_br_b.astype(BF16)
    wc_all = w_br_c.astype(BF16)
    wo_all = w_out.astype(BF16)
    wq_all = p_wq.astype(BF16)
    u_all = p_u.astype(BF16)
    v_all = p_v.astype(BF16)
    row3 = lambda a: a.reshape(DEPTH, 1, a.shape[-1])
    g1_all, g2_all = row3(norm1_g), row3(norm2_g)
    lng_all, lnb_all = row3(a_ln_g), row3(a_ln_b)
    bst_all = jnp.swapaxes(a_bs, 1, 2)
    gw2p_all = jnp.pad(b_gw2, ((0, 0), (0, Z2_W - B_GATE_RANK), (0, 0))).astype(BF16)
    gb_all, on_all = row3(b_gb), row3(b_on_g)
    gq_all, pqg_all = row3(c_qn_g), row3(p_qn_g)
    cb1_all = cmp_b1.reshape(DEPTH, 2, 1, C_HD)
    cb2_all = cmp_b2.reshape(DEPTH, 2, 1, C_HD)
    c40 = jnp.concatenate([jnp.repeat(c_prompt, N_SEQ_ROWS, axis=0), c_sample], axis=0)
    mod = adaln_table(c40, ada_w, ada_b)
    rope_p = _rope_tables(jnp.arange(SEQ, dtype=jnp.int32))
    rope_s = _rope_tables(jnp.full((T_S,), PAST_LEN, jnp.int32))
    m5 = jnp.asarray(_m5_table().T, BF16)
    xp = x_prompt.reshape(T_P, D_MODEL)
    xs = jnp.pad(x_sample.reshape(DEC_BATCH, D_MODEL), ((0, T_S - DEC_BATCH), (0, 0)))
    pool_w = lambda p: jnp.broadcast_to(jnp.swapaxes(p, 1, 2)[:, :, :, None, None], (DEPTH, CMP_STRIDE, 2, C_KV_HEADS, C_HD))
    wf_all = pool_w(cmp_pool[:, :, :CMP_STRIDE])
    ws_all = pool_w(cmp_pool[:, :, CMP_STRIDE:])
    w00_all = jnp.repeat(a_ws[:, :, 0, 0], A_GW, axis=1)[:, None, :]
    b0_all = jnp.repeat(a_bs[:, :, 0], A_GW, axis=1)[:, None, :]
    m5s = jnp.asarray(_m5s_table(), BF16)
    ckv_s = sample_cmp_mlp(sample_pool(cache_nsa_kv, page_table.reshape(-1), wf_all, ws_all), cmp_w1, cb1_all, cmp_w2, cb2_all)

    outs = {k: [] for k in ('kv_p', 'kv_s', 'win_p', 'win_s', 'gla_p', 'gla_s', 'cv_s')}
    yp = ys = None
    for l in range(DEPTH):
        streams = []
        for is_s, x, y, tm in ((False, xp, yp, tm_p), (True, xs, ys, tm_s)):
            if l == 0:
                h = norm_mod(x, g1_all, mod, l, 0, 1, tm, is_s)
            else:
                x, h = resid_norm_mod(x, y, g1_all, mod, l - 1, l, 5, 0, 1, tm, is_s)
            z1 = matmul(h, w1_all, l, tm, 1024, BF16, "w_in")
            z2 = matmul(h, w2_all, l, tm, Z2_W, F32, "w_in_gates")
            rope = rope_s if is_s else rope_p
            qn, kva, kvb = qk_prep(z1, gq_all, c_kn_g, l, rope[0], rope[1], rope[2], tm, is_s)
            streams.append((x, z1, z2, qn, kva, kvb))
        x, z1, z2, qn, kva, kvb = streams[0]
        o_a = chunk_mlp_prompt(z1, lng_all, lnb_all, a_ws, bst_all, l)
        o_b, s_p = gla_prompt(z1, z2, gw2p_all, gb_all, on_all, l)
        ckv = compress_prompt(kva, cmp_pool, cmp_w1, cb1_all, cmp_w2, cb2_all, l)
        o_cmp, selt = cmp_select_prompt(qn, ckv, z2, m5)
        o_slc = flash_prompt(qn, kva, selt, z2, 1)
        o_win = flash_prompt(qn, kvb, selt, z2, 2)
        mixed_p = (o_a, o_b, o_cmp, o_slc, o_win)
        outs['kv_p'].append(kva.reshape(BATCH, SEQ, 4, C_KV_HEADS, C_HD))
        outs['win_p'].append(kvb.reshape(BATCH, SEQ, 2, C_KV_HEADS, C_HD)[:, SEQ - WINDOW:])
        outs['gla_p'].append(s_p)
        xs_, z1s, z2s, qns, kvas, kvbs = streams[1]
        oa_s, ob_s, av_s, s_s = sample_ab(z1s, z2s, lng_all, lnb_all, w00_all, b0_all, gw2p_all, gb_all, on_all, state_gla, l)
        ocmp_s, idx, ok = sample_cmp(qns, ckv_s, z2s, m5s, l)
        nsel = C_KV_HEADS * SLC_TOPK
        idx_f = idx[:, 0, :nsel].reshape(-1)
        ok_f = ok[:, 0, :nsel].reshape(-1)
        jc = jnp.minimum(idx_f, N_PAST_BLK - 1)
        bpp = PAGE_SIZE // SLC_BLK
        phys = page_table[jnp.repeat(jnp.arange(DEC_BATCH), nsel), jc // bpp]
        oslc_s, owin_s = sample_attend(cache_nsa_kv, state_win_kv, phys, jc % bpp, idx_f, ok_f, qns, kvas, kvbs, z2s, l)
        pad_s = lambda a: jnp.pad(a, ((0, T_S - a.shape[0]), (0, 0))).astype(BF16)
        heads = lambda o: o[:, :, :C_REP].reshape(DEC_BATCH, C_WIDTH)
        mixed_s = (pad_s(oa_s), pad_s(ob_s), pad_s(ocmp_s.reshape(DEC_BATCH, C_WIDTH)), pad_s(heads(oslc_s)), pad_s(heads(owin_s)))
        outs['kv_s'].append(kvas[:DEC_BATCH].reshape(DEC_BATCH, 1, 4, C_KV_HEADS, C_HD))
        outs['win_s'].append(kvbs[:DEC_BATCH].reshape(DEC_BATCH, 1, 2, C_KV_HEADS, C_HD))
        outs['gla_s'].append(s_s)
        outs['cv_s'].append(av_s[:DEC_BATCH, None, :])
        new = []
        for is_s, st, mixed, tm in ((False, streams[0], mixed_p, tm_p), (True, streams[1], mixed_s, tm_s)):
            x, z1 = st[0], st[1]
            mg = merge(*mixed, z1, wa_all, wb_all, wc_all, l, tm)
            att = matmul(mg, wo_all, l, tm, 1024, F32, "w_out")
            x1, h2 = resid_norm_mod(x, att, g2_all, mod, l, l, 2, 3, 4, tm, is_s)
            pq = peer_query(h2, wq_all, pqg_all, l, tm)
            r2, cnt, e1, e2 = peer_topk(pq, p_subkeys, l, tm)
            y = peer_main(h2, u_all, v_all, r2, cnt, e1, e2, l, tm)
            new.append((x1, y))
        (xp, yp), (xs, ys) = new
    xp = resid(xp, yp, mod, DEPTH - 1, 5, tm_p, False)
    xs = resid(xs, ys, mod, DEPTH - 1, 5, tm_s, True)
    return (
        xp.reshape(BATCH, SEQ, D_MODEL),
        xs[:DEC_BATCH].reshape(DEC_BATCH, 1, D_MODEL),
        jnp.stack(outs['kv_p']),
        jnp.stack(outs['kv_s']),
        jnp.stack(outs['win_p']),
        jnp.stack(outs['win_s']),
        jnp.stack(outs['gla_p']),
        jnp.stack(outs['gla_s']),
        jnp.stack(outs['cv_s']),
    )
```
